```python
import jax, jax.numpy as jnp
from jax import lax
import numpy as np

D_MODEL = 1024
BATCH = 8
SEQ = 4096
DEPTH = 1

HEAD_DIM = 64
ATTN_HEADS = 8
ATTN_WIDTH = ATTN_HEADS * HEAD_DIM
RWKV_HEADS = 8
RWKV_WIDTH = RWKV_HEADS * HEAD_DIM
MIX_WIDTH = ATTN_WIDTH + RWKV_WIDTH
DECAY_RANK = 64
ICLR_RANK = 64
MOBA_BLOCK = 256
MOBA_TOPK = 3
Q_CHUNK = 32
ROPE_THETA = 10000.0
RMS_EPS = 1e-6
GN_EPS = 64e-5
NORMALIZE_EPS_SQ = 1e-24

SHIFT_WIDTH = 3 * RWKV_WIDTH + DECAY_RANK + ICLR_RANK
IN_WIDTH = 4 * ATTN_WIDTH + SHIFT_WIDTH + RWKV_WIDTH
B_SHIFT_START = 4 * ATTN_WIDTH
B_GATE_START = B_SHIFT_START + SHIFT_WIDTH

kernel_name = 'hybrid_moba_rwkv7_parallel_heads'


def rms_norm(x, gain):
    xf = x.astype(jnp.float32)
    return xf * lax.rsqrt(jnp.mean(xf * xf, axis=-1, keepdims=True) + RMS_EPS) * gain.astype(jnp.float32)


def rotary(x):
    t, d = x.shape[2], x.shape[3]
    inv_freq = 1.0 / (ROPE_THETA ** (jnp.arange(0, d, 2, dtype=jnp.float32) / d))
    ang = jnp.arange(t, dtype=jnp.float32)[:, None] * inv_freq[None, :]
    cos, sin = jnp.cos(ang), jnp.sin(ang)
    x1, x2 = x[..., : d // 2], x[..., d // 2:]
    return jnp.concatenate([x1 * cos - x2 * sin, x2 * cos + x1 * sin], axis=-1)


def moba_attention(q, k, v):
    b, h, t, d = q.shape
    nb = -(-t // MOBA_BLOCK)
    t_pad = nb * MOBA_BLOCK
    pad = ((0, 0), (0, 0), (0, t_pad - t), (0, 0))
    q, k, v = jnp.pad(q, pad), jnp.pad(k, pad), jnp.pad(v, pad)
    kb = k.reshape(b, h, nb, MOBA_BLOCK, d)
    vb = v.reshape(b, h, nb, MOBA_BLOCK, d)
    k_mean = kb.mean(axis=3)
    n_sel = min(MOBA_TOPK, nb)
    scale = d ** -0.5
    b_idx = jnp.arange(b)[:, None, None, None]
    h_idx = jnp.arange(h)[None, :, None, None]
    block_ids = jnp.arange(nb)

    def query_chunk(i):
        start = i * Q_CHUNK
        cur = start // MOBA_BLOCK
        qc = lax.dynamic_slice_in_dim(q, start, Q_CHUNK, axis=2)
        gate = jnp.einsum('bhqd,bhnd->bhqn', qc, k_mean)
        gate = jnp.where(block_ids < cur, gate, -jnp.inf)
        _, sel = lax.top_k(gate, n_sel)
        sel_ok = jnp.arange(n_sel) < cur
        k_sel = kb[b_idx, h_idx, sel]
        v_sel = vb[b_idx, h_idx, sel]
        s_sel = jnp.einsum('bhqd,bhqjsd->bhqjs', qc, k_sel) * scale
        s_sel = jnp.where(sel_ok[:, None], s_sel, -jnp.inf)
        k_own = lax.dynamic_index_in_dim(kb, cur, axis=2, keepdims=False)
        v_own = lax.dynamic_index_in_dim(vb, cur, axis=2, keepdims=False)
        s_own = jnp.einsum('bhqd,bhsd->bhqs', qc, k_own) * scale
        q_pos = start + jnp.arange(Q_CHUNK)
        k_pos = cur * MOBA_BLOCK + jnp.arange(MOBA_BLOCK)
        s_own = jnp.where(k_pos[None, :] <= q_pos[:, None], s_own, -jnp.inf)
        scores = jnp.concatenate([s_sel.reshape(b, h, Q_CHUNK, n_sel * MOBA_BLOCK), s_own], axis=-1)
        probs = jax.nn.softmax(scores, axis=-1)
        p_sel = probs[..., : n_sel * MOBA_BLOCK].reshape(b, h, Q_CHUNK, n_sel, MOBA_BLOCK)
        p_own = probs[..., n_sel * MOBA_BLOCK:]
        return (jnp.einsum('bhqjs,bhqjsd->bhqd', p_sel, v_sel)
                + jnp.einsum('bhqs,bhsd->bhqd', p_own, v_own))

    out = lax.map(query_chunk, jnp.arange(t_pad // Q_CHUNK))
    out = jnp.moveaxis(out, 0, 2).reshape(b, h, t_pad, d)
    return out[:, :, :t]


def rwkv7_time_mix(p, shift_mu, decay_bias, decay_up, iclr_bias, iclr_up, k_k, k_a, r_k, gn_gain, gn_bias):
    f32 = jnp.float32
    p = p.astype(f32)
    b, t, _ = p.shape
    c = RWKV_WIDTH
    prev = jnp.pad(p[:, :-1], ((0, 0), (1, 0), (0, 0)))
    p = p + (prev - p) * shift_mu.astype(f32)
    r, k, v = p[..., :c], p[..., c:2 * c], p[..., 2 * c:3 * c]
    w_down = p[..., 3 * c:3 * c + DECAY_RANK]
    a_down = p[..., 3 * c + DECAY_RANK:]
    w_log = -jax.nn.softplus(-(decay_bias.astype(f32) + jnp.tanh(w_down) @ decay_up.astype(f32))) - 0.5
    decay = jnp.exp(-jnp.exp(w_log))
    a = jax.nn.sigmoid(iclr_bias.astype(f32) + a_down @ iclr_up.astype(f32))
    kk = k * k_k.astype(f32)
    k = k * (1.0 + (a - 1.0) * k_a.astype(f32))
    heads = lambda z: z.reshape(b, t, RWKV_HEADS, HEAD_DIM)
    r, k, v, decay, a, kk = (heads(z) for z in (r, k, v, decay, a, kk))
    kk = kk * lax.rsqrt(jnp.maximum(jnp.sum(kk * kk, axis=-1, keepdims=True), NORMALIZE_EPS_SQ))

    def step(state, inp):
        r_t, w_t, k_t, v_t, kk_t, a_t = inp
        sa = jnp.einsum('bhvk,bhk->bhv', state, -kk_t)
        state = (state * w_t[:, :, None, :]
                 + sa[..., None] * (kk_t * a_t)[:, :, None, :]
                 + v_t[..., None] * k_t[:, :, None, :])
        return state, jnp.einsum('bhvk,bhk->bhv', state, r_t)

    seq_first = lambda z: jnp.moveaxis(z, 1, 0)
    state0 = jnp.zeros((b, RWKV_HEADS, HEAD_DIM, HEAD_DIM), f32)
    _, y = lax.scan(step, state0, tuple(seq_first(z) for z in (r, decay, k, v, kk, a)))
    y = jnp.moveaxis(y, 0, 1)
    mu = jnp.mean(y, axis=-1, keepdims=True)
    var = jnp.mean(jnp.square(y - mu), axis=-1, keepdims=True)
    y = ((y - mu) * lax.rsqrt(var + GN_EPS)).reshape(b, t, c) * gn_gain.astype(f32) + gn_bias.astype(f32)
    bonus = jnp.sum(r * k * r_k.astype(f32), axis=-1, keepdims=True) * v
    return y + bonus.reshape(b, t, c)


def setup_inputs(seed: int = 0) -> dict:
    key = jax.random.key(seed)
    ks = jax.random.split(key, 16)
    f32 = jnp.float32
    nrm = lambda kk, shape, s: s * jax.random.normal(kk, shape, f32)
    L = DEPTH
    return {
        'x': jax.random.normal(ks[0], (BATCH, SEQ, D_MODEL), f32),
        'norm_gain': 1.0 + nrm(ks[1], (L, D_MODEL), 0.02),
        'w_in': nrm(ks[2], (L, D_MODEL, IN_WIDTH), D_MODEL ** -0.5),
        'shift_mu': jax.random.uniform(ks[3], (L, SHIFT_WIDTH), f32),
        'decay_bias': nrm(ks[4], (L, RWKV_WIDTH), 0.5) - 2.0,
        'decay_up': nrm(ks[5], (L, DECAY_RANK, RWKV_WIDTH), 0.5 * DECAY_RANK ** -0.5),
        'iclr_bias': nrm(ks[6], (L, RWKV_WIDTH), 0.5),
        'iclr_up': nrm(ks[7], (L, ICLR_RANK, RWKV_WIDTH), ICLR_RANK ** -0.5),
        'k_k': 0.85 + nrm(ks[8], (L, RWKV_WIDTH), 0.1),
        'k_a': 1.0 + nrm(ks[9], (L, RWKV_WIDTH), 0.1),
        'r_k': nrm(ks[10], (L, RWKV_HEADS, HEAD_DIM), 0.1),
        'gn_gain': 1.0 + nrm(ks[11], (L, RWKV_WIDTH), 0.02),
        'gn_bias': nrm(ks[12], (L, RWKV_WIDTH), 0.02),
        'w_out': nrm(ks[13], (L, MIX_WIDTH, D_MODEL), MIX_WIDTH ** -0.5),
        'final_gain': 1.0 + nrm(ks[14], (D_MODEL,), 0.02),
    }


def reference(x, norm_gain, w_in, shift_mu, decay_bias, decay_up, iclr_bias, iclr_up,
              k_k, k_a, r_k, gn_gain, gn_bias, w_out, final_gain):
    f32 = jnp.float32
    b, t, _ = x.shape
    h = x.astype(f32)
    to_heads = lambda z: z.reshape(b, t, ATTN_HEADS, HEAD_DIM).transpose(0, 2, 1, 3)
    for layer in range(DEPTH):
        u = rms_norm(h, norm_gain[layer])
        p = jnp.einsum('btd,dc->btc', u, w_in[layer].astype(f32))
        q = rotary(to_heads(p[..., 0:ATTN_WIDTH]))
        k = rotary(to_heads(p[..., ATTN_WIDTH:2 * ATTN_WIDTH]))
        v = to_heads(p[..., 2 * ATTN_WIDTH:3 * ATTN_WIDTH])
        y_a = moba_attention(q, k, v).transpose(0, 2, 1, 3).reshape(b, t, ATTN_WIDTH)
        y_a = y_a * jax.nn.silu(p[..., 3 * ATTN_WIDTH:4 * ATTN_WIDTH])
        y_b = rwkv7_time_mix(p[..., B_SHIFT_START:B_GATE_START], shift_mu[layer], decay_bias[layer],
                             decay_up[layer], iclr_bias[layer], iclr_up[layer], k_k[layer], k_a[layer],
                             r_k[layer], gn_gain[layer], gn_bias[layer])
        y_b = y_b * jax.nn.silu(p[..., B_GATE_START:])
        h = h + jnp.einsum('btc,cd->btd', jnp.concatenate([y_a, y_b], axis=-1), w_out[layer].astype(f32))
    return rms_norm(h, final_gain).astype(x.dtype)
```

```python
import functools

import jax
import jax.numpy as jnp
import numpy as np
from jax import lax
from jax.experimental import pallas as pl
from jax.experimental.pallas import tpu as pltpu

HEAD_DIM = 64
HALF = HEAD_DIM // 2
ATTN_HEADS = 8
ATTN_WIDTH = ATTN_HEADS * HEAD_DIM
RWKV_HEADS = 8
RWKV_WIDTH = RWKV_HEADS * HEAD_DIM
LORA_RANK = 64
SHIFT_WIDTH = 3 * RWKV_WIDTH + 2 * LORA_RANK
MOBA_BLOCK = 256
MOBA_TOPK = 3
ROPE_THETA = 10000.0
RMS_EPS = 1e-6
GN_EPS = 64e-5
NORMALIZE_EPS_SQ = 1e-24

GROUP_HEADS = 4
GROUP_WIDTH = GROUP_HEADS * HEAD_DIM
CHUNK = 64
MASKED = -1e30
VMEM_LIMIT_BYTES = 48 * 1024 * 1024

F32 = jnp.float32
BF16 = jnp.bfloat16


def _dot(a, b):
    return jnp.dot(a, b, preferred_element_type=F32)


def _dot_nt(a, b):
    return lax.dot_general(a, b, (((1,), (1,)), ((), ())), preferred_element_type=F32)


def _dot_tn(a, b):
    return lax.dot_general(a, b, (((0,), (0,)), ((), ())), preferred_element_type=F32)


def _split_bf16(x):
    hi = x.astype(BF16)
    lo = (x - hi.astype(F32)).astype(BF16)
    return hi, lo


def _silu(z):
    return z / (1.0 + jnp.exp(-z))


def _rotary(acc, cos, sin):
    outs = []
    for g in range(ATTN_WIDTH // GROUP_WIDTH):
        x1 = acc[:, g * GROUP_WIDTH:g * GROUP_WIDTH + 128]
        x2 = acc[:, g * GROUP_WIDTH + 128:(g + 1) * GROUP_WIDTH]
        outs.append(x1 * cos - x2 * sin)
        outs.append(x2 * cos + x1 * sin)
    return jnp.concatenate(outs, axis=-1)


def _in_proj_kernel(x_ref, gain_ref, wnn_ref, wnt_ref, cos_ref, sin_ref,
                    q_ref, k_ref, kmean_ref, vt_ref, gat_ref, pb_ref, gb_ref):
    x = x_ref[...]
    ms = jnp.mean(x * x, axis=-1, keepdims=True)
    u = (x * lax.rsqrt(ms + RMS_EPS) * gain_ref[...]).astype(BF16)
    cos = cos_ref[...]
    sin = sin_ref[...]

    q = _rotary(_dot(u, wnn_ref[:, 0:ATTN_WIDTH]), cos, sin)
    q_ref[...] = (q * (HEAD_DIM ** -0.5)).astype(BF16)
    k = _rotary(_dot(u, wnn_ref[:, ATTN_WIDTH:2 * ATTN_WIDTH]), cos, sin)
    k_ref[...] = k.astype(BF16)
    kmean_ref[0] = jnp.mean(k, axis=0, keepdims=True)

    base = 2 * ATTN_WIDTH
    for c in range(0, SHIFT_WIDTH, 512):
        w = min(512, SHIFT_WIDTH - c)
        pb_ref[:, c:c + w] = _dot(u, wnn_ref[:, base + c:base + c + w])
    gb = _dot(u, wnn_ref[:, base + SHIFT_WIDTH:base + SHIFT_WIDTH + RWKV_WIDTH])
    gb_ref[...] = _silu(gb).astype(BF16)

    vt_ref[0, 0] = _dot_nt(wnt_ref[0:ATTN_WIDTH, :], u).astype(BF16)
    gat = _dot_nt(wnt_ref[ATTN_WIDTH:2 * ATTN_WIDTH, :], u)
    gat_ref[0, 0] = _silu(gat).astype(BF16)


def _in_proj(x2d, gain, wnn, wnt, cos, sin, batch, nb):
    rows, d_model = x2d.shape
    n_nn = wnn.shape[1]
    tile = MOBA_BLOCK
    row_spec = lambda width: pl.BlockSpec((tile, width), lambda i: (i, 0))
    const = lambda shape: pl.BlockSpec(shape, lambda i: (0,) * len(shape))
    blk_t = pl.BlockSpec((1, 1, ATTN_WIDTH, tile), lambda i: (i // nb, i % nb, 0, 0))
    return pl.pallas_call(
        _in_proj_kernel,
        grid=(rows // tile,),
        in_specs=[
            row_spec(d_model),
            const((1, d_model)),
            const((d_model, n_nn)),
            const((2 * ATTN_WIDTH, d_model)),
            pl.BlockSpec((tile, 128), lambda i: (i % nb, 0)),
            pl.BlockSpec((tile, 128), lambda i: (i % nb, 0)),
        ],
        out_specs=[
            row_spec(ATTN_WIDTH),
            row_spec(ATTN_WIDTH),
            pl.BlockSpec((1, 1, ATTN_WIDTH), lambda i: (i, 0, 0)),
            blk_t,
            blk_t,
            row_spec(SHIFT_WIDTH),
            row_spec(RWKV_WIDTH),
        ],
        out_shape=[
            jax.ShapeDtypeStruct((rows, ATTN_WIDTH), BF16),
            jax.ShapeDtypeStruct((rows, ATTN_WIDTH), BF16),
            jax.ShapeDtypeStruct((rows // tile, 1, ATTN_WIDTH), F32),
            jax.ShapeDtypeStruct((batch, nb, ATTN_WIDTH, tile), BF16),
            jax.ShapeDtypeStruct((batch, nb, ATTN_WIDTH, tile), BF16),
            jax.ShapeDtypeStruct((rows, SHIFT_WIDTH), F32),
            jax.ShapeDtypeStruct((rows, RWKV_WIDTH), BF16),
        ],
        compiler_params=pltpu.CompilerParams(
            dimension_semantics=("arbitrary",), vmem_limit_bytes=VMEM_LIMIT_BYTES),
        name="in_proj",
    )(x2d, gain, wnn, wnt, cos, sin)


def _moba_kernel(q_ref, k_ref, kmean_ref, vt_ref, gat_ref, out_ref, bias_ref, m_ref, l_ref, acc_ref, *, nb):
    qi = pl.program_id(2)
    blk = MOBA_BLOCK
    q = q_ref[...]
    lane = lax.broadcasted_iota(jnp.int32, (1, GROUP_WIDTH), 1)
    head_of_lane = (lane % 128) // HALF
    km_hi, km_lo = _split_bf16(kmean_ref[0])
    blk_id = lax.broadcasted_iota(jnp.int32, (nb, blk), 0)
    k_diag = k_ref[0, pl.ds(pl.multiple_of(qi * blk, blk), blk), :]
    k_pos = lax.broadcasted_iota(jnp.int32, (blk, blk), 0)
    q_pos = lax.broadcasted_iota(jnp.int32, (blk, blk), 1)
    causal = k_pos <= q_pos

    q_heads = []
    for h in range(GROUP_HEADS):
        qh = jnp.where(head_of_lane == h, q, jnp.zeros_like(q))
        q_heads.append(qh)
        gate = _dot_nt(km_hi, qh) + _dot_nt(km_lo, qh)
        gate = jnp.where(blk_id < qi, gate, -jnp.inf)
        sel = jnp.zeros((nb, blk), jnp.bool_)
        for r in range(min(MOBA_TOPK, nb)):
            top = jnp.max(gate, axis=0, keepdims=True)
            idx = jnp.min(jnp.where(gate == top, blk_id, nb), axis=0, keepdims=True)
            hit = (blk_id == idx) & (r < qi)
            sel = sel | hit
            gate = jnp.where(blk_id == idx, -jnp.inf, gate)
        bias_ref[h] = jnp.where(sel, 0.0, MASKED).astype(F32)

        s = _dot_nt(k_diag, qh)
        s = jnp.where(causal, s, MASKED)
        m = jnp.max(s, axis=0, keepdims=True)
        p = jnp.exp(s - m)
        m_ref[h] = m
        l_ref[h] = jnp.sum(p, axis=0, keepdims=True)
        acc_ref[h] = _dot(vt_ref[0, qi, h * HEAD_DIM:(h + 1) * HEAD_DIM, :], p.astype(BF16))

    def past_block(ki, carry):
        kb = k_ref[0, pl.ds(pl.multiple_of(ki * blk, blk), blk), :]
        for h in range(GROUP_HEADS):
            s = _dot_nt(kb, q_heads[h]) + bias_ref[h, pl.ds(ki, 1), :]
            m_old = m_ref[h]
            m_new = jnp.maximum(m_old, jnp.max(s, axis=0, keepdims=True))
            alpha = jnp.exp(m_old - m_new)
            p = jnp.exp(s - m_new)
            m_ref[h] = m_new
            l_ref[h] = alpha * l_ref[h] + jnp.sum(p, axis=0, keepdims=True)
            pv = _dot(vt_ref[0, ki, h * HEAD_DIM:(h + 1) * HEAD_DIM, :], p.astype(BF16))
            acc_ref[h] = alpha * acc_ref[h] + pv
        return carry

    lax.fori_loop(0, qi, past_block, 0)

    for h in range(GROUP_HEADS):
        rows = slice(h * HEAD_DIM, (h + 1) * HEAD_DIM)
        y = acc_ref[h] / l_ref[h]
        out_ref[0, 0, rows, :] = (y * gat_ref[0, 0, rows, :].astype(F32)).astype(BF16)


def _moba(q, k, kmean, vt, gat, batch, seq):
    nb = seq // MOBA_BLOCK
    blk = MOBA_BLOCK
    groups = ATTN_WIDTH // GROUP_WIDTH
    k3 = k.reshape(batch, seq, ATTN_WIDTH)
    km3 = kmean.reshape(batch, nb, ATTN_WIDTH)
    tile_t = pl.BlockSpec((1, 1, GROUP_WIDTH, blk), lambda b, g, i: (b, i, g, 0))
    return pl.pallas_call(
        functools.partial(_moba_kernel, nb=nb),
        grid=(batch, groups, nb),
        in_specs=[
            pl.BlockSpec((blk, GROUP_WIDTH), lambda b, g, i: (b * nb + i, g)),
            pl.BlockSpec((1, seq, GROUP_WIDTH), lambda b, g, i: (b, 0, g)),
            pl.BlockSpec((1, nb, GROUP_WIDTH), lambda b, g, i: (b, 0, g)),
            pl.BlockSpec((1, nb, GROUP_WIDTH, blk), lambda b, g, i: (b, 0, g, 0)),
            tile_t,
        ],
        out_specs=tile_t,
        out_shape=jax.ShapeDtypeStruct((batch, nb, ATTN_WIDTH, blk), BF16),
        scratch_shapes=[
            pltpu.VMEM((GROUP_HEADS, nb, blk), F32),
            pltpu.VMEM((GROUP_HEADS, 1, blk), F32),
            pltpu.VMEM((GROUP_HEADS, 1, blk), F32),
            pltpu.VMEM((GROUP_HEADS, HEAD_DIM, blk), F32),
        ],
        compiler_params=pltpu.CompilerParams(
            dimension_semantics=("arbitrary", "arbitrary", "arbitrary"), vmem_limit_bytes=VMEM_LIMIT_BYTES),
        name="moba",
    )(q, k3, km3, vt, gat)


def _block_diag(x_bf16, mask):
    tiled = jnp.concatenate([x_bf16] * GROUP_HEADS, axis=0)
    return jnp.where(mask, tiled, jnp.zeros_like(tiled))


def _rwkv_kernel(pb_ref, gb_ref, mu_ref, wlora_ref, dbias_ref, ibias_ref, kk_ref, ka_ref, rk_ref,
                 gng_ref, gnb_ref, out_ref, state_ref, prev_ref, y_ref):
    tile = pb_ref.shape[0]
    c_w = RWKV_WIDTH

    @pl.when(pl.program_id(1) == 0)
    def _():
        state_ref[...] = jnp.zeros_like(state_ref)
        prev_ref[...] = jnp.zeros_like(prev_ref)

    p = pb_ref[...]
    row = lax.broadcasted_iota(jnp.int32, (tile, 1), 0)
    shifted = jnp.where(row == 0, prev_ref[0:1, :], pltpu.roll(p, 1, axis=0))
    prev_ref[0:1, :] = p[tile - 1:tile, :]
    xs = p + (shifted - p) * mu_ref[...]
    r = xs[:, 0:c_w]
    k = xs[:, c_w:2 * c_w]
    v = xs[:, 2 * c_w:3 * c_w]
    lora_in = xs[:, 3 * c_w:3 * c_w + 2 * LORA_RANK]
    lane128 = lax.broadcasted_iota(jnp.int32, (1, 2 * LORA_RANK), 1)
    lora_in = jnp.where(lane128 < LORA_RANK, jnp.tanh(lora_in), lora_in)
    li_hi, li_lo = _split_bf16(lora_in)
    w_hi = wlora_ref[0]
    w_lo = wlora_ref[1]
    lora = _dot(li_hi, w_hi) + _dot(li_lo, w_hi) + _dot(li_hi, w_lo)
    z = -(dbias_ref[...] + lora[:, 0:c_w])
    softplus = jnp.maximum(z, 0.0) + jnp.log(1.0 + jnp.exp(-jnp.abs(z)))
    g = -jnp.exp(-softplus - 0.5)
    a = 1.0 / (1.0 + jnp.exp(-(ibias_ref[...] + lora[:, c_w:2 * c_w])))

    hr = lax.broadcasted_iota(jnp.int32, (c_w, c_w), 0) // HEAD_DIM
    hc = lax.broadcasted_iota(jnp.int32, (c_w, c_w), 1) // HEAD_DIM
    head_ones = jnp.where(hr == hc, 1.0, 0.0).astype(BF16)

    def head_sum(t):
        hi, lo = _split_bf16(t)
        return _dot(hi, head_ones) + _dot(lo, head_ones)

    kk = k * kk_ref[...]
    kk = kk * lax.rsqrt(jnp.maximum(head_sum(kk * kk), NORMALIZE_EPS_SQ))
    k2 = k * (1.0 + (a - 1.0) * ka_ref[...])
    b = kk * a
    bonus = head_sum(r * k2 * rk_ref[...]) * v

    ti = lax.broadcasted_iota(jnp.int32, (CHUNK, CHUNK), 0)
    tj = lax.broadcasted_iota(jnp.int32, (CHUNK, CHUNK), 1)
    tri_incl = jnp.where(tj <= ti, 1.0, 0.0).astype(BF16)
    g_hi, g_lo = _split_bf16(g)

    gr = lax.broadcasted_iota(jnp.int32, (GROUP_WIDTH, GROUP_WIDTH), 0) // HEAD_DIM
    gc = lax.broadcasted_iota(jnp.int32, (GROUP_WIDTH, GROUP_WIDTH), 1) // HEAD_DIM
    diag_mask = gr == gc
    t_idx = lax.broadcasted_iota(jnp.int32, (CHUNK, GROUP_WIDTH), 0)
    j_idx = lax.broadcasted_iota(jnp.int32, (CHUNK, GROUP_WIDTH), 1) % HEAD_DIM
    strict_lower = j_idx < t_idx
    lower = j_idx <= t_idx
    eye = (j_idx == t_idx).astype(F32)
    lane_head = lax.broadcasted_iota(jnp.int32, (1, GROUP_WIDTH), 1) // HEAD_DIM

    for c in range(tile // CHUNK):
        rs = slice(c * CHUNK, (c + 1) * CHUNK)
        cum = _dot(tri_incl, g_hi[rs]) + _dot(tri_incl, g_lo[rs])
        cum_last = cum[CHUNK - 1:CHUNK, :]
        dec_in = jnp.exp(cum)
        dec_ex = jnp.exp(cum - g[rs])
        inv = jnp.exp(-cum)
        to_end = jnp.exp(cum_last - cum)
        end_decay = jnp.exp(cum_last)
        for gi in range(c_w // GROUP_WIDTH):
            ls = slice(gi * GROUP_WIDTH, (gi + 1) * GROUP_WIDTH)
            a_t = (-kk[rs, ls] * dec_ex[:, ls]).astype(BF16)
            r_t = (r[rs, ls] * dec_in[:, ls]).astype(BF16)
            b_t = (b[rs, ls] * inv[:, ls]).astype(BF16)
            k_t = (k2[rs, ls] * inv[:, ls]).astype(BF16)
            b_e = (b[rs, ls] * to_end[:, ls]).astype(BF16)
            k_e = (k2[rs, ls] * to_end[:, ls]).astype(BF16)
            v_c = v[rs, ls].astype(BF16)

            ar = jnp.concatenate([a_t, r_t], axis=0)
            pb = _dot_nt(ar, _block_diag(b_t, diag_mask))
            pk = _dot_nt(ar, _block_diag(k_t, diag_mask))
            n_ab = jnp.where(strict_lower, pb[0:CHUNK], 0.0)
            a_ak = jnp.where(strict_lower, pk[0:CHUNK], 0.0).astype(BF16)
            a_rb = jnp.where(lower, pb[CHUNK:], 0.0).astype(BF16)
            a_rk = jnp.where(lower, pk[CHUNK:], 0.0).astype(BF16)

            pw = n_ab
            t_inv = eye + n_ab
            steps = CHUNK.bit_length() - 1
            for s in range(1, steps):
                w_p = _block_diag(pw.astype(BF16), diag_mask)
                if s == 1:
                    pw = _dot(pw.astype(BF16), w_p)
                else:
                    both = _dot(jnp.concatenate([pw, t_inv], axis=0).astype(BF16), w_p)
                    t_inv = t_inv + both[CHUNK:]
                    pw = both[0:CHUNK]
            t_inv = t_inv + _dot(t_inv.astype(BF16), _block_diag(pw.astype(BF16), diag_mask))

            st = state_ref[gi]
            w_s = _block_diag(st.astype(BF16), diag_mask)
            w_v = _block_diag(v_c, diag_mask)
            ar_s = _dot_nt(ar, w_s)
            x = ar_s[0:CHUNK] + _dot(a_ak, w_v)
            u = _dot(t_inv.astype(BF16), _block_diag(x.astype(BF16), diag_mask))
            u_bf = u.astype(BF16)
            y = ar_s[CHUNK:] + _dot(a_rb, _block_diag(u_bf, diag_mask)) + _dot(a_rk, w_v)
            y_ref[rs, ls] = y

            full = _dot_tn(jnp.concatenate([u_bf, v_c], axis=0),
                           jnp.concatenate([b_e, k_e], axis=0))
            new = st * end_decay[:, ls]
            for h in range(GROUP_HEADS):
                new = new + jnp.where(lane_head == h, full[h * HEAD_DIM:(h + 1) * HEAD_DIM, :], 0.0)
            state_ref[gi] = new

    y = y_ref[...]
    mean = head_sum(y) * (1.0 / HEAD_DIM)
    d = y - mean
    var = head_sum(d * d) * (1.0 / HEAD_DIM)
    yn = d * lax.rsqrt(var + GN_EPS) * gng_ref[...] + gnb_ref[...]
    out_ref[...] = ((yn + bonus) * gb_ref[...].astype(F32)).astype(BF16)


def _rwkv(pb, gb, mu, wlora, dbias, ibias, k_k, k_a, r_k, gn_gain, gn_bias, batch, seq):
    tile = MOBA_BLOCK
    nt = seq // tile
    row_spec = lambda width: pl.BlockSpec((tile, width), lambda b, i: (b * nt + i, 0))
    vec = lambda width: pl.BlockSpec((1, width), lambda b, i: (0, 0))
    return pl.pallas_call(
        _rwkv_kernel,
        grid=(batch, nt),
        in_specs=[
            row_spec(SHIFT_WIDTH),
            row_spec(RWKV_WIDTH),
            vec(SHIFT_WIDTH),
            pl.BlockSpec((2, 2 * LORA_RANK, 2 * RWKV_WIDTH), lambda b, i: (0, 0, 0)),
        ] + [vec(RWKV_WIDTH)] * 7,
        out_specs=row_spec(RWKV_WIDTH),
        out_shape=jax.ShapeDtypeStruct((batch * seq, RWKV_WIDTH), BF16),
        scratch_shapes=[
            pltpu.VMEM((RWKV_WIDTH // GROUP_WIDTH, HEAD_DIM, GROUP_WIDTH), F32),
            pltpu.VMEM((8, SHIFT_WIDTH), F32),
            pltpu.VMEM((tile, RWKV_WIDTH), F32),
        ],
        compiler_params=pltpu.CompilerParams(
            dimension_semantics=("arbitrary", "arbitrary"), vmem_limit_bytes=VMEM_LIMIT_BYTES),
        name="rwkv",
    )(pb, gb, mu, wlora, dbias, ibias, k_k, k_a, r_k, gn_gain, gn_bias)


def _out_proj_kernel(yat_ref, yb_ref, x_ref, woa_ref, wob_ref, gain_ref, out_ref):
    d = _dot_tn(yat_ref[0, 0], woa_ref[...]) + _dot(yb_ref[...], wob_ref[...])
    h = x_ref[...] + d
    ms = jnp.mean(h * h, axis=-1, keepdims=True)
    out_ref[...] = (h * lax.rsqrt(ms + RMS_EPS) * gain_ref[...]).astype(out_ref.dtype)


def _out_proj(yat, yb, x2d, woa, wob, gain, nb):
    rows, d_model = x2d.shape
    tile = MOBA_BLOCK
    row_spec = lambda width: pl.BlockSpec((tile, width), lambda i: (i, 0))
    const = lambda shape: pl.BlockSpec(shape, lambda i: (0,) * len(shape))
    return pl.pallas_call(
        _out_proj_kernel,
        grid=(rows // tile,),
        in_specs=[
            pl.BlockSpec((1, 1, ATTN_WIDTH, tile), lambda i: (i // nb, i % nb, 0, 0)),
            row_spec(RWKV_WIDTH),
            row_spec(d_model),
            const((ATTN_WIDTH, d_model)),
            const((RWKV_WIDTH, d_model)),
            const((1, d_model)),
        ],
        out_specs=row_spec(d_model),
        out_shape=jax.ShapeDtypeStruct((rows, d_model), x2d.dtype),
        compiler_params=pltpu.CompilerParams(
            dimension_semantics=("arbitrary",), vmem_limit_bytes=VMEM_LIMIT_BYTES),
        name="out_proj",
    )(yat, yb, x2d, woa, wob, gain)


def _rotary_column_order():
    order = []
    for g in range(ATTN_WIDTH // GROUP_WIDTH):
        for part in range(2):
            for hh in range(GROUP_HEADS):
                head = g * GROUP_HEADS + hh
                order.extend(head * HEAD_DIM + part * HALF + j for j in range(HALF))
    return np.asarray(order, np.int32)


def _layer(h2d, batch, seq, norm_gain, w_in, shift_mu, decay_bias, decay_up, iclr_bias, iclr_up,
           k_k, k_a, r_k, gn_gain, gn_bias, w_out, out_gain, cos, sin):
    nb = seq // MOBA_BLOCK
    aw, rw = ATTN_WIDTH, RWKV_WIDTH
    order = _rotary_column_order()
    b0 = 4 * aw
    w_in = w_in.astype(F32)
    wnn = jnp.concatenate([w_in[:, 0:aw][:, order], w_in[:, aw:2 * aw][:, order],
                           w_in[:, b0:b0 + SHIFT_WIDTH], w_in[:, b0 + SHIFT_WIDTH:]], axis=1).astype(BF16)
    wnt = w_in[:, 2 * aw:4 * aw].T.astype(BF16)
    zeros = jnp.zeros((LORA_RANK, rw), F32)
    wl = jnp.concatenate([jnp.concatenate([decay_up.astype(F32), zeros], axis=1),
                          jnp.concatenate([zeros, iclr_up.astype(F32)], axis=1)], axis=0)
    wl_hi = wl.astype(BF16)
    wl_lo = (wl - wl_hi.astype(F32)).astype(BF16)
    wlora = jnp.stack([wl_hi, wl_lo])
    row = lambda t: t.astype(F32).reshape(1, -1)

    q, k, kmean, vt, gat, pb, gb = _in_proj(h2d, row(norm_gain), wnn, wnt, cos, sin, batch, nb)
    yat = _moba(q, k, kmean, vt, gat, batch, seq)
    yb = _rwkv(pb, gb, row(shift_mu), wlora, row(decay_bias), row(iclr_bias), row(k_k), row(k_a),
               row(r_k), row(gn_gain), row(gn_bias), batch, seq)
    w_out = w_out.astype(BF16)
    return _out_proj(yat, yb, h2d, w_out[0:aw], w_out[aw:], out_gain, nb)


def kernel(x, norm_gain, w_in, shift_mu, decay_bias, decay_up, iclr_bias, iclr_up,
           k_k, k_a, r_k, gn_gain, gn_bias, w_out, final_gain):
    batch, seq, d_model = x.shape
    depth = norm_gain.shape[0]
    assert depth == 1, "the final RMSNorm is fused into the single layer's output projection"
    assert seq % MOBA_BLOCK == 0 and w_in.shape[-1] == 4 * ATTN_WIDTH + SHIFT_WIDTH + RWKV_WIDTH
    inv_freq = 1.0 / (ROPE_THETA ** (jnp.arange(0, HEAD_DIM, 2, dtype=F32) / HEAD_DIM))
    ang = jnp.arange(seq, dtype=F32)[:, None] * inv_freq[None, :]
    cos = jnp.tile(jnp.cos(ang), (1, GROUP_HEADS))
    sin = jnp.tile(jnp.sin(ang), (1, GROUP_HEADS))
    out = _layer(x.reshape(batch * seq, d_model).astype(F32), batch, seq, norm_gain[0], w_in[0],
                 shift_mu[0], decay_bias[0], decay_up[0], iclr_bias[0], iclr_up[0], k_k[0], k_a[0], r_k[0],
                 gn_gain[0], gn_bias[0], w_out[0], final_gain.astype(F32).reshape(1, -1), cos, sin)
    return out.reshape(batch, seq, d_model).astype(x.dtype)
```

```python
import functools

import jax
import jax.numpy as jnp
import numpy as np
from jax import lax
from jax.experimental import pallas as pl
from jax.experimental.pallas import tpu as pltpu

HEAD_DIM = 64
HALF = HEAD_DIM // 2
ATTN_HEADS = 8
ATTN_WIDTH = ATTN_HEADS * HEAD_DIM
RWKV_HEADS = 8
RWKV_WIDTH = RWKV_HEADS * HEAD_DIM
LORA_RANK = 64
SHIFT_WIDTH = 3 * RWKV_WIDTH + 2 * LORA_RANK
MOBA_BLOCK = 256
MOBA_TOPK = 3
ROPE_THETA = 10000.0
RMS_EPS = 1e-6
GN_EPS = 64e-5
NORMALIZE_EPS_SQ = 1e-24

GROUP_HEADS = 4
GROUP_WIDTH = GROUP_HEADS * HEAD_DIM
CHUNK = 64
MASKED = -1e30
ACC_ROWS = HEAD_DIM + 16
LOG2E = 1.4426950408889634
Q_SCALE = HEAD_DIM ** -0.5 * LOG2E
VMEM_LIMIT_BYTES = 48 * 1024 * 1024

F32 = jnp.float32
BF16 = jnp.bfloat16


def _dot(a, b):
    return jnp.dot(a, b, preferred_element_type=F32)


def _dot_nt(a, b):
    return lax.dot_general(a, b, (((1,), (1,)), ((), ())), preferred_element_type=F32)


def _dot_tn(a, b):
    return lax.dot_general(a, b, (((0,), (0,)), ((), ())), preferred_element_type=F32)


def _split_bf16(x):
    hi = x.astype(BF16)
    lo = (x - hi.astype(F32)).astype(BF16)
    return hi, lo


def _silu(z):
    return z / (1.0 + jnp.exp(-z))


def _rotary(acc, cos, sin):
    outs = []
    for g in range(ATTN_WIDTH // GROUP_WIDTH):
        x1 = acc[:, g * GROUP_WIDTH:g * GROUP_WIDTH + 128]
        x2 = acc[:, g * GROUP_WIDTH + 128:(g + 1) * GROUP_WIDTH]
        outs.append(x1 * cos - x2 * sin)
        outs.append(x2 * cos + x1 * sin)
    return jnp.concatenate(outs, axis=-1)


def _rotary_t(acc, cos_t, sin_t):
    outs = []
    for g in range(ATTN_WIDTH // GROUP_WIDTH):
        x1 = acc[g * GROUP_WIDTH:g * GROUP_WIDTH + 128, :]
        x2 = acc[g * GROUP_WIDTH + 128:(g + 1) * GROUP_WIDTH, :]
        outs.append(x1 * cos_t - x2 * sin_t)
        outs.append(x2 * cos_t + x1 * sin_t)
    return jnp.concatenate(outs, axis=0)


def _in_proj_kernel(x_ref, gain_ref, wnn_ref, wnt_ref, cos_ref, sin_ref, cost_ref, sint_ref,
                    qt_ref, k_ref, kmean_ref, vt_ref, gat_ref, pb_ref, gb_ref):
    x = x_ref[...]
    ms = jnp.mean(x * x, axis=-1, keepdims=True)
    u = (x * lax.rsqrt(ms + RMS_EPS) * gain_ref[...]).astype(BF16)

    k = _rotary(_dot(u, wnn_ref[:, 0:ATTN_WIDTH]), cos_ref[...], sin_ref[...])
    k_ref[...] = k.astype(BF16)
    kmean_ref[0] = jnp.mean(k, axis=0, keepdims=True)

    base = ATTN_WIDTH
    for c in range(0, SHIFT_WIDTH, 512):
        w = min(512, SHIFT_WIDTH - c)
        pb_ref[:, c:c + w] = _dot(u, wnn_ref[:, base + c:base + c + w])
    gb = _dot(u, wnn_ref[:, base + SHIFT_WIDTH:base + SHIFT_WIDTH + RWKV_WIDTH])
    gb_ref[...] = _silu(gb).astype(BF16)

    qt = _rotary_t(_dot_nt(wnt_ref[0:ATTN_WIDTH, :], u), cost_ref[0], sint_ref[0])
    qt_ref[0, 0] = (qt * Q_SCALE).astype(BF16)
    vt_ref[0, 0] = _dot_nt(wnt_ref[ATTN_WIDTH:2 * ATTN_WIDTH, :], u).astype(BF16)
    gat = _dot_nt(wnt_ref[2 * ATTN_WIDTH:3 * ATTN_WIDTH, :], u)
    gat_ref[0, 0] = _silu(gat).astype(BF16)


def _in_proj(x2d, gain, wnn, wnt, cos, sin, cos_t, sin_t, batch, nb):
    rows, d_model = x2d.shape
    tile = MOBA_BLOCK
    row_spec = lambda width: pl.BlockSpec((tile, width), lambda i: (i, 0))
    const = lambda shape: pl.BlockSpec(shape, lambda i: (0,) * len(shape))
    blk_t = pl.BlockSpec((1, 1, ATTN_WIDTH, tile), lambda i: (i // nb, i % nb, 0, 0))
    blk_t_shape = jax.ShapeDtypeStruct((batch, nb, ATTN_WIDTH, tile), BF16)
    return pl.pallas_call(
        _in_proj_kernel,
        grid=(rows // tile,),
        in_specs=[
            row_spec(d_model),
            const((1, d_model)),
            const(wnn.shape),
            const(wnt.shape),
            pl.BlockSpec((tile, 128), lambda i: (i % nb, 0)),
            pl.BlockSpec((tile, 128), lambda i: (i % nb, 0)),
            pl.BlockSpec((1, 128, tile), lambda i: (i % nb, 0, 0)),
            pl.BlockSpec((1, 128, tile), lambda i: (i % nb, 0, 0)),
        ],
        out_specs=[
            blk_t,
            row_spec(ATTN_WIDTH),
            pl.BlockSpec((1, 1, ATTN_WIDTH), lambda i: (i, 0, 0)),
            blk_t,
            blk_t,
            row_spec(SHIFT_WIDTH),
            row_spec(RWKV_WIDTH),
        ],
        out_shape=[
            blk_t_shape,
            jax.ShapeDtypeStruct((rows, ATTN_WIDTH), BF16),
            jax.ShapeDtypeStruct((rows // tile, 1, ATTN_WIDTH), F32),
            blk_t_shape,
            blk_t_shape,
            jax.ShapeDtypeStruct((rows, SHIFT_WIDTH), F32),
            jax.ShapeDtypeStruct((rows, RWKV_WIDTH), BF16),
        ],
        compiler_params=pltpu.CompilerParams(
            dimension_semantics=("arbitrary",), vmem_limit_bytes=VMEM_LIMIT_BYTES),
        name="in_proj",
    )(x2d, gain, wnn, wnt, cos, sin, cos_t, sin_t)


def _moba_kernel(qt_ref, k_ref, kmean_ref, vt_ref, gat_ref, out_ref, bias_ref, m_ref, acc_ref, *, nb):
    qi = pl.program_id(2)
    blk = MOBA_BLOCK
    qt = qt_ref[0, 0]
    feat = lax.broadcasted_iota(jnp.int32, (GROUP_WIDTH, 1), 0)
    head_of_feat = (feat % 128) // HALF
    q_heads = [jnp.where(head_of_feat == h, qt, jnp.zeros_like(qt)) for h in range(GROUP_HEADS)]
    ones_rows = jnp.ones((ACC_ROWS - HEAD_DIM, blk), BF16)

    def values_ext(ki, h):
        return jnp.concatenate([vt_ref[0, ki, h * HEAD_DIM:(h + 1) * HEAD_DIM, :], ones_rows], axis=0)

    km_hi, km_lo = _split_bf16(kmean_ref[0])
    blk_id = lax.broadcasted_iota(jnp.int32, (nb, blk), 0)
    for h in range(GROUP_HEADS):
        gate = _dot(km_hi, q_heads[h]) + _dot(km_lo, q_heads[h])
        gate = jnp.where(blk_id < qi, gate, -jnp.inf)
        sel = jnp.zeros((nb, blk), jnp.bool_)
        for r in range(min(MOBA_TOPK, nb)):
            top = jnp.max(gate, axis=0, keepdims=True)
            idx = jnp.min(jnp.where(gate == top, blk_id, nb), axis=0, keepdims=True)
            hit = (blk_id == idx) & (r < qi)
            sel = sel | hit
            gate = jnp.where(blk_id == idx, -jnp.inf, gate)
        bias_ref[h] = jnp.where(sel, 0.0, MASKED).astype(F32)

    k_diag = k_ref[0, pl.ds(pl.multiple_of(qi * blk, blk), blk), :]
    k_pos = lax.broadcasted_iota(jnp.int32, (blk, blk), 0)
    q_pos = lax.broadcasted_iota(jnp.int32, (blk, blk), 1)
    causal = k_pos <= q_pos
    scores = [_dot(k_diag, q_heads[h]) for h in range(GROUP_HEADS)]
    for h in range(GROUP_HEADS):
        s = jnp.where(causal, scores[h], MASKED)
        m = jnp.max(s, axis=0, keepdims=True)
        p = jnp.exp2(s - m).astype(BF16)
        m_ref[h] = m
        acc_ref[h] = _dot(values_ext(qi, h), p)

    def past_block(ki, carry):
        kb = k_ref[0, pl.ds(pl.multiple_of(ki * blk, blk), blk), :]
        scores = [_dot(kb, q_heads[h]) for h in range(GROUP_HEADS)]
        for h in range(GROUP_HEADS):
            s = scores[h]
            bias = bias_ref[h, pl.ds(ki, 1), :]
            m_old = m_ref[h]
            m_new = jnp.maximum(m_old, jnp.max(s, axis=0, keepdims=True) + bias)
            alpha = jnp.exp2(m_old - m_new)
            p = jnp.exp2(s - (m_new - bias)).astype(BF16)
            m_ref[h] = m_new
            acc_ref[h] = alpha * acc_ref[h] + _dot(values_ext(ki, h), p)
        return carry

    lax.fori_loop(0, qi, past_block, 0)

    for h in range(GROUP_HEADS):
        rows = slice(h * HEAD_DIM, (h + 1) * HEAD_DIM)
        acc = acc_ref[h]
        y = acc[0:HEAD_DIM] / acc[HEAD_DIM:HEAD_DIM + 1]
        out_ref[0, 0, rows, :] = (y * gat_ref[0, 0, rows, :].astype(F32)).astype(BF16)


def _moba(qt, k, kmean, vt, gat, batch, seq):
    nb = seq // MOBA_BLOCK
    blk = MOBA_BLOCK
    groups = ATTN_WIDTH // GROUP_WIDTH
    k3 = k.reshape(batch, seq, ATTN_WIDTH)
    km3 = kmean.reshape(batch, nb, ATTN_WIDTH)
    tile_t = pl.BlockSpec((1, 1, GROUP_WIDTH, blk), lambda b, g, i: (b, i, g, 0))
    return pl.pallas_call(
        functools.partial(_moba_kernel, nb=nb),
        grid=(batch, groups, nb),
        in_specs=[
            tile_t,
            pl.BlockSpec((1, seq, GROUP_WIDTH), lambda b, g, i: (b, 0, g)),
            pl.BlockSpec((1, nb, GROUP_WIDTH), lambda b, g, i: (b, 0, g)),
            pl.BlockSpec((1, nb, GROUP_WIDTH, blk), lambda b, g, i: (b, 0, g, 0)),
            tile_t,
        ],
        out_specs=tile_t,
        out_shape=jax.ShapeDtypeStruct((batch, nb, ATTN_WIDTH, blk), BF16),
        scratch_shapes=[
            pltpu.VMEM((GROUP_HEADS, nb, blk), F32),
            pltpu.VMEM((GROUP_HEADS, 1, blk), F32),
            pltpu.VMEM((GROUP_HEADS, ACC_ROWS, blk), F32),
        ],
        compiler_params=pltpu.CompilerParams(
            dimension_semantics=("arbitrary", "arbitrary", "arbitrary"), vmem_limit_bytes=VMEM_LIMIT_BYTES),
        name="moba",
    )(qt, k3, km3, vt, gat)


def _block_diag(x_bf16, mask):
    tiled = jnp.concatenate([x_bf16] * GROUP_HEADS, axis=0)
    return jnp.where(mask, tiled, jnp.zeros_like(tiled))


def _rwkv_kernel(pb_ref, gb_ref, mu_ref, wlora_ref, dbias_ref, ibias_ref, kk_ref, ka_ref, rk_ref,
                 gng_ref, gnb_ref, out_ref, state_ref, prev_ref, y_ref):
    tile = pb_ref.shape[0]
    c_w = RWKV_WIDTH

    @pl.when(pl.program_id(1) == 0)
    def _():
        state_ref[...] = jnp.zeros_like(state_ref)
        prev_ref[...] = jnp.zeros_like(prev_ref)

    p = pb_ref[...]
    row = lax.broadcasted_iota(jnp.int32, (tile, 1), 0)
    shifted = jnp.where(row == 0, prev_ref[0:1, :], pltpu.roll(p, 1, axis=0))
    prev_ref[0:1, :] = p[tile - 1:tile, :]
    xs = p + (shifted - p) * mu_ref[...]
    r = xs[:, 0:c_w]
    k = xs[:, c_w:2 * c_w]
    v = xs[:, 2 * c_w:3 * c_w]
    lora_in = xs[:, 3 * c_w:3 * c_w + 2 * LORA_RANK]
    lane128 = lax.broadcasted_iota(jnp.int32, (1, 2 * LORA_RANK), 1)
    lora_in = jnp.where(lane128 < LORA_RANK, jnp.tanh(lora_in), lora_in)
    li_hi, li_lo = _split_bf16(lora_in)
    w_hi = wlora_ref[0]
    w_lo = wlora_ref[1]
    lora = _dot(li_hi, w_hi) + _dot(li_lo, w_hi) + _dot(li_hi, w_lo)
    z = -(dbias_ref[...] + lora[:, 0:c_w])
    softplus = jnp.maximum(z, 0.0) + jnp.log(1.0 + jnp.exp(-jnp.abs(z)))
    g = -jnp.exp(-softplus - 0.5)
    a = 1.0 / (1.0 + jnp.exp(-(ibias_ref[...] + lora[:, c_w:2 * c_w])))

    hr = lax.broadcasted_iota(jnp.int32, (c_w, c_w), 0) // HEAD_DIM
    hc = lax.broadcasted_iota(jnp.int32, (c_w, c_w), 1) // HEAD_DIM
    head_ones = jnp.where(hr == hc, 1.0, 0.0).astype(BF16)

    def head_sum(t):
        hi, lo = _split_bf16(t)
        return _dot(hi, head_ones) + _dot(lo, head_ones)

    kk = k * kk_ref[...]
    kk = kk * lax.rsqrt(jnp.maximum(head_sum(kk * kk), NORMALIZE_EPS_SQ))
    k2 = k * (1.0 + (a - 1.0) * ka_ref[...])
    b = kk * a
    bonus = head_sum(r * k2 * rk_ref[...]) * v

    ti = lax.broadcasted_iota(jnp.int32, (CHUNK, CHUNK), 0)
    tj = lax.broadcasted_iota(jnp.int32, (CHUNK, CHUNK), 1)
    tri_incl = jnp.where(tj <= ti, 1.0, 0.0).astype(BF16)
    g_hi, g_lo = _split_bf16(g)

    gr = lax.broadcasted_iota(jnp.int32, (GROUP_WIDTH, GROUP_WIDTH), 0) // HEAD_DIM
    gc = lax.broadcasted_iota(jnp.int32, (GROUP_WIDTH, GROUP_WIDTH), 1) // HEAD_DIM
    diag_mask = gr == gc
    t_idx = lax.broadcasted_iota(jnp.int32, (CHUNK, GROUP_WIDTH), 0)
    j_idx = lax.broadcasted_iota(jnp.int32, (CHUNK, GROUP_WIDTH), 1) % HEAD_DIM
    strict_lower = j_idx < t_idx
    lower = j_idx <= t_idx
    eye = (j_idx == t_idx).astype(F32)
    lane_head = lax.broadcasted_iota(jnp.int32, (1, GROUP_WIDTH), 1) // HEAD_DIM

    for c in range(tile // CHUNK):
        rs = slice(c * CHUNK, (c + 1) * CHUNK)
        cum = _dot(tri_incl, g_hi[rs]) + _dot(tri_incl, g_lo[rs])
        cum_last = cum[CHUNK - 1:CHUNK, :]
        dec_in = jnp.exp(cum)
        dec_ex = jnp.exp(cum - g[rs])
        inv = jnp.exp(-cum)
        to_end = jnp.exp(cum_last - cum)
        end_decay = jnp.exp(cum_last)
        for gi in range(c_w // GROUP_WIDTH):
            ls = slice(gi * GROUP_WIDTH, (gi + 1) * GROUP_WIDTH)
            a_t = (-kk[rs, ls] * dec_ex[:, ls]).astype(BF16)
            r_t = (r[rs, ls] * dec_in[:, ls]).astype(BF16)
            b_t = (b[rs, ls] * inv[:, ls]).astype(BF16)
            k_t = (k2[rs, ls] * inv[:, ls]).astype(BF16)
            b_e = (b[rs, ls] * to_end[:, ls]).astype(BF16)
            k_e = (k2[rs, ls] * to_end[:, ls]).astype(BF16)
            v_c = v[rs, ls].astype(BF16)

            ar = jnp.concatenate([a_t, r_t], axis=0)
            pb = _dot_nt(ar, _block_diag(b_t, diag_mask))
            pk = _dot_nt(ar, _block_diag(k_t, diag_mask))
            n_ab = jnp.where(strict_lower, pb[0:CHUNK], 0.0)
            a_ak = jnp.where(strict_lower, pk[0:CHUNK], 0.0).astype(BF16)
            a_rb = jnp.where(lower, pb[CHUNK:], 0.0).astype(BF16)
            a_rk = jnp.where(lower, pk[CHUNK:], 0.0).astype(BF16)

            pw = n_ab
            t_inv = eye + n_ab
            steps = CHUNK.bit_length() - 1
            for s in range(1, steps):
                w_p = _block_diag(pw.astype(BF16), diag_mask)
                if s == 1:
                    pw = _dot(pw.astype(BF16), w_p)
                else:
                    both = _dot(jnp.concatenate([pw, t_inv], axis=0).astype(BF16), w_p)
                    t_inv = t_inv + both[CHUNK:]
                    pw = both[0:CHUNK]
            t_inv = t_inv + _dot(t_inv.astype(BF16), _block_diag(pw.astype(BF16), diag_mask))

            st = state_ref[gi]
            w_s = _block_diag(st.astype(BF16), diag_mask)
            w_v = _block_diag(v_c, diag_mask)
            ar_s = _dot_nt(ar, w_s)
            x = ar_s[0:CHUNK] + _dot(a_ak, w_v)
            u = _dot(t_inv.astype(BF16), _block_diag(x.astype(BF16), diag_mask))
            u_bf = u.astype(BF16)
            y = ar_s[CHUNK:] + _dot(a_rb, _block_diag(u_bf, diag_mask)) + _dot(a_rk, w_v)
            y_ref[rs, ls] = y

            full = _dot_tn(jnp.concatenate([u_bf, v_c], axis=0),
                           jnp.concatenate([b_e, k_e], axis=0))
            new = st * end_decay[:, ls]
            for h in range(GROUP_HEADS):
                new = new + jnp.where(lane_head == h, full[h * HEAD_DIM:(h + 1) * HEAD_DIM, :], 0.0)
            state_ref[gi] = new

    y = y_ref[...]
    mean = head_sum(y) * (1.0 / HEAD_DIM)
    d = y - mean
    var = head_sum(d * d) * (1.0 / HEAD_DIM)
    yn = d * lax.rsqrt(var + GN_EPS) * gng_ref[...] + gnb_ref[...]
    out_ref[...] = ((yn + bonus) * gb_ref[...].astype(F32)).astype(BF16)


def _rwkv(pb, gb, mu, wlora, dbias, ibias, k_k, k_a, r_k, gn_gain, gn_bias, batch, seq):
    tile = MOBA_BLOCK
    nt = seq // tile
    row_spec = lambda width: pl.BlockSpec((tile, width), lambda b, i: (b * nt + i, 0))
    vec = lambda width: pl.BlockSpec((1, width), lambda b, i: (0, 0))
    return pl.pallas_call(
        _rwkv_kernel,
        grid=(batch, nt),
        in_specs=[
            row_spec(SHIFT_WIDTH),
            row_spec(RWKV_WIDTH),
            vec(SHIFT_WIDTH),
            pl.BlockSpec((2, 2 * LORA_RANK, 2 * RWKV_WIDTH), lambda b, i: (0, 0, 0)),
        ] + [vec(RWKV_WIDTH)] * 7,
        out_specs=row_spec(RWKV_WIDTH),
        out_shape=jax.ShapeDtypeStruct((batch * seq, RWKV_WIDTH), BF16),
        scratch_shapes=[
            pltpu.VMEM((RWKV_WIDTH // GROUP_WIDTH, HEAD_DIM, GROUP_WIDTH), F32),
            pltpu.VMEM((8, SHIFT_WIDTH), F32),
            pltpu.VMEM((tile, RWKV_WIDTH), F32),
        ],
        compiler_params=pltpu.CompilerParams(
            dimension_semantics=("arbitrary", "arbitrary"), vmem_limit_bytes=VMEM_LIMIT_BYTES),
        name="rwkv",
    )(pb, gb, mu, wlora, dbias, ibias, k_k, k_a, r_k, gn_gain, gn_bias)


def _out_proj_kernel(yat_ref, yb_ref, x_ref, woa_ref, wob_ref, gain_ref, out_ref):
    d = _dot_tn(yat_ref[0, 0], woa_ref[...]) + _dot(yb_ref[...], wob_ref[...])
    h = x_ref[...] + d
    ms = jnp.mean(h * h, axis=-1, keepdims=True)
    out_ref[...] = (h * lax.rsqrt(ms + RMS_EPS) * gain_ref[...]).astype(out_ref.dtype)


def _out_proj(yat, yb, x2d, woa, wob, gain, nb):
    rows, d_model = x2d.shape
    tile = MOBA_BLOCK
    row_spec = lambda width: pl.BlockSpec((tile, width), lambda i: (i, 0))
    const = lambda shape: pl.BlockSpec(shape, lambda i: (0,) * len(shape))
    return pl.pallas_call(
        _out_proj_kernel,
        grid=(rows // tile,),
        in_specs=[
            pl.BlockSpec((1, 1, ATTN_WIDTH, tile), lambda i: (i // nb, i % nb, 0, 0)),
            row_spec(RWKV_WIDTH),
            row_spec(d_model),
            const((ATTN_WIDTH, d_model)),
            const((RWKV_WIDTH, d_model)),
            const((1, d_model)),
        ],
        out_specs=row_spec(d_model),
        out_shape=jax.ShapeDtypeStruct((rows, d_model), x2d.dtype),
        compiler_params=pltpu.CompilerParams(
            dimension_semantics=("arbitrary",), vmem_limit_bytes=VMEM_LIMIT_BYTES),
        name="out_proj",
    )(yat, yb, x2d, woa, wob, gain)


def _rotary_column_order():
    order = []
    for g in range(ATTN_WIDTH // GROUP_WIDTH):
        for part in range(2):
            for hh in range(GROUP_HEADS):
                head = g * GROUP_HEADS + hh
                order.extend(head * HEAD_DIM + part * HALF + j for j in range(HALF))
    return np.asarray(order, np.int32)


def _layer(h2d, batch, seq, norm_gain, w_in, shift_mu, decay_bias, decay_up, iclr_bias, iclr_up,
           k_k, k_a, r_k, gn_gain, gn_bias, w_out, out_gain, cos, sin):
    nb = seq // MOBA_BLOCK
    aw, rw = ATTN_WIDTH, RWKV_WIDTH
    order = _rotary_column_order()
    b0 = 4 * aw
    w_in = w_in.astype(F32)
    wnn = jnp.concatenate([w_in[:, aw:2 * aw][:, order], w_in[:, b0:]], axis=1).astype(BF16)
    wnt = jnp.concatenate([w_in[:, 0:aw][:, order], w_in[:, 2 * aw:4 * aw]], axis=1).T.astype(BF16)
    zeros = jnp.zeros((LORA_RANK, rw), F32)
    wl = jnp.concatenate([jnp.concatenate([decay_up.astype(F32), zeros], axis=1),
                          jnp.concatenate([zeros, iclr_up.astype(F32)], axis=1)], axis=0)
    wl_hi = wl.astype(BF16)
    wl_lo = (wl - wl_hi.astype(F32)).astype(BF16)
    wlora = jnp.stack([wl_hi, wl_lo])
    row = lambda t: t.astype(F32).reshape(1, -1)

    cos_t = cos.reshape(nb, MOBA_BLOCK, 128).transpose(0, 2, 1)
    sin_t = sin.reshape(nb, MOBA_BLOCK, 128).transpose(0, 2, 1)
    qt, k, kmean, vt, gat, pb, gb = _in_proj(h2d, row(norm_gain), wnn, wnt, cos, sin, cos_t, sin_t, batch, nb)
    yat = _moba(qt, k, kmean, vt, gat, batch, seq)
    yb = _rwkv(pb, gb, row(shift_mu), wlora, row(decay_bias), row(iclr_bias), row(k_k), row(k_a),
               row(r_k), row(gn_gain), row(gn_bias), batch, seq)
    w_out = w_out.astype(BF16)
    return _out_proj(yat, yb, h2d, w_out[0:aw], w_out[aw:], out_gain, nb)


def kernel(x, norm_gain, w_in, shift_mu, decay_bias, decay_up, iclr_bias, iclr_up,
           k_k, k_a, r_k, gn_gain, gn_bias, w_out, final_gain):
    batch, seq, d_model = x.shape
    depth = norm_gain.shape[0]
    assert depth == 1, "the final RMSNorm is fused into the single layer's output projection"
    assert seq % MOBA_BLOCK == 0 and w_in.shape[-1] == 4 * ATTN_WIDTH + SHIFT_WIDTH + RWKV_WIDTH
    inv_freq = 1.0 / (ROPE_THETA ** (jnp.arange(0, HEAD_DIM, 2, dtype=F32) / HEAD_DIM))
    ang = jnp.arange(seq, dtype=F32)[:, None] * inv_freq[None, :]
    cos = jnp.tile(jnp.cos(ang), (1, GROUP_HEADS))
    sin = jnp.tile(jnp.sin(ang), (1, GROUP_HEADS))
    out = _layer(x.reshape(batch * seq, d_model).astype(F32), batch, seq, norm_gain[0], w_in[0],
                 shift_mu[0], decay_bias[0], decay_up[0], iclr_bias[0], iclr_up[0], k_k[0], k_a[0], r_k[0],
                 gn_gain[0], gn_bias[0], w_out[0], final_gain.astype(F32).reshape(1, -1), cos, sin)
    return out.reshape(batch, seq, d_model).astype(x.dtype)
```

```python
import functools

import jax
import jax.numpy as jnp
import numpy as np
from jax import lax
from jax.experimental import pallas as pl
from jax.experimental.pallas import tpu as pltpu

HEAD_DIM = 64
HALF = HEAD_DIM // 2
ATTN_HEADS = 8
ATTN_WIDTH = ATTN_HEADS * HEAD_DIM
RWKV_HEADS = 8
RWKV_WIDTH = RWKV_HEADS * HEAD_DIM
LORA_RANK = 64
SHIFT_WIDTH = 3 * RWKV_WIDTH + 2 * LORA_RANK
MOBA_BLOCK = 256
MOBA_TOPK = 3
ROPE_THETA = 10000.0
RMS_EPS = 1e-6
GN_EPS = 64e-5
NORMALIZE_EPS_SQ = 1e-24

GROUP_HEADS = 4
GROUP_WIDTH = GROUP_HEADS * HEAD_DIM
CHUNK = 64
MASKED = -1e30
ACC_ROWS = HEAD_DIM + 16
LOG2E = 1.4426950408889634
Q_SCALE = HEAD_DIM ** -0.5 * LOG2E
VMEM_LIMIT_BYTES = 48 * 1024 * 1024

F32 = jnp.float32
BF16 = jnp.bfloat16


def _dot(a, b):
    return jnp.dot(a, b, preferred_element_type=F32)


def _dot_nt(a, b):
    return lax.dot_general(a, b, (((1,), (1,)), ((), ())), preferred_element_type=F32)


def _dot_tn(a, b):
    return lax.dot_general(a, b, (((0,), (0,)), ((), ())), preferred_element_type=F32)


def _split_bf16(x):
    hi = x.astype(BF16)
    lo = (x - hi.astype(F32)).astype(BF16)
    return hi, lo


def _silu(z):
    return z / (1.0 + jnp.exp(-z))


def _rotary(acc, cos, sin):
    outs = []
    for g in range(ATTN_WIDTH // GROUP_WIDTH):
        x1 = acc[:, g * GROUP_WIDTH:g * GROUP_WIDTH + 128]
        x2 = acc[:, g * GROUP_WIDTH + 128:(g + 1) * GROUP_WIDTH]
        outs.append(x1 * cos - x2 * sin)
        outs.append(x2 * cos + x1 * sin)
    return jnp.concatenate(outs, axis=-1)


def _rotary_t(acc, cos_t, sin_t):
    outs = []
    for g in range(ATTN_WIDTH // GROUP_WIDTH):
        x1 = acc[g * GROUP_WIDTH:g * GROUP_WIDTH + 128, :]
        x2 = acc[g * GROUP_WIDTH + 128:(g + 1) * GROUP_WIDTH, :]
        outs.append(x1 * cos_t - x2 * sin_t)
        outs.append(x2 * cos_t + x1 * sin_t)
    return jnp.concatenate(outs, axis=0)


def _in_proj_kernel(x_ref, gain_ref, wnn_ref, wnt_ref, cos_ref, sin_ref, cost_ref, sint_ref,
                    qt_ref, k_ref, kmean_ref, vt_ref, gat_ref, pb_ref, gb_ref):
    x = x_ref[...]
    ms = jnp.mean(x * x, axis=-1, keepdims=True)
    u = (x * lax.rsqrt(ms + RMS_EPS) * gain_ref[...]).astype(BF16)

    k = _rotary(_dot(u, wnn_ref[:, 0:ATTN_WIDTH]), cos_ref[...], sin_ref[...])
    k_ref[...] = k.astype(BF16)
    kmean_ref[0] = jnp.mean(k, axis=0, keepdims=True)

    base = ATTN_WIDTH
    for c in range(0, SHIFT_WIDTH, 512):
        w = min(512, SHIFT_WIDTH - c)
        pb_ref[:, c:c + w] = _dot(u, wnn_ref[:, base + c:base + c + w])
    gb = _dot(u, wnn_ref[:, base + SHIFT_WIDTH:base + SHIFT_WIDTH + RWKV_WIDTH])
    gb_ref[...] = _silu(gb).astype(BF16)

    qt = _rotary_t(_dot_nt(wnt_ref[0:ATTN_WIDTH, :], u), cost_ref[0], sint_ref[0])
    qt_ref[0, 0] = (qt * Q_SCALE).astype(BF16)
    vt_ref[0, 0] = _dot_nt(wnt_ref[ATTN_WIDTH:2 * ATTN_WIDTH, :], u).astype(BF16)
    gat = _dot_nt(wnt_ref[2 * ATTN_WIDTH:3 * ATTN_WIDTH, :], u)
    gat_ref[0, 0] = _silu(gat).astype(BF16)


def _in_proj(x2d, gain, wnn, wnt, cos, sin, cos_t, sin_t, batch, nb):
    rows, d_model = x2d.shape
    tile = MOBA_BLOCK
    row_spec = lambda width: pl.BlockSpec((tile, width), lambda i: (i, 0))
    const = lambda shape: pl.BlockSpec(shape, lambda i: (0,) * len(shape))
    blk_t = pl.BlockSpec((1, 1, ATTN_WIDTH, tile), lambda i: (i // nb, i % nb, 0, 0))
    blk_t_shape = jax.ShapeDtypeStruct((batch, nb, ATTN_WIDTH, tile), BF16)
    return pl.pallas_call(
        _in_proj_kernel,
        grid=(rows // tile,),
        in_specs=[
            row_spec(d_model),
            const((1, d_model)),
            const(wnn.shape),
            const(wnt.shape),
            pl.BlockSpec((tile, 128), lambda i: (i % nb, 0)),
            pl.BlockSpec((tile, 128), lambda i: (i % nb, 0)),
            pl.BlockSpec((1, 128, tile), lambda i: (i % nb, 0, 0)),
            pl.BlockSpec((1, 128, tile), lambda i: (i % nb, 0, 0)),
        ],
        out_specs=[
            blk_t,
            row_spec(ATTN_WIDTH),
            pl.BlockSpec((1, 1, ATTN_WIDTH), lambda i: (i, 0, 0)),
            blk_t,
            blk_t,
            row_spec(SHIFT_WIDTH),
            row_spec(RWKV_WIDTH),
        ],
        out_shape=[
            blk_t_shape,
            jax.ShapeDtypeStruct((rows, ATTN_WIDTH), BF16),
            jax.ShapeDtypeStruct((rows // tile, 1, ATTN_WIDTH), F32),
            blk_t_shape,
            blk_t_shape,
            jax.ShapeDtypeStruct((rows, SHIFT_WIDTH), F32),
            jax.ShapeDtypeStruct((rows, RWKV_WIDTH), BF16),
        ],
        compiler_params=pltpu.CompilerParams(
            dimension_semantics=("arbitrary",), vmem_limit_bytes=VMEM_LIMIT_BYTES),
        name="in_proj",
    )(x2d, gain, wnn, wnt, cos, sin, cos_t, sin_t)


def _moba_kernel(qta_ref, qtb_ref, k_ref, kmean_ref, vt_ref, gata_ref, gatb_ref, outa_ref, outb_ref,
                 qh_ref, bias_ref, m_ref, acc_ref, *, nb):
    j = pl.program_id(2)
    blk = MOBA_BLOCK
    tiles = ((qta_ref, gata_ref, outa_ref, j), (qtb_ref, gatb_ref, outb_ref, nb - 1 - j))
    feat = lax.broadcasted_iota(jnp.int32, (GROUP_WIDTH, 1), 0)
    head_of_feat = (feat % 128) // HALF
    ones_rows = jnp.ones((ACC_ROWS - HEAD_DIM, blk), BF16)
    km_hi, km_lo = _split_bf16(kmean_ref[0])
    blk_id = lax.broadcasted_iota(jnp.int32, (nb, blk), 0)
    k_pos = lax.broadcasted_iota(jnp.int32, (blk, blk), 0)
    q_pos = lax.broadcasted_iota(jnp.int32, (blk, blk), 1)
    causal = k_pos <= q_pos

    def values_ext(ki, h):
        return jnp.concatenate([vt_ref[0, ki, h * HEAD_DIM:(h + 1) * HEAD_DIM, :], ones_rows], axis=0)

    for t, (qt_ref, _, _, qi) in enumerate(tiles):
        qt = qt_ref[0, 0]
        for h in range(GROUP_HEADS):
            qh = jnp.where(head_of_feat == h, qt, jnp.zeros_like(qt))
            qh_ref[t, h] = qh
            gate = _dot(km_hi, qh) + _dot(km_lo, qh)
            gate = jnp.where(blk_id < qi, gate, -jnp.inf)
            sel = jnp.zeros((nb, blk), jnp.bool_)
            for r in range(min(MOBA_TOPK, nb)):
                top = jnp.max(gate, axis=0, keepdims=True)
                idx = jnp.min(jnp.where(gate == top, blk_id, nb), axis=0, keepdims=True)
                hit = (blk_id == idx) & (r < qi)
                sel = sel | hit
                gate = jnp.where(blk_id == idx, -jnp.inf, gate)
            bias_ref[t, h] = jnp.where(sel, 0.0, MASKED).astype(F32)

    def issue_scores(t, ki):
        kb = k_ref[0, pl.ds(pl.multiple_of(ki * blk, blk), blk), :]
        return [_dot(kb, qh_ref[t, h]) for h in range(GROUP_HEADS)]

    def finish_own(t, ki, scores):
        for h in range(GROUP_HEADS):
            s = jnp.where(causal, scores[h], MASKED)
            m = jnp.max(s, axis=0, keepdims=True)
            p = jnp.exp2(s - m).astype(BF16)
            m_ref[t, h] = m
            acc_ref[t, h] = _dot(values_ext(ki, h), p)

    def finish_past(t, ki, scores):
        for h in range(GROUP_HEADS):
            s = scores[h]
            bias = bias_ref[t, h, pl.ds(ki, 1), :]
            m_old = m_ref[t, h]
            m_new = jnp.maximum(m_old, jnp.max(s, axis=0, keepdims=True) + bias)
            alpha = jnp.exp2(m_old - m_new)
            p = jnp.exp2(s - (m_new - bias)).astype(BF16)
            m_ref[t, h] = m_new
            acc_ref[t, h] = alpha * acc_ref[t, h] + _dot(values_ext(ki, h), p)

    work = [(finish_own, 0, j), (finish_own, 1, nb - 1 - j)]
    for n in range(nb - 1):
        second = (n >= j).astype(jnp.int32)
        work.append((finish_past, second, n - second * j))
    pending = None
    for fin, t, ki in work:
        scores = issue_scores(t, ki)
        if pending is not None:
            pending[0](*pending[1:])
        pending = (fin, t, ki, scores)
    pending[0](*pending[1:])

    for t, (_, gat_ref, out_ref, _) in enumerate(tiles):
        for h in range(GROUP_HEADS):
            rows = slice(h * HEAD_DIM, (h + 1) * HEAD_DIM)
            acc = acc_ref[t, h]
            y = acc[0:HEAD_DIM] / acc[HEAD_DIM:HEAD_DIM + 1]
            out_ref[0, 0, rows, :] = (y * gat_ref[0, 0, rows, :].astype(F32)).astype(BF16)


def _moba(qt, k, kmean, vt, gat, batch, seq):
    nb = seq // MOBA_BLOCK
    assert nb % 2 == 0
    blk = MOBA_BLOCK
    groups = ATTN_WIDTH // GROUP_WIDTH
    k3 = k.reshape(batch, seq, ATTN_WIDTH)
    km3 = kmean.reshape(batch, nb, ATTN_WIDTH)
    tile_a = pl.BlockSpec((1, 1, GROUP_WIDTH, blk), lambda b, g, j: (b, j, g, 0))
    tile_b = pl.BlockSpec((1, 1, GROUP_WIDTH, blk), lambda b, g, j: (b, nb - 1 - j, g, 0))
    out_shape = jax.ShapeDtypeStruct((batch, nb // 2, ATTN_WIDTH, blk), BF16)
    return pl.pallas_call(
        functools.partial(_moba_kernel, nb=nb),
        grid=(batch, groups, nb // 2),
        in_specs=[
            tile_a,
            tile_b,
            pl.BlockSpec((1, seq, GROUP_WIDTH), lambda b, g, j: (b, 0, g)),
            pl.BlockSpec((1, nb, GROUP_WIDTH), lambda b, g, j: (b, 0, g)),
            pl.BlockSpec((1, nb, GROUP_WIDTH, blk), lambda b, g, j: (b, 0, g, 0)),
            tile_a,
            tile_b,
        ],
        out_specs=[tile_a, tile_a],
        out_shape=[out_shape, out_shape],
        scratch_shapes=[
            pltpu.VMEM((2, GROUP_HEADS, GROUP_WIDTH, blk), BF16),
            pltpu.VMEM((2, GROUP_HEADS, nb, blk), F32),
            pltpu.VMEM((2, GROUP_HEADS, 1, blk), F32),
            pltpu.VMEM((2, GROUP_HEADS, ACC_ROWS, blk), F32),
        ],
        compiler_params=pltpu.CompilerParams(
            dimension_semantics=("arbitrary", "arbitrary", "arbitrary"), vmem_limit_bytes=VMEM_LIMIT_BYTES),
        name="moba",
    )(qt, qt, k3, km3, vt, gat, gat)


def _block_diag(x_bf16, mask):
    tiled = jnp.concatenate([x_bf16] * GROUP_HEADS, axis=0)
    return jnp.where(mask, tiled, jnp.zeros_like(tiled))


def _rwkv_kernel(pb_ref, gb_ref, mu_ref, wlora_ref, dbias_ref, ibias_ref, kk_ref, ka_ref, rk_ref,
                 gng_ref, gnb_ref, out_ref, state_ref, prev_ref, y_ref):
    tile = pb_ref.shape[0]
    c_w = RWKV_WIDTH

    @pl.when(pl.program_id(1) == 0)
    def _():
        state_ref[...] = jnp.zeros_like(state_ref)
        prev_ref[...] = jnp.zeros_like(prev_ref)

    p = pb_ref[...]
    row = lax.broadcasted_iota(jnp.int32, (tile, 1), 0)
    shifted = jnp.where(row == 0, prev_ref[0:1, :], pltpu.roll(p, 1, axis=0))
    prev_ref[0:1, :] = p[tile - 1:tile, :]
    xs = p + (shifted - p) * mu_ref[...]
    r = xs[:, 0:c_w]
    k = xs[:, c_w:2 * c_w]
    v = xs[:, 2 * c_w:3 * c_w]
    lora_in = xs[:, 3 * c_w:3 * c_w + 2 * LORA_RANK]
    lane128 = lax.broadcasted_iota(jnp.int32, (1, 2 * LORA_RANK), 1)
    lora_in = jnp.where(lane128 < LORA_RANK, jnp.tanh(lora_in), lora_in)
    li_hi, li_lo = _split_bf16(lora_in)
    w_hi = wlora_ref[0]
    w_lo = wlora_ref[1]
    lora = _dot(li_hi, w_hi) + _dot(li_lo, w_hi) + _dot(li_hi, w_lo)
    z = -(dbias_ref[...] + lora[:, 0:c_w])
    softplus = jnp.maximum(z, 0.0) + jnp.log(1.0 + jnp.exp(-jnp.abs(z)))
    g = -jnp.exp(-softplus - 0.5)
    a = 1.0 / (1.0 + jnp.exp(-(ibias_ref[...] + lora[:, c_w:2 * c_w])))

    gr = lax.broadcasted_iota(jnp.int32, (GROUP_WIDTH, GROUP_WIDTH), 0) // HEAD_DIM
    gc = lax.broadcasted_iota(jnp.int32, (GROUP_WIDTH, GROUP_WIDTH), 1) // HEAD_DIM
    diag_mask = gr == gc

    head_ones = jnp.where(diag_mask, 1.0, 0.0).astype(BF16)

    def head_sum(t):
        t = t.astype(BF16)
        return jnp.concatenate([_dot(t[:, i:i + GROUP_WIDTH], head_ones)
                                for i in range(0, c_w, GROUP_WIDTH)], axis=-1)

    kk = k * kk_ref[...]
    kk = kk * lax.rsqrt(jnp.maximum(head_sum(kk * kk), NORMALIZE_EPS_SQ))
    k2 = k * (1.0 + (a - 1.0) * ka_ref[...])
    b = kk * a
    bonus = head_sum(r * k2 * rk_ref[...]) * v

    ti = lax.broadcasted_iota(jnp.int32, (CHUNK, CHUNK), 0)
    tj = lax.broadcasted_iota(jnp.int32, (CHUNK, CHUNK), 1)
    tri_incl = jnp.where(tj <= ti, 1.0, 0.0).astype(BF16)
    g_hi, g_lo = _split_bf16(g)

    t_idx = lax.broadcasted_iota(jnp.int32, (CHUNK, GROUP_WIDTH), 0)
    j_idx = lax.broadcasted_iota(jnp.int32, (CHUNK, GROUP_WIDTH), 1) % HEAD_DIM
    strict_lower = j_idx < t_idx
    lower = j_idx <= t_idx
    eye = (j_idx == t_idx).astype(F32)
    lane_head = lax.broadcasted_iota(jnp.int32, (1, GROUP_WIDTH), 1) // HEAD_DIM

    def bd(x):
        return _block_diag(x.astype(BF16), diag_mask)

    def diag_blocks(full):
        out = full[(GROUP_HEADS - 1) * HEAD_DIM:, :]
        for h in range(GROUP_HEADS - 2, -1, -1):
            out = jnp.where(lane_head == h, full[h * HEAD_DIM:(h + 1) * HEAD_DIM, :], out)
        return out

    n_groups = c_w // GROUP_WIDTH
    items = []
    for c in range(tile // CHUNK):
        rs = slice(c * CHUNK, (c + 1) * CHUNK)
        cum = _dot(tri_incl, g_hi[rs]) + _dot(tri_incl, g_lo[rs])
        cum_last = cum[CHUNK - 1:CHUNK, :]
        dec_in = jnp.exp(cum)
        dec_ex = jnp.exp(cum - g[rs])
        inv = jnp.exp(-cum)
        to_end = jnp.exp(cum_last - cum)
        end_decay = jnp.exp(cum_last)
        for gi in range(n_groups):
            ls = slice(gi * GROUP_WIDTH, (gi + 1) * GROUP_WIDTH)
            it = dict(rs=rs, ls=ls, gi=gi)
            it["a_t"] = (-kk[rs, ls] * dec_ex[:, ls]).astype(BF16)
            it["r_f"] = r[rs, ls] * dec_in[:, ls]
            b_t = (b[rs, ls] * inv[:, ls]).astype(BF16)
            k_t = (k2[rs, ls] * inv[:, ls]).astype(BF16)
            it["bk_e"] = jnp.concatenate([(b[rs, ls] * to_end[:, ls]).astype(BF16),
                                          (k2[rs, ls] * to_end[:, ls]).astype(BF16)], axis=0)
            it["v_c"] = v[rs, ls].astype(BF16)
            it["m_diag"] = eye * end_decay[:, ls]
            ar = jnp.concatenate([it["a_t"], it["r_f"].astype(BF16)], axis=0)
            pb = _dot_nt(ar, _block_diag(b_t, diag_mask))
            pk = _dot_nt(ar, _block_diag(k_t, diag_mask))
            it["pw"] = jnp.where(strict_lower, pb[0:CHUNK], 0.0)
            it["t_inv"] = eye + it["pw"]
            it["a_rb"] = jnp.where(lower, pb[CHUNK:], 0.0).astype(BF16)
            it["a_k"] = jnp.concatenate([jnp.where(strict_lower, pk[0:CHUNK], 0.0),
                                         jnp.where(lower, pk[CHUNK:], 0.0)], axis=0).astype(BF16)
            items.append(it)

    steps = CHUNK.bit_length() - 1
    for s in range(1, steps + 1):
        for it in items:
            w_p = bd(it["pw"])
            if s == 1:
                it["pw"] = _dot(it["pw"].astype(BF16), w_p)
            elif s < steps:
                both = _dot(jnp.concatenate([it["pw"], it["t_inv"]], axis=0).astype(BF16), w_p)
                it["t_inv"] = it["t_inv"] + both[CHUNK:]
                it["pw"] = both[0:CHUNK]
            else:
                it["t_inv"] = (it["t_inv"] + _dot(it["t_inv"].astype(BF16), w_p)).astype(BF16)

    for it in items:
        kv = _dot(it["a_k"], bd(it["v_c"]))
        it["akv"] = kv[0:CHUNK]
        it["arkv"] = kv[CHUNK:]
    for it in items:
        it["w_t"] = _dot(it["t_inv"], bd(it["a_t"])).astype(BF16)
        it["u_t"] = _dot(it["t_inv"], bd(it["akv"])).astype(BF16)
    for it in items:
        it["r_hat"] = (it["r_f"] + _dot(it["a_rb"], bd(it["w_t"]))).astype(BF16)
        it["y_hat"] = _dot(it["a_rb"], bd(it["u_t"])) + it["arkv"]
    for it in items:
        m_full = _dot_tn(it["bk_e"][0:CHUNK], it["w_t"])
        it["m"] = (it["m_diag"] + diag_blocks(m_full)).astype(BF16)
        n_full = _dot_tn(it["bk_e"], jnp.concatenate([it["u_t"], it["v_c"]], axis=0))
        it["n"] = diag_blocks(n_full)

    states = [state_ref[gi] for gi in range(n_groups)]
    for it in items:
        gi = it["gi"]
        res = _dot(jnp.concatenate([it["m"], it["r_hat"]], axis=0), bd(states[gi]))
        y_ref[it["rs"], it["ls"]] = res[CHUNK:] + it["y_hat"]
        states[gi] = res[0:CHUNK] + it["n"]
    for gi in range(n_groups):
        state_ref[gi] = states[gi]

    y = y_ref[...]
    mean = head_sum(y) * (1.0 / HEAD_DIM)
    d = y - mean
    var = head_sum(d * d) * (1.0 / HEAD_DIM)
    yn = d * lax.rsqrt(var + GN_EPS) * gng_ref[...] + gnb_ref[...]
    out_ref[...] = ((yn + bonus) * gb_ref[...].astype(F32)).astype(BF16)


def _rwkv(pb, gb, mu, wlora, dbias, ibias, k_k, k_a, r_k, gn_gain, gn_bias, batch, seq):
    tile = MOBA_BLOCK
    nt = seq // tile
    row_spec = lambda width: pl.BlockSpec((tile, width), lambda b, i: (b * nt + i, 0))
    vec = lambda width: pl.BlockSpec((1, width), lambda b, i: (0, 0))
    return pl.pallas_call(
        _rwkv_kernel,
        grid=(batch, nt),
        in_specs=[
            row_spec(SHIFT_WIDTH),
            row_spec(RWKV_WIDTH),
            vec(SHIFT_WIDTH),
            pl.BlockSpec((2, 2 * LORA_RANK, 2 * RWKV_WIDTH), lambda b, i: (0, 0, 0)),
        ] + [vec(RWKV_WIDTH)] * 7,
        out_specs=row_spec(RWKV_WIDTH),
        out_shape=jax.ShapeDtypeStruct((batch * seq, RWKV_WIDTH), BF16),
        scratch_shapes=[
            pltpu.VMEM((RWKV_WIDTH // GROUP_WIDTH, HEAD_DIM, GROUP_WIDTH), F32),
            pltpu.VMEM((8, SHIFT_WIDTH), F32),
            pltpu.VMEM((tile, RWKV_WIDTH), F32),
        ],
        compiler_params=pltpu.CompilerParams(
            dimension_semantics=("arbitrary", "arbitrary"), vmem_limit_bytes=VMEM_LIMIT_BYTES),
        name="rwkv",
    )(pb, gb, mu, wlora, dbias, ibias, k_k, k_a, r_k, gn_gain, gn_bias)


def _out_proj_kernel(yat1_ref, yat2_ref, yb_ref, x_ref, woa_ref, wob_ref, gain_ref, out_ref, *, nb):
    in_first = (pl.program_id(0) % nb) < nb // 2
    yat = jnp.where(in_first, yat1_ref[0, 0], yat2_ref[0, 0])
    d = _dot_tn(yat, woa_ref[...]) + _dot(yb_ref[...], wob_ref[...])
    h = x_ref[...] + d
    ms = jnp.mean(h * h, axis=-1, keepdims=True)
    out_ref[...] = (h * lax.rsqrt(ms + RMS_EPS) * gain_ref[...]).astype(out_ref.dtype)


def _out_proj(yat_first, yat_second, yb, x2d, woa, wob, gain, nb):
    rows, d_model = x2d.shape
    tile = MOBA_BLOCK
    half = nb // 2
    row_spec = lambda width: pl.BlockSpec((tile, width), lambda i: (i, 0))
    const = lambda shape: pl.BlockSpec(shape, lambda i: (0,) * len(shape))
    return pl.pallas_call(
        functools.partial(_out_proj_kernel, nb=nb),
        grid=(rows // tile,),
        in_specs=[
            pl.BlockSpec((1, 1, ATTN_WIDTH, tile), lambda i: (i // nb, jnp.minimum(i % nb, half - 1), 0, 0)),
            pl.BlockSpec((1, 1, ATTN_WIDTH, tile), lambda i: (i // nb, jnp.minimum(nb - 1 - i % nb, half - 1), 0, 0)),
            row_spec(RWKV_WIDTH),
            row_spec(d_model),
            const((ATTN_WIDTH, d_model)),
            const((RWKV_WIDTH, d_model)),
            const((1, d_model)),
        ],
        out_specs=row_spec(d_model),
        out_shape=jax.ShapeDtypeStruct((rows, d_model), x2d.dtype),
        compiler_params=pltpu.CompilerParams(
            dimension_semantics=("arbitrary",), vmem_limit_bytes=VMEM_LIMIT_BYTES),
        name="out_proj",
    )(yat_first, yat_second, yb, x2d, woa, wob, gain)


def _rotary_column_order():
    order = []
    for g in range(ATTN_WIDTH // GROUP_WIDTH):
        for part in range(2):
            for hh in range(GROUP_HEADS):
                head = g * GROUP_HEADS + hh
                order.extend(head * HEAD_DIM + part * HALF + j for j in range(HALF))
    return np.asarray(order, np.int32)


def _layer(h2d, batch, seq, norm_gain, w_in, shift_mu, decay_bias, decay_up, iclr_bias, iclr_up,
           k_k, k_a, r_k, gn_gain, gn_bias, w_out, out_gain, cos, sin):
    nb = seq // MOBA_BLOCK
    aw, rw = ATTN_WIDTH, RWKV_WIDTH
    order = _rotary_column_order()
    b0 = 4 * aw
    w_in = w_in.astype(F32)
    wnn = jnp.concatenate([w_in[:, aw:2 * aw][:, order], w_in[:, b0:]], axis=1).astype(BF16)
    wnt = jnp.concatenate([w_in[:, 0:aw][:, order], w_in[:, 2 * aw:4 * aw]], axis=1).T.astype(BF16)
    zeros = jnp.zeros((LORA_RANK, rw), F32)
    wl = jnp.concatenate([jnp.concatenate([decay_up.astype(F32), zeros], axis=1),
                          jnp.concatenate([zeros, iclr_up.astype(F32)], axis=1)], axis=0)
    wl_hi = wl.astype(BF16)
    wl_lo = (wl - wl_hi.astype(F32)).astype(BF16)
    wlora = jnp.stack([wl_hi, wl_lo])
    row = lambda t: t.astype(F32).reshape(1, -1)

    cos_t = cos.reshape(nb, MOBA_BLOCK, 128).transpose(0, 2, 1)
    sin_t = sin.reshape(nb, MOBA_BLOCK, 128).transpose(0, 2, 1)
    qt, k, kmean, vt, gat, pb, gb = _in_proj(h2d, row(norm_gain), wnn, wnt, cos, sin, cos_t, sin_t, batch, nb)
    yat_first, yat_second = _moba(qt, k, kmean, vt, gat, batch, seq)
    yb = _rwkv(pb, gb, row(shift_mu), wlora, row(decay_bias), row(iclr_bias), row(k_k), row(k_a),
               row(r_k), row(gn_gain), row(gn_bias), batch, seq)
    w_out = w_out.astype(BF16)
    return _out_proj(yat_first, yat_second, yb, h2d, w_out[0:aw], w_out[aw:], out_gain, nb)


def kernel(x, norm_gain, w_in, shift_mu, decay_bias, decay_up, iclr_bias, iclr_up,
           k_k, k_a, r_k, gn_gain, gn_bias, w_out, final_gain):
    batch, seq, d_model = x.shape
    depth = norm_gain.shape[0]
    assert depth == 1, "the final RMSNorm is fused into the single layer's output projection"
    assert seq % MOBA_BLOCK == 0 and w_in.shape[-1] == 4 * ATTN_WIDTH + SHIFT_WIDTH + RWKV_WIDTH
    inv_freq = 1.0 / (ROPE_THETA ** (jnp.arange(0, HEAD_DIM, 2, dtype=F32) / HEAD_DIM))
    ang = jnp.arange(seq, dtype=F32)[:, None] * inv_freq[None, :]
    cos = jnp.tile(jnp.cos(ang), (1, GROUP_HEADS))
    sin = jnp.tile(jnp.sin(ang), (1, GROUP_HEADS))
    out = _layer(x.reshape(batch * seq, d_model).astype(F32), batch, seq, norm_gain[0], w_in[0],
                 shift_mu[0], decay_bias[0], decay_up[0], iclr_bias[0], iclr_up[0], k_k[0], k_a[0], r_k[0],
                 gn_gain[0], gn_bias[0], w_out[0], final_gain.astype(F32).reshape(1, -1), cos, sin)
    return out.reshape(batch, seq, d_model).astype(x.dtype)
```

```python
import functools

import jax
import jax.numpy as jnp
import numpy as np
from jax import lax
from jax.experimental import pallas as pl
from jax.experimental.pallas import tpu as pltpu

HEAD_DIM = 64
HALF = HEAD_DIM // 2
ATTN_HEADS = 8
ATTN_WIDTH = ATTN_HEADS * HEAD_DIM
RWKV_HEADS = 8
RWKV_WIDTH = RWKV_HEADS * HEAD_DIM
LORA_RANK = 64
SHIFT_WIDTH = 3 * RWKV_WIDTH + 2 * LORA_RANK
MOBA_BLOCK = 256
MOBA_TOPK = 3
ROPE_THETA = 10000.0
RMS_EPS = 1e-6
GN_EPS = 64e-5
NORMALIZE_EPS_SQ = 1e-24

GROUP_HEADS = 4
GROUP_WIDTH = GROUP_HEADS * HEAD_DIM
CHUNK = 64
MASKED = -1e30
DECAY_SCALE = -float(np.exp(-0.5))
ACC_ROWS = HEAD_DIM + 16
LOG2E = 1.4426950408889634
Q_SCALE = HEAD_DIM ** -0.5 * LOG2E
VMEM_LIMIT_BYTES = 48 * 1024 * 1024

F32 = jnp.float32
BF16 = jnp.bfloat16


def _dot(a, b):
    return jnp.dot(a, b, preferred_element_type=F32)


def _dot_nt(a, b):
    return lax.dot_general(a, b, (((1,), (1,)), ((), ())), preferred_element_type=F32)


def _dot_tn(a, b):
    return lax.dot_general(a, b, (((0,), (0,)), ((), ())), preferred_element_type=F32)


def _split_bf16(x):
    hi = x.astype(BF16)
    lo = (x - hi.astype(F32)).astype(BF16)
    return hi, lo


def _silu(z):
    return z / (1.0 + jnp.exp(-z))


def _rotary(acc, cos, sin):
    outs = []
    for g in range(ATTN_WIDTH // GROUP_WIDTH):
        x1 = acc[:, g * GROUP_WIDTH:g * GROUP_WIDTH + 128]
        x2 = acc[:, g * GROUP_WIDTH + 128:(g + 1) * GROUP_WIDTH]
        outs.append(x1 * cos - x2 * sin)
        outs.append(x2 * cos + x1 * sin)
    return jnp.concatenate(outs, axis=-1)


def _rotary_t(acc, cos_t, sin_t):
    outs = []
    for g in range(ATTN_WIDTH // GROUP_WIDTH):
        x1 = acc[g * GROUP_WIDTH:g * GROUP_WIDTH + 128, :]
        x2 = acc[g * GROUP_WIDTH + 128:(g + 1) * GROUP_WIDTH, :]
        outs.append(x1 * cos_t - x2 * sin_t)
        outs.append(x2 * cos_t + x1 * sin_t)
    return jnp.concatenate(outs, axis=0)


def _in_proj_kernel(x_ref, gain_ref, wnn_ref, wnt_ref, cos_ref, sin_ref, cost_ref, sint_ref, mu_ref,
                    qt_ref, k_ref, kmean_ref, vt_ref, gat_ref, rkv_ref, lora_ref, gb_ref, prev_ref, *, nb):
    tile = x_ref.shape[0]
    x = x_ref[...]
    ms = jnp.mean(x * x, axis=-1, keepdims=True)
    u = (x * lax.rsqrt(ms + RMS_EPS) * gain_ref[...]).astype(BF16)

    k = _rotary(_dot(u, wnn_ref[:, 0:ATTN_WIDTH]), cos_ref[...], sin_ref[...])
    k_ref[...] = k.astype(BF16)
    kmean_ref[0] = jnp.mean(k, axis=0, keepdims=True)

    @pl.when(pl.program_id(0) % nb == 0)
    def _():
        prev_ref[...] = jnp.zeros_like(prev_ref)

    base = ATTN_WIDTH
    row = lax.broadcasted_iota(jnp.int32, (tile, 1), 0)
    for c in range(0, SHIFT_WIDTH, 512):
        w = min(512, SHIFT_WIDTH - c)
        p = _dot(u, wnn_ref[:, base + c:base + c + w])
        shifted = jnp.where(row == 0, prev_ref[0:1, c:c + w], pltpu.roll(p, 1, axis=0))
        prev_ref[0:1, c:c + w] = p[tile - 1:tile, :]
        xs = p + (shifted - p) * mu_ref[:, c:c + w]
        if c + w <= 3 * RWKV_WIDTH:
            rkv_ref[:, c:c + w] = xs.astype(BF16)
        else:
            lora_ref[...] = xs
    gb = _dot(u, wnn_ref[:, base + SHIFT_WIDTH:base + SHIFT_WIDTH + RWKV_WIDTH])
    gb_ref[...] = _silu(gb).astype(BF16)

    qt = _rotary_t(_dot_nt(wnt_ref[0:ATTN_WIDTH, :], u), cost_ref[0], sint_ref[0])
    qt_ref[0, 0] = (qt * Q_SCALE).astype(BF16)
    vt_ref[0, 0] = _dot_nt(wnt_ref[ATTN_WIDTH:2 * ATTN_WIDTH, :], u).astype(BF16)
    gat = _dot_nt(wnt_ref[2 * ATTN_WIDTH:3 * ATTN_WIDTH, :], u)
    gat_ref[0, 0] = _silu(gat).astype(BF16)


def _in_proj(x2d, gain, wnn, wnt, cos, sin, cos_t, sin_t, mu, batch, nb):
    rows, d_model = x2d.shape
    tile = MOBA_BLOCK
    row_spec = lambda width: pl.BlockSpec((tile, width), lambda i: (i, 0))
    const = lambda shape: pl.BlockSpec(shape, lambda i: (0,) * len(shape))
    blk_t = pl.BlockSpec((1, 1, ATTN_WIDTH, tile), lambda i: (i // nb, i % nb, 0, 0))
    blk_t_shape = jax.ShapeDtypeStruct((batch, nb, ATTN_WIDTH, tile), BF16)
    return pl.pallas_call(
        functools.partial(_in_proj_kernel, nb=nb),
        grid=(rows // tile,),
        in_specs=[
            row_spec(d_model),
            const((1, d_model)),
            const(wnn.shape),
            const(wnt.shape),
            pl.BlockSpec((tile, 128), lambda i: (i % nb, 0)),
            pl.BlockSpec((tile, 128), lambda i: (i % nb, 0)),
            pl.BlockSpec((1, 128, tile), lambda i: (i % nb, 0, 0)),
            pl.BlockSpec((1, 128, tile), lambda i: (i % nb, 0, 0)),
            const((1, SHIFT_WIDTH)),
        ],
        out_specs=[
            blk_t,
            row_spec(ATTN_WIDTH),
            pl.BlockSpec((1, 1, ATTN_WIDTH), lambda i: (i, 0, 0)),
            blk_t,
            blk_t,
            row_spec(3 * RWKV_WIDTH),
            row_spec(2 * LORA_RANK),
            row_spec(RWKV_WIDTH),
        ],
        out_shape=[
            blk_t_shape,
            jax.ShapeDtypeStruct((rows, ATTN_WIDTH), BF16),
            jax.ShapeDtypeStruct((rows // tile, 1, ATTN_WIDTH), F32),
            blk_t_shape,
            blk_t_shape,
            jax.ShapeDtypeStruct((rows, 3 * RWKV_WIDTH), BF16),
            jax.ShapeDtypeStruct((rows, 2 * LORA_RANK), F32),
            jax.ShapeDtypeStruct((rows, RWKV_WIDTH), BF16),
        ],
        scratch_shapes=[pltpu.VMEM((8, SHIFT_WIDTH), F32)],
        compiler_params=pltpu.CompilerParams(
            dimension_semantics=("arbitrary",), vmem_limit_bytes=VMEM_LIMIT_BYTES),
        name="in_proj",
    )(x2d, gain, wnn, wnt, cos, sin, cos_t, sin_t, mu)


def _moba_kernel(qta_ref, qtb_ref, k_ref, kmean_ref, vt_ref, gata_ref, gatb_ref, outa_ref, outb_ref,
                 qh_ref, bias_ref, m_ref, acc_ref, *, nb):
    j = pl.program_id(2)
    blk = MOBA_BLOCK
    tiles = ((qta_ref, gata_ref, outa_ref, j), (qtb_ref, gatb_ref, outb_ref, nb - 1 - j))
    feat = lax.broadcasted_iota(jnp.int32, (GROUP_WIDTH, 1), 0)
    head_of_feat = (feat % 128) // HALF
    ones_rows = jnp.ones((ACC_ROWS - HEAD_DIM, blk), BF16)
    km_hi, km_lo = _split_bf16(kmean_ref[0])
    blk_id = lax.broadcasted_iota(jnp.int32, (nb, blk), 0)
    k_pos = lax.broadcasted_iota(jnp.int32, (blk, blk), 0)
    q_pos = lax.broadcasted_iota(jnp.int32, (blk, blk), 1)
    causal = k_pos <= q_pos

    def values_ext(ki, h):
        return jnp.concatenate([vt_ref[0, ki, h * HEAD_DIM:(h + 1) * HEAD_DIM, :], ones_rows], axis=0)

    for t, (qt_ref, _, _, qi) in enumerate(tiles):
        qt = qt_ref[0, 0]
        for h in range(GROUP_HEADS):
            qh = jnp.where(head_of_feat == h, qt, jnp.zeros_like(qt))
            qh_ref[t, h] = qh
            gate = _dot(km_hi, qh) + _dot(km_lo, qh)
            gate = jnp.where(blk_id < qi, gate, -jnp.inf)
            sel = jnp.zeros((nb, blk), jnp.bool_)
            for r in range(min(MOBA_TOPK, nb)):
                top = jnp.max(gate, axis=0, keepdims=True)
                idx = jnp.min(jnp.where(gate == top, blk_id, nb), axis=0, keepdims=True)
                hit = (blk_id == idx) & (r < qi)
                sel = sel | hit
                gate = jnp.where(blk_id == idx, -jnp.inf, gate)
            bias_ref[t, h] = jnp.where(sel, 0.0, MASKED).astype(F32)

    def issue_scores(t, ki):
        kb = k_ref[0, pl.ds(pl.multiple_of(ki * blk, blk), blk), :]
        return [_dot(kb, qh_ref[t, h]) for h in range(GROUP_HEADS)]

    def finish_own(t, ki, scores):
        for h in range(GROUP_HEADS):
            s = jnp.where(causal, scores[h], MASKED)
            m = jnp.max(s, axis=0, keepdims=True)
            p = jnp.exp2(s - m).astype(BF16)
            m_ref[t, h] = m
            acc_ref[t, h] = _dot(values_ext(ki, h), p)

    def finish_past(t, ki, scores):
        for h in range(GROUP_HEADS):
            s = scores[h]
            bias = bias_ref[t, h, pl.ds(ki, 1), :]
            m_old = m_ref[t, h]
            m_new = jnp.maximum(m_old, jnp.max(s, axis=0, keepdims=True) + bias)
            alpha = jnp.exp2(m_old - m_new)
            p = jnp.exp2(s - (m_new - bias)).astype(BF16)
            m_ref[t, h] = m_new
            acc_ref[t, h] = alpha * acc_ref[t, h] + _dot(values_ext(ki, h), p)

    work = [(finish_own, 0, j), (finish_own, 1, nb - 1 - j)]
    for n in range(nb - 1):
        second = (n >= j).astype(jnp.int32)
        work.append((finish_past, second, n - second * j))
    pending = None
    for fin, t, ki in work:
        scores = issue_scores(t, ki)
        if pending is not None:
            pending[0](*pending[1:])
        pending = (fin, t, ki, scores)
    pending[0](*pending[1:])

    for t, (_, gat_ref, out_ref, _) in enumerate(tiles):
        for h in range(GROUP_HEADS):
            rows = slice(h * HEAD_DIM, (h + 1) * HEAD_DIM)
            acc = acc_ref[t, h]
            y = acc[0:HEAD_DIM] / acc[HEAD_DIM:HEAD_DIM + 1]
            out_ref[0, 0, rows, :] = (y * gat_ref[0, 0, rows, :].astype(F32)).astype(BF16)


def _moba(qt, k, kmean, vt, gat, batch, seq):
    nb = seq // MOBA_BLOCK
    assert nb % 2 == 0
    blk = MOBA_BLOCK
    groups = ATTN_WIDTH // GROUP_WIDTH
    k3 = k.reshape(batch, seq, ATTN_WIDTH)
    km3 = kmean.reshape(batch, nb, ATTN_WIDTH)
    tile_a = pl.BlockSpec((1, 1, GROUP_WIDTH, blk), lambda b, g, j: (b, j, g, 0))
    tile_b = pl.BlockSpec((1, 1, GROUP_WIDTH, blk), lambda b, g, j: (b, nb - 1 - j, g, 0))
    out_shape = jax.ShapeDtypeStruct((batch, nb // 2, ATTN_WIDTH, blk), BF16)
    return pl.pallas_call(
        functools.partial(_moba_kernel, nb=nb),
        grid=(batch, groups, nb // 2),
        in_specs=[
            tile_a,
            tile_b,
            pl.BlockSpec((1, seq, GROUP_WIDTH), lambda b, g, j: (b, 0, g)),
            pl.BlockSpec((1, nb, GROUP_WIDTH), lambda b, g, j: (b, 0, g)),
            pl.BlockSpec((1, nb, GROUP_WIDTH, blk), lambda b, g, j: (b, 0, g, 0)),
            tile_a,
            tile_b,
        ],
        out_specs=[tile_a, tile_a],
        out_shape=[out_shape, out_shape],
        scratch_shapes=[
            pltpu.VMEM((2, GROUP_HEADS, GROUP_WIDTH, blk), BF16),
            pltpu.VMEM((2, GROUP_HEADS, nb, blk), F32),
            pltpu.VMEM((2, GROUP_HEADS, 1, blk), F32),
            pltpu.VMEM((2, GROUP_HEADS, ACC_ROWS, blk), F32),
        ],
        compiler_params=pltpu.CompilerParams(
            dimension_semantics=("arbitrary", "arbitrary", "arbitrary"), vmem_limit_bytes=VMEM_LIMIT_BYTES),
        name="moba",
    )(qt, qt, k3, km3, vt, gat, gat)


def _block_diag(x_bf16, mask):
    tiled = jnp.concatenate([x_bf16] * GROUP_HEADS, axis=0)
    return jnp.where(mask, tiled, jnp.zeros_like(tiled))


def _rwkv_kernel(rkv_ref, lora_ref, gb_ref, wlora_ref, dbias_ref, ibias_ref, kk_ref, ka_ref, rk_ref,
                 gng_ref, gnb_ref, yat1_ref, yat2_ref, x_ref, woa_ref, wob_ref, gain_ref,
                 out_ref, state_ref, y_ref, *, nb):
    tile = rkv_ref.shape[0]
    c_w = RWKV_WIDTH

    @pl.when(pl.program_id(1) == 0)
    def _():
        state_ref[...] = jnp.zeros_like(state_ref)

    r = rkv_ref[:, 0:c_w].astype(F32)
    k = rkv_ref[:, c_w:2 * c_w].astype(F32)
    v = rkv_ref[:, 2 * c_w:3 * c_w].astype(F32)
    lora_in = lora_ref[...]
    lane128 = lax.broadcasted_iota(jnp.int32, (1, 2 * LORA_RANK), 1)
    lora_in = jnp.where(lane128 < LORA_RANK, jnp.tanh(lora_in), lora_in)
    li_hi, li_lo = _split_bf16(lora_in)
    w_hi = wlora_ref[0]
    w_lo = wlora_ref[1]
    lora = _dot(li_hi, w_hi) + _dot(li_lo, w_hi) + _dot(li_hi, w_lo)
    g = DECAY_SCALE / (1.0 + jnp.exp(-(dbias_ref[...] + lora[:, 0:c_w])))
    a = 1.0 / (1.0 + jnp.exp(-(ibias_ref[...] + lora[:, c_w:2 * c_w])))

    gr = lax.broadcasted_iota(jnp.int32, (GROUP_WIDTH, GROUP_WIDTH), 0) // HEAD_DIM
    gc = lax.broadcasted_iota(jnp.int32, (GROUP_WIDTH, GROUP_WIDTH), 1) // HEAD_DIM
    diag_mask = gr == gc

    head_ones = jnp.where(diag_mask, 1.0, 0.0).astype(BF16)

    def head_sum(t):
        t = t.astype(BF16)
        return jnp.concatenate([_dot(t[:, i:i + GROUP_WIDTH], head_ones)
                                for i in range(0, c_w, GROUP_WIDTH)], axis=-1)

    kk = k * kk_ref[...]
    kk = kk * lax.rsqrt(jnp.maximum(head_sum(kk * kk), NORMALIZE_EPS_SQ))
    k2 = k * (1.0 + (a - 1.0) * ka_ref[...])
    b = kk * a
    bonus = head_sum(r * k2 * rk_ref[...]) * v

    ti = lax.broadcasted_iota(jnp.int32, (CHUNK, CHUNK), 0)
    tj = lax.broadcasted_iota(jnp.int32, (CHUNK, CHUNK), 1)
    tri_incl = jnp.where(tj <= ti, 1.0, 0.0).astype(BF16)
    g_hi, g_lo = _split_bf16(g)

    t_idx = lax.broadcasted_iota(jnp.int32, (CHUNK, GROUP_WIDTH), 0)
    j_idx = lax.broadcasted_iota(jnp.int32, (CHUNK, GROUP_WIDTH), 1) % HEAD_DIM
    strict_lower = j_idx < t_idx
    lower = j_idx <= t_idx
    eye = (j_idx == t_idx).astype(F32)
    lane_head = lax.broadcasted_iota(jnp.int32, (1, GROUP_WIDTH), 1) // HEAD_DIM

    def bd(x):
        return _block_diag(x.astype(BF16), diag_mask)

    def diag_blocks(full):
        out = full[(GROUP_HEADS - 1) * HEAD_DIM:, :]
        for h in range(GROUP_HEADS - 2, -1, -1):
            out = jnp.where(lane_head == h, full[h * HEAD_DIM:(h + 1) * HEAD_DIM, :], out)
        return out

    n_groups = c_w // GROUP_WIDTH
    items = []
    for c in range(tile // CHUNK):
        rs = slice(c * CHUNK, (c + 1) * CHUNK)
        cum = _dot(tri_incl, g_hi[rs]) + _dot(tri_incl, g_lo[rs])
        dec_in = jnp.exp(cum)
        dec_ex = jnp.exp(cum - g[rs])
        inv = 1.0 / dec_in
        end_decay = dec_in[CHUNK - 1:CHUNK, :]
        to_end = inv * end_decay
        for gi in range(n_groups):
            ls = slice(gi * GROUP_WIDTH, (gi + 1) * GROUP_WIDTH)
            it = dict(rs=rs, ls=ls, gi=gi)
            it["a_t"] = (-kk[rs, ls] * dec_ex[:, ls]).astype(BF16)
            it["r_f"] = r[rs, ls] * dec_in[:, ls]
            b_t = (b[rs, ls] * inv[:, ls]).astype(BF16)
            k_t = (k2[rs, ls] * inv[:, ls]).astype(BF16)
            it["bk_e"] = jnp.concatenate([(b[rs, ls] * to_end[:, ls]).astype(BF16),
                                          (k2[rs, ls] * to_end[:, ls]).astype(BF16)], axis=0)
            it["v_c"] = v[rs, ls].astype(BF16)
            it["m_diag"] = eye * end_decay[:, ls]
            ar = jnp.concatenate([it["a_t"], it["r_f"].astype(BF16)], axis=0)
            pb = _dot_nt(ar, _block_diag(b_t, diag_mask))
            pk = _dot_nt(ar, _block_diag(k_t, diag_mask))
            it["pw"] = jnp.where(strict_lower, pb[0:CHUNK], 0.0)
            it["t_inv"] = eye + it["pw"]
            it["a_rb"] = jnp.where(lower, pb[CHUNK:], 0.0).astype(BF16)
            it["a_k"] = jnp.concatenate([jnp.where(strict_lower, pk[0:CHUNK], 0.0),
                                         jnp.where(lower, pk[CHUNK:], 0.0)], axis=0).astype(BF16)
            items.append(it)

    steps = CHUNK.bit_length() - 1
    for s in range(1, steps + 1):
        for it in items:
            w_p = bd(it["pw"])
            if s == 1:
                it["pw"] = _dot(it["pw"].astype(BF16), w_p)
            elif s < steps:
                both = _dot(jnp.concatenate([it["pw"], it["t_inv"]], axis=0).astype(BF16), w_p)
                it["t_inv"] = it["t_inv"] + both[CHUNK:]
                it["pw"] = both[0:CHUNK]
            else:
                it["t_inv"] = (it["t_inv"] + _dot(it["t_inv"].astype(BF16), w_p)).astype(BF16)

    for it in items:
        kv = _dot(it["a_k"], bd(it["v_c"]))
        it["akv"] = kv[0:CHUNK]
        it["arkv"] = kv[CHUNK:]
    for it in items:
        it["w_t"] = _dot(it["t_inv"], bd(it["a_t"])).astype(BF16)
        it["u_t"] = _dot(it["t_inv"], bd(it["akv"])).astype(BF16)
    for it in items:
        it["r_hat"] = (it["r_f"] + _dot(it["a_rb"], bd(it["w_t"]))).astype(BF16)
        it["y_hat"] = _dot(it["a_rb"], bd(it["u_t"])) + it["arkv"]
    for it in items:
        m_full = _dot_tn(it["bk_e"][0:CHUNK], it["w_t"])
        it["m"] = (it["m_diag"] + diag_blocks(m_full)).astype(BF16)
        n_full = _dot_tn(it["bk_e"], jnp.concatenate([it["u_t"], it["v_c"]], axis=0))
        it["n"] = diag_blocks(n_full)

    states = [state_ref[gi] for gi in range(n_groups)]
    for it in items:
        gi = it["gi"]
        res = _dot(jnp.concatenate([it["m"], it["r_hat"]], axis=0), bd(states[gi]))
        y_ref[it["rs"], it["ls"]] = res[CHUNK:] + it["y_hat"]
        states[gi] = res[0:CHUNK] + it["n"]
    for gi in range(n_groups):
        state_ref[gi] = states[gi]

    y = y_ref[...]
    mean = head_sum(y) * (1.0 / HEAD_DIM)
    d = y - mean
    var = head_sum(d * d) * (1.0 / HEAD_DIM)
    yn = d * lax.rsqrt(var + GN_EPS) * gng_ref[...] + gnb_ref[...]
    yb = ((yn + bonus) * gb_ref[...].astype(F32)).astype(BF16)

    in_first = pl.program_id(1) < nb // 2
    yat = jnp.where(in_first, yat1_ref[0, 0], yat2_ref[0, 0])
    h = x_ref[...] + _dot_tn(yat, woa_ref[...]) + _dot(yb, wob_ref[...])
    ms = jnp.mean(h * h, axis=-1, keepdims=True)
    out_ref[...] = (h * lax.rsqrt(ms + RMS_EPS) * gain_ref[...]).astype(out_ref.dtype)


def _rwkv(rkv, lora, gb, wlora, dbias, ibias, k_k, k_a, r_k, gn_gain, gn_bias,
          yat_first, yat_second, x2d, woa, wob, out_gain, batch, seq):
    tile = MOBA_BLOCK
    nt = seq // tile
    half = nt // 2
    d_model = x2d.shape[1]
    row_spec = lambda width: pl.BlockSpec((tile, width), lambda b, i: (b * nt + i, 0))
    const = lambda shape: pl.BlockSpec(shape, lambda b, i: (0,) * len(shape))
    vec = lambda width: const((1, width))
    blk_t = lambda pick: pl.BlockSpec((1, 1, ATTN_WIDTH, tile), lambda b, i: (b, jnp.minimum(pick(i), half - 1), 0, 0))
    return pl.pallas_call(
        functools.partial(_rwkv_kernel, nb=nt),
        grid=(batch, nt),
        in_specs=[
            row_spec(3 * RWKV_WIDTH),
            row_spec(2 * LORA_RANK),
            row_spec(RWKV_WIDTH),
            const((2, 2 * LORA_RANK, 2 * RWKV_WIDTH)),
        ] + [vec(RWKV_WIDTH)] * 7 + [
            blk_t(lambda i: i),
            blk_t(lambda i: nt - 1 - i),
            row_spec(d_model),
            const((ATTN_WIDTH, d_model)),
            const((RWKV_WIDTH, d_model)),
            vec(d_model),
        ],
        out_specs=row_spec(d_model),
        out_shape=jax.ShapeDtypeStruct(x2d.shape, x2d.dtype),
        scratch_shapes=[
            pltpu.VMEM((RWKV_WIDTH // GROUP_WIDTH, HEAD_DIM, GROUP_WIDTH), F32),
            pltpu.VMEM((tile, RWKV_WIDTH), F32),
        ],
        compiler_params=pltpu.CompilerParams(
            dimension_semantics=("arbitrary", "arbitrary"), vmem_limit_bytes=VMEM_LIMIT_BYTES),
        name="rwkv_out",
    )(rkv, lora, gb, wlora, dbias, ibias, k_k, k_a, r_k, gn_gain, gn_bias,
      yat_first, yat_second, x2d, woa, wob, out_gain)


def _rotary_column_order():
    order = []
    for g in range(ATTN_WIDTH // GROUP_WIDTH):
        for part in range(2):
            for hh in range(GROUP_HEADS):
                head = g * GROUP_HEADS + hh
                order.extend(head * HEAD_DIM + part * HALF + j for j in range(HALF))
    return np.asarray(order, np.int32)


def _layer(h2d, batch, seq, norm_gain, w_in, shift_mu, decay_bias, decay_up, iclr_bias, iclr_up,
           k_k, k_a, r_k, gn_gain, gn_bias, w_out, out_gain, cos, sin):
    nb = seq // MOBA_BLOCK
    aw, rw = ATTN_WIDTH, RWKV_WIDTH
    order = _rotary_column_order()
    b0 = 4 * aw
    w_in = w_in.astype(F32)
    wnn = jnp.concatenate([w_in[:, aw:2 * aw][:, order], w_in[:, b0:]], axis=1).astype(BF16)
    wnt = jnp.concatenate([w_in[:, 0:aw][:, order], w_in[:, 2 * aw:4 * aw]], axis=1).T.astype(BF16)
    zeros = jnp.zeros((LORA_RANK, rw), F32)
    wl = jnp.concatenate([jnp.concatenate([decay_up.astype(F32), zeros], axis=1),
                          jnp.concatenate([zeros, iclr_up.astype(F32)], axis=1)], axis=0)
    wl_hi = wl.astype(BF16)
    wl_lo = (wl - wl_hi.astype(F32)).astype(BF16)
    wlora = jnp.stack([wl_hi, wl_lo])
    row = lambda t: t.astype(F32).reshape(1, -1)

    cos_t = cos.reshape(nb, MOBA_BLOCK, 128).transpose(0, 2, 1)
    sin_t = sin.reshape(nb, MOBA_BLOCK, 128).transpose(0, 2, 1)
    qt, k, kmean, vt, gat, rkv, lora, gb = _in_proj(h2d, row(norm_gain), wnn, wnt, cos, sin, cos_t, sin_t,
                                                    row(shift_mu), batch, nb)
    yat_first, yat_second = _moba(qt, k, kmean, vt, gat, batch, seq)
    w_out = w_out.astype(BF16)
    return _rwkv(rkv, lora, gb, wlora, row(decay_bias), row(iclr_bias), row(k_k), row(k_a), row(r_k),
                 row(gn_gain), row(gn_bias), yat_first, yat_second, h2d, w_out[0:aw], w_out[aw:], out_gain,
                 batch, seq)


def kernel(x, norm_gain, w_in, shift_mu, decay_bias, decay_up, iclr_bias, iclr_up,
           k_k, k_a, r_k, gn_gain, gn_bias, w_out, final_gain):
    batch, seq, d_model = x.shape
    depth = norm_gain.shape[0]
    assert depth == 1, "the final RMSNorm is fused into the single layer's output projection"
    assert seq % MOBA_BLOCK == 0 and w_in.shape[-1] == 4 * ATTN_WIDTH + SHIFT_WIDTH + RWKV_WIDTH
    inv_freq = 1.0 / (ROPE_THETA ** (jnp.arange(0, HEAD_DIM, 2, dtype=F32) / HEAD_DIM))
    ang = jnp.arange(seq, dtype=F32)[:, None] * inv_freq[None, :]
    cos = jnp.tile(jnp.cos(ang), (1, GROUP_HEADS))
    sin = jnp.tile(jnp.sin(ang), (1, GROUP_HEADS))
    out = _layer(x.reshape(batch * seq, d_model).astype(F32), batch, seq, norm_gain[0], w_in[0],
                 shift_mu[0], decay_bias[0], decay_up[0], iclr_bias[0], iclr_up[0], k_k[0], k_a[0], r_k[0],
                 gn_gain[0], gn_bias[0], w_out[0], final_gain.astype(F32).reshape(1, -1), cos, sin)
    return out.reshape(batch, seq, d_model).astype(x.dtype)
```

```python
import functools

import jax
import jax.numpy as jnp
import numpy as np
from jax import lax
from jax.experimental import pallas as pl
from jax.experimental.pallas import tpu as pltpu

HEAD_DIM = 64
HALF = HEAD_DIM // 2
ATTN_HEADS = 8
ATTN_WIDTH = ATTN_HEADS * HEAD_DIM
RWKV_HEADS = 8
RWKV_WIDTH = RWKV_HEADS * HEAD_DIM
LORA_RANK = 64
SHIFT_WIDTH = 3 * RWKV_WIDTH + 2 * LORA_RANK
MOBA_BLOCK = 256
MOBA_TOPK = 3
ROPE_THETA = 10000.0
RMS_EPS = 1e-6
GN_EPS = 64e-5
NORMALIZE_EPS_SQ = 1e-24

GROUP_HEADS = 4
GROUP_WIDTH = GROUP_HEADS * HEAD_DIM
CHUNK = 64
RW_OPERANDS = 8
MASKED = -1e30
DECAY_SCALE = -float(np.exp(-0.5))
SCORE_BOUND_MARGIN = 1.0 + 2.0 ** -6
SCORE_BOUND_SAFE = 56.0
ACC_ROWS = HEAD_DIM + 16
LOG2E = 1.4426950408889634
Q_SCALE = HEAD_DIM ** -0.5 * LOG2E
VMEM_LIMIT_BYTES = 48 * 1024 * 1024

F32 = jnp.float32
BF16 = jnp.bfloat16


def _dot(a, b):
    return jnp.dot(a, b, preferred_element_type=F32)


def _dot_nt(a, b):
    return lax.dot_general(a, b, (((1,), (1,)), ((), ())), preferred_element_type=F32)


def _dot_tn(a, b):
    return lax.dot_general(a, b, (((0,), (0,)), ((), ())), preferred_element_type=F32)


def _split_bf16(x):
    hi = x.astype(BF16)
    lo = (x - hi.astype(F32)).astype(BF16)
    return hi, lo


def _silu(z):
    return z / (1.0 + jnp.exp(-z))


def _rotary(acc, cos, sin):
    outs = []
    for g in range(ATTN_WIDTH // GROUP_WIDTH):
        x1 = acc[:, g * GROUP_WIDTH:g * GROUP_WIDTH + 128]
        x2 = acc[:, g * GROUP_WIDTH + 128:(g + 1) * GROUP_WIDTH]
        outs.append(x1 * cos - x2 * sin)
        outs.append(x2 * cos + x1 * sin)
    return jnp.concatenate(outs, axis=-1)


def _rotary_t(acc, cos_t, sin_t):
    outs = []
    for g in range(ATTN_WIDTH // GROUP_WIDTH):
        x1 = acc[g * GROUP_WIDTH:g * GROUP_WIDTH + 128, :]
        x2 = acc[g * GROUP_WIDTH + 128:(g + 1) * GROUP_WIDTH, :]
        outs.append(x1 * cos_t - x2 * sin_t)
        outs.append(x2 * cos_t + x1 * sin_t)
    return jnp.concatenate(outs, axis=0)


def _head_sum(t):
    gr = lax.broadcasted_iota(jnp.int32, (GROUP_WIDTH, GROUP_WIDTH), 0) // HEAD_DIM
    gc = lax.broadcasted_iota(jnp.int32, (GROUP_WIDTH, GROUP_WIDTH), 1) // HEAD_DIM
    head_ones = jnp.where(gr == gc, 1.0, 0.0).astype(BF16)
    t = t.astype(BF16)
    return jnp.concatenate([_dot(t[:, i:i + GROUP_WIDTH], head_ones)
                            for i in range(0, t.shape[1], GROUP_WIDTH)], axis=-1)


def _in_proj_kernel(x_ref, gain_ref, wnn_ref, wnt_ref, cos_ref, sin_ref, cost_ref, sint_ref, mu_ref, hsel_ref,
                    wlora_ref, dbias_ref, ibias_ref, kk_ref, ka_ref, rk_ref,
                    qt_ref, k_ref, kmean_ref, kn2_ref, vt_ref, gat_ref, rw_ref, edec_ref, gb_ref, prev_ref,
                    *, nb):
    tile = x_ref.shape[0]
    x = x_ref[...]
    ms = jnp.mean(x * x, axis=-1, keepdims=True)
    u = (x * lax.rsqrt(ms + RMS_EPS) * gain_ref[...]).astype(BF16)

    k_att = _rotary(_dot(u, wnn_ref[:, 0:ATTN_WIDTH]), cos_ref[...], sin_ref[...])
    k_bf = k_att.astype(BF16)
    k_ref[...] = k_bf
    kmean_ref[0] = jnp.mean(k_att, axis=0, keepdims=True)
    k_sq = k_bf.astype(F32)
    k_sq = (k_sq * k_sq).astype(BF16)

    @pl.when(pl.program_id(0) % nb == 0)
    def _():
        prev_ref[...] = jnp.zeros_like(prev_ref)

    base = ATTN_WIDTH
    c_w = RWKV_WIDTH
    row = lax.broadcasted_iota(jnp.int32, (tile, 1), 0)

    def shifted_cols(c, w):
        p = _dot(u, wnn_ref[:, base + c:base + c + w])
        shifted = jnp.where(row == 0, prev_ref[0:1, c:c + w], pltpu.roll(p, 1, axis=0))
        prev_ref[0:1, c:c + w] = p[tile - 1:tile, :]
        return p + (shifted - p) * mu_ref[:, c:c + w]

    lora_in = shifted_cols(3 * c_w, 2 * LORA_RANK)
    lane128 = lax.broadcasted_iota(jnp.int32, (1, 2 * LORA_RANK), 1)
    lora_in = jnp.where(lane128 < LORA_RANK, jnp.tanh(lora_in), lora_in)
    li_hi, li_lo = _split_bf16(lora_in)
    k = shifted_cols(c_w, c_w)
    kk = k * kk_ref[...]
    r = shifted_cols(0, c_w)
    v = shifted_cols(2 * c_w, c_w)

    kn2_ref[0] = jnp.max(_dot(k_sq, hsel_ref[...]), axis=0, keepdims=True)
    w_hi = wlora_ref[0]
    w_lo = wlora_ref[1]
    lora = _dot(li_hi, w_hi) + _dot(li_lo, w_hi) + _dot(li_hi, w_lo)
    kk = kk * lax.rsqrt(jnp.maximum(_head_sum(kk * kk), NORMALIZE_EPS_SQ))

    gb = _dot(u, wnn_ref[:, base + SHIFT_WIDTH:base + SHIFT_WIDTH + RWKV_WIDTH])
    gb_ref[...] = _silu(gb).astype(BF16)

    g = DECAY_SCALE / (1.0 + jnp.exp(-(dbias_ref[...] + lora[:, 0:c_w])))
    a = 1.0 / (1.0 + jnp.exp(-(ibias_ref[...] + lora[:, c_w:2 * c_w])))
    g_hi, g_lo = _split_bf16(g)
    k2 = k * (1.0 + (a - 1.0) * ka_ref[...])
    b = kk * a

    qt = _rotary_t(_dot_nt(wnt_ref[0:ATTN_WIDTH, :], u), cost_ref[0], sint_ref[0])
    qt_ref[0, 0] = (qt * Q_SCALE).astype(BF16)

    ti = lax.broadcasted_iota(jnp.int32, (CHUNK, CHUNK), 0)
    tj = lax.broadcasted_iota(jnp.int32, (CHUNK, CHUNK), 1)
    tri_incl = jnp.where(tj <= ti, 1.0, 0.0).astype(BF16)
    cums = [_dot(tri_incl, g_hi[c:c + CHUNK]) + _dot(tri_incl, g_lo[c:c + CHUNK])
            for c in range(0, tile, CHUNK)]
    bonus = _head_sum(r * k2 * rk_ref[...]) * v

    for i, cum in enumerate(cums):
        rs = slice(i * CHUNK, (i + 1) * CHUNK)
        dec_in = jnp.exp(cum)
        dec_ex = jnp.exp(cum - g[rs])
        inv = 1.0 / dec_in
        end_decay = dec_in[CHUNK - 1:CHUNK, :]
        to_end = inv * end_decay
        edec_ref[i] = end_decay
        cols = (-kk[rs] * dec_ex, r[rs] * dec_in, b[rs] * inv, k2[rs] * inv,
                b[rs] * to_end, k2[rs] * to_end, v[rs], bonus[rs])
        for n, col in enumerate(cols):
            rw_ref[rs, n * c_w:(n + 1) * c_w] = col.astype(BF16)

    vt_ref[0, 0] = _dot_nt(wnt_ref[ATTN_WIDTH:2 * ATTN_WIDTH, :], u).astype(BF16)
    gat = _dot_nt(wnt_ref[2 * ATTN_WIDTH:3 * ATTN_WIDTH, :], u)
    gat_ref[0, 0] = _silu(gat).astype(BF16)


def _in_proj(x2d, gain, wnn, wnt, cos, sin, cos_t, sin_t, mu, hsel, wlora, rw_vecs, batch, nb):
    rows, d_model = x2d.shape
    tile = MOBA_BLOCK
    row_spec = lambda width: pl.BlockSpec((tile, width), lambda i: (i, 0))
    const = lambda shape: pl.BlockSpec(shape, lambda i: (0,) * len(shape))
    blk_t = pl.BlockSpec((1, 1, ATTN_WIDTH, tile), lambda i: (i // nb, i % nb, 0, 0))
    blk_t_shape = jax.ShapeDtypeStruct((batch, nb, ATTN_WIDTH, tile), BF16)
    return pl.pallas_call(
        functools.partial(_in_proj_kernel, nb=nb),
        grid=(rows // tile,),
        in_specs=[
            row_spec(d_model),
            const((1, d_model)),
            const(wnn.shape),
            const(wnt.shape),
            pl.BlockSpec((tile, 128), lambda i: (i % nb, 0)),
            pl.BlockSpec((tile, 128), lambda i: (i % nb, 0)),
            pl.BlockSpec((1, 128, tile), lambda i: (i % nb, 0, 0)),
            pl.BlockSpec((1, 128, tile), lambda i: (i % nb, 0, 0)),
            const((1, SHIFT_WIDTH)),
            const(hsel.shape),
            const(wlora.shape),
        ] + [const((1, RWKV_WIDTH))] * len(rw_vecs),
        out_specs=[
            blk_t,
            row_spec(ATTN_WIDTH),
            pl.BlockSpec((1, 1, ATTN_WIDTH), lambda i: (i, 0, 0)),
            pl.BlockSpec((1, 1, 128), lambda i: (i, 0, 0)),
            blk_t,
            blk_t,
            row_spec(RW_OPERANDS * RWKV_WIDTH),
            pl.BlockSpec((tile // CHUNK, 1, RWKV_WIDTH), lambda i: (i, 0, 0)),
            row_spec(RWKV_WIDTH),
        ],
        out_shape=[
            blk_t_shape,
            jax.ShapeDtypeStruct((rows, ATTN_WIDTH), BF16),
            jax.ShapeDtypeStruct((rows // tile, 1, ATTN_WIDTH), F32),
            jax.ShapeDtypeStruct((rows // tile, 1, 128), F32),
            blk_t_shape,
            blk_t_shape,
            jax.ShapeDtypeStruct((rows, RW_OPERANDS * RWKV_WIDTH), BF16),
            jax.ShapeDtypeStruct((rows // CHUNK, 1, RWKV_WIDTH), F32),
            jax.ShapeDtypeStruct((rows, RWKV_WIDTH), BF16),
        ],
        scratch_shapes=[pltpu.VMEM((8, SHIFT_WIDTH), F32)],
        compiler_params=pltpu.CompilerParams(
            dimension_semantics=("arbitrary",), vmem_limit_bytes=VMEM_LIMIT_BYTES),
        name="in_proj",
    )(x2d, gain, wnn, wnt, cos, sin, cos_t, sin_t, mu, hsel, wlora, *rw_vecs)


def _moba_kernel(qta_ref, qtb_ref, k_ref, kmean_ref, kn2_ref, vt_ref, gata_ref, gatb_ref, outa_ref, outb_ref,
                 qh_ref, bias_ref, m_ref, acc_ref, *, nb):
    j = pl.program_id(2)
    blk = MOBA_BLOCK
    tiles = ((qta_ref, gata_ref, outa_ref, j), (qtb_ref, gatb_ref, outb_ref, nb - 1 - j))
    feat = lax.broadcasted_iota(jnp.int32, (GROUP_WIDTH, 1), 0)
    head_of_feat = (feat % 128) // HALF
    ones_rows = jnp.ones((ACC_ROWS - HEAD_DIM, blk), BF16)
    km_hi, km_lo = _split_bf16(kmean_ref[0])
    blk_id = lax.broadcasted_iota(jnp.int32, (nb, blk), 0)
    k_pos = lax.broadcasted_iota(jnp.int32, (blk, blk), 0)
    q_pos = lax.broadcasted_iota(jnp.int32, (blk, blk), 1)
    causal = k_pos <= q_pos
    kn2 = kn2_ref[0]
    kn2_row = lax.broadcasted_iota(jnp.int32, kn2.shape, 0)
    kn2_lane = lax.broadcasted_iota(jnp.int32, (1, kn2.shape[1]), 1)
    first_head = pl.program_id(1) * GROUP_HEADS

    def values_ext(ki, h):
        return jnp.concatenate([vt_ref[0, ki, h * HEAD_DIM:(h + 1) * HEAD_DIM, :], ones_rows], axis=0)

    bounds = []
    for t, (qt_ref, _, _, qi) in enumerate(tiles):
        qt = qt_ref[0, 0]
        q_sq = qt.astype(F32)
        q_sq = q_sq * q_sq
        kn2_seen = jnp.max(jnp.where(kn2_row <= qi, kn2, 0.0), axis=0, keepdims=True)
        for h in range(GROUP_HEADS):
            qn2 = (jnp.sum(q_sq[h * HALF:(h + 1) * HALF], axis=0, keepdims=True)
                   + jnp.sum(q_sq[128 + h * HALF:128 + (h + 1) * HALF], axis=0, keepdims=True))
            kn2_h = jnp.sum(jnp.where(kn2_lane == first_head + h, kn2_seen, 0.0), axis=1, keepdims=True)
            bounds.append(jnp.sqrt(kn2_h * qn2) * SCORE_BOUND_MARGIN)
            qh = jnp.where(head_of_feat == h, qt, jnp.zeros_like(qt))
            qh_ref[t, h] = qh
            gate = _dot(km_hi, qh) + _dot(km_lo, qh)
            gate = jnp.where(blk_id < qi, gate, -jnp.inf)
            sel = jnp.zeros((nb, blk), jnp.bool_)
            for r in range(min(MOBA_TOPK, nb)):
                top = jnp.max(gate, axis=0, keepdims=True)
                idx = jnp.min(jnp.where(gate == top, blk_id, nb), axis=0, keepdims=True)
                hit = (blk_id == idx) & (r < qi)
                sel = sel | hit
                gate = jnp.where(blk_id == idx, -jnp.inf, gate)
            bias_ref[t, h] = jnp.where(sel, 0.0, MASKED).astype(F32)

    def issue_scores(t, ki):
        kb = k_ref[0, pl.ds(pl.multiple_of(ki * blk, blk), blk), :]
        return [_dot(kb, qh_ref[t, h]) for h in range(GROUP_HEADS)]

    def finish_own(t, ki, scores):
        for h in range(GROUP_HEADS):
            s = jnp.where(causal, scores[h], MASKED)
            m = jnp.max(s, axis=0, keepdims=True)
            p = jnp.exp2(s - m).astype(BF16)
            m_ref[t, h] = m
            acc_ref[t, h] = _dot(values_ext(ki, h), p)

    def finish_past(t, ki, scores):
        for h in range(GROUP_HEADS):
            s = scores[h]
            bias = bias_ref[t, h, pl.ds(ki, 1), :]
            m_old = m_ref[t, h]
            m_new = jnp.maximum(m_old, jnp.max(s, axis=0, keepdims=True) + bias)
            alpha = jnp.exp2(m_old - m_new)
            p = jnp.exp2(s - (m_new - bias)).astype(BF16)
            m_ref[t, h] = m_new
            acc_ref[t, h] = alpha * acc_ref[t, h] + _dot(values_ext(ki, h), p)

    def bounded_own(t, ki, scores):
        for h in range(GROUP_HEADS):
            s = jnp.where(causal, scores[h], MASKED)
            p = jnp.exp2(s - m_ref[t, h]).astype(BF16)
            acc_ref[t, h] = _dot(values_ext(ki, h), p)

    def bounded_past(t, ki, scores):
        for h in range(GROUP_HEADS):
            bias = bias_ref[t, h, pl.ds(ki, 1), :]
            p = jnp.exp2(scores[h] - (m_ref[t, h] - bias)).astype(BF16)
            acc_ref[t, h] = acc_ref[t, h] + _dot(values_ext(ki, h), p)

    def attend(own, past):
        work = [(own, 0, j), (own, 1, nb - 1 - j)]
        for n in range(nb - 1):
            second = (n >= j).astype(jnp.int32)
            work.append((past, second, n - second * j))
        pending = None
        for fin, t, ki in work:
            scores = issue_scores(t, ki)
            if pending is not None:
                pending[0](*pending[1:])
            pending = (fin, t, ki, scores)
        pending[0](*pending[1:])

    bound_ok = jnp.max(jnp.concatenate(bounds, axis=0)) < SCORE_BOUND_SAFE

    @pl.when(bound_ok)
    def _():
        for i, bound in enumerate(bounds):
            m_ref[i // GROUP_HEADS, i % GROUP_HEADS] = bound
        attend(bounded_own, bounded_past)

    @pl.when(jnp.logical_not(bound_ok))
    def _():
        attend(finish_own, finish_past)

    for t, (_, gat_ref, out_ref, _) in enumerate(tiles):
        for h in range(GROUP_HEADS):
            rows = slice(h * HEAD_DIM, (h + 1) * HEAD_DIM)
            acc = acc_ref[t, h]
            y = acc[0:HEAD_DIM] / acc[HEAD_DIM:HEAD_DIM + 1]
            out_ref[0, 0, rows, :] = (y * gat_ref[0, 0, rows, :].astype(F32)).astype(BF16)


def _moba(qt, k, kmean, kn2, vt, gat, batch, seq):
    nb = seq // MOBA_BLOCK
    assert nb % 2 == 0
    blk = MOBA_BLOCK
    groups = ATTN_WIDTH // GROUP_WIDTH
    k3 = k.reshape(batch, seq, ATTN_WIDTH)
    km3 = kmean.reshape(batch, nb, ATTN_WIDTH)
    kn3 = kn2.reshape(batch, nb, 128)
    tile_a = pl.BlockSpec((1, 1, GROUP_WIDTH, blk), lambda b, g, j: (b, j, g, 0))
    tile_b = pl.BlockSpec((1, 1, GROUP_WIDTH, blk), lambda b, g, j: (b, nb - 1 - j, g, 0))
    out_shape = jax.ShapeDtypeStruct((batch, nb // 2, ATTN_WIDTH, blk), BF16)
    return pl.pallas_call(
        functools.partial(_moba_kernel, nb=nb),
        grid=(batch, groups, nb // 2),
        in_specs=[
            tile_a,
            tile_b,
            pl.BlockSpec((1, seq, GROUP_WIDTH), lambda b, g, j: (b, 0, g)),
            pl.BlockSpec((1, nb, GROUP_WIDTH), lambda b, g, j: (b, 0, g)),
            pl.BlockSpec((1, nb, 128), lambda b, g, j: (b, 0, 0)),
            pl.BlockSpec((1, nb, GROUP_WIDTH, blk), lambda b, g, j: (b, 0, g, 0)),
            tile_a,
            tile_b,
        ],
        out_specs=[tile_a, tile_a],
        out_shape=[out_shape, out_shape],
        scratch_shapes=[
            pltpu.VMEM((2, GROUP_HEADS, GROUP_WIDTH, blk), BF16),
            pltpu.VMEM((2, GROUP_HEADS, nb, blk), F32),
            pltpu.VMEM((2, GROUP_HEADS, 1, blk), F32),
            pltpu.VMEM((2, GROUP_HEADS, ACC_ROWS, blk), F32),
        ],
        compiler_params=pltpu.CompilerParams(
            dimension_semantics=("arbitrary", "arbitrary", "arbitrary"), vmem_limit_bytes=VMEM_LIMIT_BYTES),
        name="moba",
    )(qt, qt, k3, km3, kn3, vt, gat, gat)


def _block_diag(x_bf16, mask):
    tiled = jnp.concatenate([x_bf16] * GROUP_HEADS, axis=0)
    return jnp.where(mask, tiled, jnp.zeros_like(tiled))


def _rwkv_kernel(rw_ref, edec_ref, gb_ref, gng_ref, gnb_ref, yat1_ref, yat2_ref, x_ref, woa_ref, wob_ref,
                 gain_ref, out_ref, state_ref, y_ref, *, nb):
    tile = rw_ref.shape[0]
    c_w = RWKV_WIDTH

    @pl.when(pl.program_id(1) == 0)
    def _():
        state_ref[...] = jnp.zeros_like(state_ref)

    gr = lax.broadcasted_iota(jnp.int32, (GROUP_WIDTH, GROUP_WIDTH), 0) // HEAD_DIM
    gc = lax.broadcasted_iota(jnp.int32, (GROUP_WIDTH, GROUP_WIDTH), 1) // HEAD_DIM
    diag_mask = gr == gc
    t_idx = lax.broadcasted_iota(jnp.int32, (CHUNK, GROUP_WIDTH), 0)
    j_idx = lax.broadcasted_iota(jnp.int32, (CHUNK, GROUP_WIDTH), 1) % HEAD_DIM
    strict_lower = j_idx < t_idx
    lower = j_idx <= t_idx
    eye = (j_idx == t_idx).astype(F32)
    lane_head = lax.broadcasted_iota(jnp.int32, (1, GROUP_WIDTH), 1) // HEAD_DIM

    def bd(x):
        return _block_diag(x.astype(BF16), diag_mask)

    def diag_blocks(full):
        out = full[(GROUP_HEADS - 1) * HEAD_DIM:, :]
        for h in range(GROUP_HEADS - 2, -1, -1):
            out = jnp.where(lane_head == h, full[h * HEAD_DIM:(h + 1) * HEAD_DIM, :], out)
        return out

    n_groups = c_w // GROUP_WIDTH
    items = []
    for c in range(tile // CHUNK):
        rs = slice(c * CHUNK, (c + 1) * CHUNK)
        end_decay = edec_ref[c]
        for gi in range(n_groups):
            ls = slice(gi * GROUP_WIDTH, (gi + 1) * GROUP_WIDTH)
            operand = lambda n: rw_ref[rs, n * c_w + gi * GROUP_WIDTH:n * c_w + (gi + 1) * GROUP_WIDTH]
            it = dict(rs=rs, ls=ls, gi=gi)
            it["a_t"] = operand(0)
            it["r_t"] = operand(1)
            it["bk_e"] = jnp.concatenate([operand(4), operand(5)], axis=0)
            it["v_c"] = operand(6)
            it["m_diag"] = eye * end_decay[:, ls]
            ar = jnp.concatenate([it["a_t"], it["r_t"]], axis=0)
            pb = _dot_nt(ar, _block_diag(operand(2), diag_mask))
            pk = _dot_nt(ar, _block_diag(operand(3), diag_mask))
            it["pw"] = jnp.where(strict_lower, pb[0:CHUNK], 0.0)
            it["t_inv"] = eye + it["pw"]
            it["a_rb"] = jnp.where(lower, pb[CHUNK:], 0.0).astype(BF16)
            it["a_k"] = jnp.concatenate([jnp.where(strict_lower, pk[0:CHUNK], 0.0),
                                         jnp.where(lower, pk[CHUNK:], 0.0)], axis=0).astype(BF16)
            items.append(it)

    steps = CHUNK.bit_length() - 1
    for s in range(1, steps + 1):
        for it in items:
            w_p = bd(it["pw"])
            if s == 1:
                it["pw"] = _dot(it["pw"].astype(BF16), w_p)
            elif s < steps:
                both = _dot(jnp.concatenate([it["pw"], it["t_inv"]], axis=0).astype(BF16), w_p)
                it["t_inv"] = it["t_inv"] + both[CHUNK:]
                it["pw"] = both[0:CHUNK]
            else:
                it["t_inv"] = (it["t_inv"] + _dot(it["t_inv"].astype(BF16), w_p)).astype(BF16)

    for it in items:
        kv = _dot(it["a_k"], bd(it["v_c"]))
        it["akv"] = kv[0:CHUNK]
        it["arkv"] = kv[CHUNK:]
    for it in items:
        it["w_t"] = _dot(it["t_inv"], bd(it["a_t"])).astype(BF16)
        it["u_t"] = _dot(it["t_inv"], bd(it["akv"])).astype(BF16)
    for it in items:
        it["r_hat"] = (it["r_t"].astype(F32) + _dot(it["a_rb"], bd(it["w_t"]))).astype(BF16)
        it["y_hat"] = _dot(it["a_rb"], bd(it["u_t"])) + it["arkv"]
    for it in items:
        m_full = _dot_tn(it["bk_e"][0:CHUNK], it["w_t"])
        it["m"] = (it["m_diag"] + diag_blocks(m_full)).astype(BF16)
        n_full = _dot_tn(it["bk_e"], jnp.concatenate([it["u_t"], it["v_c"]], axis=0))
        it["n"] = diag_blocks(n_full)

    states = [state_ref[gi] for gi in range(n_groups)]
    for it in items:
        gi = it["gi"]
        res = _dot(jnp.concatenate([it["m"], it["r_hat"]], axis=0), bd(states[gi]))
        y_ref[it["rs"], it["ls"]] = res[CHUNK:] + it["y_hat"]
        states[gi] = res[0:CHUNK] + it["n"]
    for gi in range(n_groups):
        state_ref[gi] = states[gi]

    y = y_ref[...]
    mean = _head_sum(y) * (1.0 / HEAD_DIM)
    d = y - mean
    var = _head_sum(d * d) * (1.0 / HEAD_DIM)
    yn = d * lax.rsqrt(var + GN_EPS) * gng_ref[...] + gnb_ref[...]
    bonus = rw_ref[:, (RW_OPERANDS - 1) * c_w:].astype(F32)
    yb = ((yn + bonus) * gb_ref[...].astype(F32)).astype(BF16)

    in_first = pl.program_id(1) < nb // 2
    yat = jnp.where(in_first, yat1_ref[0, 0], yat2_ref[0, 0])
    h = x_ref[...] + _dot_tn(yat, woa_ref[...]) + _dot(yb, wob_ref[...])
    ms = jnp.mean(h * h, axis=-1, keepdims=True)
    out_ref[...] = (h * lax.rsqrt(ms + RMS_EPS) * gain_ref[...]).astype(out_ref.dtype)


def _rwkv(rw, edec, gb, gn_gain, gn_bias, yat_first, yat_second, x2d, woa, wob, out_gain, batch, seq):
    tile = MOBA_BLOCK
    nt = seq // tile
    half = nt // 2
    d_model = x2d.shape[1]
    row_spec = lambda width: pl.BlockSpec((tile, width), lambda b, i: (b * nt + i, 0))
    const = lambda shape: pl.BlockSpec(shape, lambda b, i: (0,) * len(shape))
    vec = lambda width: const((1, width))
    blk_t = lambda pick: pl.BlockSpec((1, 1, ATTN_WIDTH, tile), lambda b, i: (b, jnp.minimum(pick(i), half - 1), 0, 0))
    return pl.pallas_call(
        functools.partial(_rwkv_kernel, nb=nt),
        grid=(batch, nt),
        in_specs=[
            row_spec(RW_OPERANDS * RWKV_WIDTH),
            pl.BlockSpec((tile // CHUNK, 1, RWKV_WIDTH), lambda b, i: (b * nt + i, 0, 0)),
            row_spec(RWKV_WIDTH),
            vec(RWKV_WIDTH),
            vec(RWKV_WIDTH),
            blk_t(lambda i: i),
            blk_t(lambda i: nt - 1 - i),
            row_spec(d_model),
            const((ATTN_WIDTH, d_model)),
            const((RWKV_WIDTH, d_model)),
            vec(d_model),
        ],
        out_specs=row_spec(d_model),
        out_shape=jax.ShapeDtypeStruct(x2d.shape, x2d.dtype),
        scratch_shapes=[
            pltpu.VMEM((RWKV_WIDTH // GROUP_WIDTH, HEAD_DIM, GROUP_WIDTH), F32),
            pltpu.VMEM((tile, RWKV_WIDTH), F32),
        ],
        compiler_params=pltpu.CompilerParams(
            dimension_semantics=("arbitrary", "arbitrary"), vmem_limit_bytes=VMEM_LIMIT_BYTES),
        name="rwkv_out",
    )(rw, edec, gb, gn_gain, gn_bias, yat_first, yat_second, x2d, woa, wob, out_gain)


def _rotary_column_order():
    order = []
    for g in range(ATTN_WIDTH // GROUP_WIDTH):
        for part in range(2):
            for hh in range(GROUP_HEADS):
                head = g * GROUP_HEADS + hh
                order.extend(head * HEAD_DIM + part * HALF + j for j in range(HALF))
    return np.asarray(order, np.int32)


def _layer(h2d, batch, seq, norm_gain, w_in, shift_mu, decay_bias, decay_up, iclr_bias, iclr_up,
           k_k, k_a, r_k, gn_gain, gn_bias, w_out, out_gain, cos, sin):
    nb = seq // MOBA_BLOCK
    aw, rw = ATTN_WIDTH, RWKV_WIDTH
    order = _rotary_column_order()
    b0 = 4 * aw
    w_in = w_in.astype(F32)
    wnn = jnp.concatenate([w_in[:, aw:2 * aw][:, order], w_in[:, b0:]], axis=1).astype(BF16)
    wnt = jnp.concatenate([w_in[:, 0:aw][:, order], w_in[:, 2 * aw:4 * aw]], axis=1).T.astype(BF16)
    zeros = jnp.zeros((LORA_RANK, rw), F32)
    wl = jnp.concatenate([jnp.concatenate([decay_up.astype(F32), zeros], axis=1),
                          jnp.concatenate([zeros, iclr_up.astype(F32)], axis=1)], axis=0)
    wl_hi = wl.astype(BF16)
    wl_lo = (wl - wl_hi.astype(F32)).astype(BF16)
    wlora = jnp.stack([wl_hi, wl_lo])
    row = lambda t: t.astype(F32).reshape(1, -1)

    cos_t = cos.reshape(nb, MOBA_BLOCK, 128).transpose(0, 2, 1)
    sin_t = sin.reshape(nb, MOBA_BLOCK, 128).transpose(0, 2, 1)
    lane = np.arange(aw)
    head_of_lane = (lane // GROUP_WIDTH) * GROUP_HEADS + (lane % 128) // HALF
    hsel = jnp.asarray(head_of_lane[:, None] == np.arange(128)[None, :], BF16)
    rw_vecs = [row(decay_bias), row(iclr_bias), row(k_k), row(k_a), row(r_k)]
    qt, k, kmean, kn2, vt, gat, rw, edec, gb = _in_proj(h2d, row(norm_gain), wnn, wnt, cos, sin, cos_t, sin_t,
                                                        row(shift_mu), hsel, wlora, rw_vecs, batch, nb)
    yat_first, yat_second = _moba(qt, k, kmean, kn2, vt, gat, batch, seq)
    w_out = w_out.astype(BF16)
    return _rwkv(rw, edec, gb, row(gn_gain), row(gn_bias), yat_first, yat_second, h2d, w_out[0:aw], w_out[aw:],
                 out_gain, batch, seq)


def kernel(x, norm_gain, w_in, shift_mu, decay_bias, decay_up, iclr_bias, iclr_up,
           k_k, k_a, r_k, gn_gain, gn_bias, w_out, final_gain):
    batch, seq, d_model = x.shape
    depth = norm_gain.shape[0]
    assert depth == 1, "the final RMSNorm is fused into the single layer's output projection"
    assert seq % MOBA_BLOCK == 0 and w_in.shape[-1] == 4 * ATTN_WIDTH + SHIFT_WIDTH + RWKV_WIDTH
    inv_freq = 1.0 / (ROPE_THETA ** (jnp.arange(0, HEAD_DIM, 2, dtype=F32) / HEAD_DIM))
    ang = jnp.arange(seq, dtype=F32)[:, None] * inv_freq[None, :]
    cos = jnp.tile(jnp.cos(ang), (1, GROUP_HEADS))
    sin = jnp.tile(jnp.sin(ang), (1, GROUP_HEADS))
    out = _layer(x.reshape(batch * seq, d_model).astype(F32), batch, seq, norm_gain[0], w_in[0],
                 shift_mu[0], decay_bias[0], decay_up[0], iclr_bias[0], iclr_up[0], k_k[0], k_a[0], r_k[0],
                 gn_gain[0], gn_bias[0], w_out[0], final_gain.astype(F32).reshape(1, -1), cos, sin)
    return out.reshape(batch, seq, d_model).astype(x.dtype)
```

```python
import functools

import jax
import jax.numpy as jnp
import numpy as np
from jax import lax
from jax.experimental import pallas as pl
from jax.experimental.pallas import tpu as pltpu

HEAD_DIM = 64
HALF = HEAD_DIM // 2
ATTN_HEADS = 8
ATTN_WIDTH = ATTN_HEADS * HEAD_DIM
RWKV_HEADS = 8
RWKV_WIDTH = RWKV_HEADS * HEAD_DIM
LORA_RANK = 64
SHIFT_WIDTH = 3 * RWKV_WIDTH + 2 * LORA_RANK
MOBA_BLOCK = 256
MOBA_TOPK = 3
ROPE_THETA = 10000.0
RMS_EPS = 1e-6
GN_EPS = 64e-5
NORMALIZE_EPS_SQ = 1e-24

GROUP_HEADS = 4
GROUP_WIDTH = GROUP_HEADS * HEAD_DIM
CHUNK = 64
ROW_TILE = 2 * MOBA_BLOCK
BLOCKS_PER_TILE = ROW_TILE // MOBA_BLOCK
RW_OPERANDS = 8
MASKED = -1e30
DECAY_SCALE = -float(np.exp(-0.5))
SCORE_BOUND_MARGIN = 1.0 + 2.0 ** -6
SCORE_BOUND_SAFE = 56.0
ACC_ROWS = HEAD_DIM + 16
LOG2E = 1.4426950408889634
Q_SCALE = HEAD_DIM ** -0.5 * LOG2E
VMEM_LIMIT_BYTES = 48 * 1024 * 1024

F32 = jnp.float32
BF16 = jnp.bfloat16


def _dot(a, b):
    return jnp.dot(a, b, preferred_element_type=F32)


def _dot_nt(a, b):
    return lax.dot_general(a, b, (((1,), (1,)), ((), ())), preferred_element_type=F32)


def _dot_tn(a, b):
    return lax.dot_general(a, b, (((0,), (0,)), ((), ())), preferred_element_type=F32)


def _split_bf16(x):
    hi = x.astype(BF16)
    lo = (x - hi.astype(F32)).astype(BF16)
    return hi, lo


def _silu(z):
    return z / (1.0 + jnp.exp(-z))


def _rotary(acc, cos, sin):
    outs = []
    for g in range(ATTN_WIDTH // GROUP_WIDTH):
        x1 = acc[:, g * GROUP_WIDTH:g * GROUP_WIDTH + 128]
        x2 = acc[:, g * GROUP_WIDTH + 128:(g + 1) * GROUP_WIDTH]
        outs.append(x1 * cos - x2 * sin)
        outs.append(x2 * cos + x1 * sin)
    return jnp.concatenate(outs, axis=-1)


def _rotary_t(acc, cos_t, sin_t):
    outs = []
    for g in range(ATTN_WIDTH // GROUP_WIDTH):
        x1 = acc[g * GROUP_WIDTH:g * GROUP_WIDTH + 128, :]
        x2 = acc[g * GROUP_WIDTH + 128:(g + 1) * GROUP_WIDTH, :]
        outs.append(x1 * cos_t - x2 * sin_t)
        outs.append(x2 * cos_t + x1 * sin_t)
    return jnp.concatenate(outs, axis=0)


def _head_sum(t):
    gr = lax.broadcasted_iota(jnp.int32, (GROUP_WIDTH, GROUP_WIDTH), 0) // HEAD_DIM
    gc = lax.broadcasted_iota(jnp.int32, (GROUP_WIDTH, GROUP_WIDTH), 1) // HEAD_DIM
    head_ones = jnp.where(gr == gc, 1.0, 0.0).astype(BF16)
    t = t.astype(BF16)
    return jnp.concatenate([_dot(t[:, i:i + GROUP_WIDTH], head_ones)
                            for i in range(0, t.shape[1], GROUP_WIDTH)], axis=-1)


def _in_proj_block(s, x_ref, gain_ref, wnn_ref, wnt_ref, cos_ref, sin_ref, cost_ref, sint_ref, mu_ref, hsel_ref,
                   wlora_ref, dbias_ref, ibias_ref, kk_ref, ka_ref, rk_ref,
                   qt_ref, k_ref, kmean_ref, kn2_ref, vt_ref, gat_ref, rw_ref, edec_ref, gb_ref, prev_ref):
    tile = MOBA_BLOCK
    rows = slice(s * tile, (s + 1) * tile)
    x = x_ref[rows, :]
    ms = jnp.mean(x * x, axis=-1, keepdims=True)
    u = (x * lax.rsqrt(ms + RMS_EPS) * gain_ref[...]).astype(BF16)

    k_att = _rotary(_dot(u, wnn_ref[:, 0:ATTN_WIDTH]), cos_ref[rows, :], sin_ref[rows, :])
    k_bf = k_att.astype(BF16)
    k_ref[rows, :] = k_bf
    kmean_ref[s] = jnp.mean(k_att, axis=0, keepdims=True)
    k_sq = k_bf.astype(F32)
    k_sq = (k_sq * k_sq).astype(BF16)

    base = ATTN_WIDTH
    c_w = RWKV_WIDTH
    row = lax.broadcasted_iota(jnp.int32, (tile, 1), 0)

    def shifted_cols(c, w):
        p = _dot(u, wnn_ref[:, base + c:base + c + w])
        shifted = jnp.where(row == 0, prev_ref[0:1, c:c + w], pltpu.roll(p, 1, axis=0))
        prev_ref[0:1, c:c + w] = p[tile - 1:tile, :]
        return p + (shifted - p) * mu_ref[:, c:c + w]

    lora_in = shifted_cols(3 * c_w, 2 * LORA_RANK)
    lane128 = lax.broadcasted_iota(jnp.int32, (1, 2 * LORA_RANK), 1)
    lora_in = jnp.where(lane128 < LORA_RANK, jnp.tanh(lora_in), lora_in)
    li_hi, li_lo = _split_bf16(lora_in)
    k = shifted_cols(c_w, c_w)
    kk = k * kk_ref[...]
    r = shifted_cols(0, c_w)
    v = shifted_cols(2 * c_w, c_w)

    kn2_ref[s] = jnp.max(_dot(k_sq, hsel_ref[...]), axis=0, keepdims=True)
    w_hi = wlora_ref[0]
    w_lo = wlora_ref[1]
    lora = _dot(li_hi, w_hi) + _dot(li_lo, w_hi) + _dot(li_hi, w_lo)
    kk = kk * lax.rsqrt(jnp.maximum(_head_sum(kk * kk), NORMALIZE_EPS_SQ))

    gb = _dot(u, wnn_ref[:, base + SHIFT_WIDTH:base + SHIFT_WIDTH + RWKV_WIDTH])
    gb_ref[rows, :] = _silu(gb).astype(BF16)

    g = DECAY_SCALE / (1.0 + jnp.exp(-(dbias_ref[...] + lora[:, 0:c_w])))
    a = 1.0 / (1.0 + jnp.exp(-(ibias_ref[...] + lora[:, c_w:2 * c_w])))
    g_hi, g_lo = _split_bf16(g)
    k2 = k * (1.0 + (a - 1.0) * ka_ref[...])
    b = kk * a

    qt = _rotary_t(_dot_nt(wnt_ref[0:ATTN_WIDTH, :], u), cost_ref[s], sint_ref[s])
    qt_ref[0, s] = (qt * Q_SCALE).astype(BF16)

    ti = lax.broadcasted_iota(jnp.int32, (CHUNK, CHUNK), 0)
    tj = lax.broadcasted_iota(jnp.int32, (CHUNK, CHUNK), 1)
    tri_incl = jnp.where(tj <= ti, 1.0, 0.0).astype(BF16)
    cums = [_dot(tri_incl, g_hi[c:c + CHUNK]) + _dot(tri_incl, g_lo[c:c + CHUNK])
            for c in range(0, tile, CHUNK)]
    bonus = _head_sum(r * k2 * rk_ref[...]) * v

    for i, cum in enumerate(cums):
        rs = slice(i * CHUNK, (i + 1) * CHUNK)
        dec_in = jnp.exp(cum)
        dec_ex = jnp.exp(cum - g[rs])
        inv = 1.0 / dec_in
        end_decay = dec_in[CHUNK - 1:CHUNK, :]
        to_end = inv * end_decay
        edec_ref[s * (tile // CHUNK) + i] = end_decay
        cols = (-kk[rs] * dec_ex, r[rs] * dec_in, b[rs] * inv, k2[rs] * inv,
                b[rs] * to_end, k2[rs] * to_end, v[rs], bonus[rs])
        for n, col in enumerate(cols):
            rw_ref[s * tile + i * CHUNK:s * tile + (i + 1) * CHUNK, n * c_w:(n + 1) * c_w] = col.astype(BF16)

    vt_ref[0, s] = _dot_nt(wnt_ref[ATTN_WIDTH:2 * ATTN_WIDTH, :], u).astype(BF16)
    gat = _dot_nt(wnt_ref[2 * ATTN_WIDTH:3 * ATTN_WIDTH, :], u)
    gat_ref[0, s] = _silu(gat).astype(BF16)


def _in_proj_kernel(x_ref, gain_ref, wnn_ref, wnt_ref, cos_ref, sin_ref, cost_ref, sint_ref, mu_ref, hsel_ref,
                    wlora_ref, dbias_ref, ibias_ref, kk_ref, ka_ref, rk_ref,
                    qt_ref, k_ref, kmean_ref, kn2_ref, vt_ref, gat_ref, rw_ref, edec_ref, gb_ref, prev_ref,
                    *, nb):
    @pl.when(pl.program_id(0) % (nb // BLOCKS_PER_TILE) == 0)
    def _():
        prev_ref[...] = jnp.zeros_like(prev_ref)

    for s in range(BLOCKS_PER_TILE):
        _in_proj_block(s, x_ref, gain_ref, wnn_ref, wnt_ref, cos_ref, sin_ref, cost_ref, sint_ref, mu_ref,
                       hsel_ref, wlora_ref, dbias_ref, ibias_ref, kk_ref, ka_ref, rk_ref,
                       qt_ref, k_ref, kmean_ref, kn2_ref, vt_ref, gat_ref, rw_ref, edec_ref, gb_ref, prev_ref)


def _in_proj(x2d, gain, wnn, wnt, cos, sin, cos_t, sin_t, mu, hsel, wlora, rw_vecs, batch, nb):
    rows, d_model = x2d.shape
    tile, per = ROW_TILE, BLOCKS_PER_TILE
    assert nb % per == 0
    nt = nb // per
    row_spec = lambda width: pl.BlockSpec((tile, width), lambda i: (i, 0))
    const = lambda shape: pl.BlockSpec(shape, lambda i: (0,) * len(shape))
    blk_t = pl.BlockSpec((1, per, ATTN_WIDTH, MOBA_BLOCK), lambda i: (i // nt, i % nt, 0, 0))
    blk_t_shape = jax.ShapeDtypeStruct((batch, nb, ATTN_WIDTH, MOBA_BLOCK), BF16)
    return pl.pallas_call(
        functools.partial(_in_proj_kernel, nb=nb),
        grid=(rows // tile,),
        in_specs=[
            row_spec(d_model),
            const((1, d_model)),
            const(wnn.shape),
            const(wnt.shape),
            pl.BlockSpec((tile, 128), lambda i: (i % nt, 0)),
            pl.BlockSpec((tile, 128), lambda i: (i % nt, 0)),
            pl.BlockSpec((per, 128, MOBA_BLOCK), lambda i: (i % nt, 0, 0)),
            pl.BlockSpec((per, 128, MOBA_BLOCK), lambda i: (i % nt, 0, 0)),
            const((1, SHIFT_WIDTH)),
            const(hsel.shape),
            const(wlora.shape),
        ] + [const((1, RWKV_WIDTH))] * len(rw_vecs),
        out_specs=[
            blk_t,
            row_spec(ATTN_WIDTH),
            pl.BlockSpec((per, 1, ATTN_WIDTH), lambda i: (i, 0, 0)),
            pl.BlockSpec((per, 1, 128), lambda i: (i, 0, 0)),
            blk_t,
            blk_t,
            row_spec(RW_OPERANDS * RWKV_WIDTH),
            pl.BlockSpec((tile // CHUNK, 1, RWKV_WIDTH), lambda i: (i, 0, 0)),
            row_spec(RWKV_WIDTH),
        ],
        out_shape=[
            blk_t_shape,
            jax.ShapeDtypeStruct((rows, ATTN_WIDTH), BF16),
            jax.ShapeDtypeStruct((rows // MOBA_BLOCK, 1, ATTN_WIDTH), F32),
            jax.ShapeDtypeStruct((rows // MOBA_BLOCK, 1, 128), F32),
            blk_t_shape,
            blk_t_shape,
            jax.ShapeDtypeStruct((rows, RW_OPERANDS * RWKV_WIDTH), BF16),
            jax.ShapeDtypeStruct((rows // CHUNK, 1, RWKV_WIDTH), F32),
            jax.ShapeDtypeStruct((rows, RWKV_WIDTH), BF16),
        ],
        scratch_shapes=[pltpu.VMEM((8, SHIFT_WIDTH), F32)],
        compiler_params=pltpu.CompilerParams(
            dimension_semantics=("arbitrary",), vmem_limit_bytes=VMEM_LIMIT_BYTES),
        name="in_proj",
    )(x2d, gain, wnn, wnt, cos, sin, cos_t, sin_t, mu, hsel, wlora, *rw_vecs)


def _moba_kernel(qta_ref, qtb_ref, k_ref, kmean_ref, kn2_ref, vt_ref, gata_ref, gatb_ref, outa_ref, outb_ref,
                 qh_ref, bias_ref, m_ref, acc_ref, *, nb):
    j = pl.program_id(2)
    blk = MOBA_BLOCK
    tiles = ((qta_ref, gata_ref, outa_ref, j), (qtb_ref, gatb_ref, outb_ref, nb - 1 - j))
    feat = lax.broadcasted_iota(jnp.int32, (GROUP_WIDTH, 1), 0)
    head_of_feat = (feat % 128) // HALF
    ones_rows = jnp.ones((ACC_ROWS - HEAD_DIM, blk), BF16)
    km_hi, km_lo = _split_bf16(kmean_ref[0])
    blk_id = lax.broadcasted_iota(jnp.int32, (nb, blk), 0)
    k_pos = lax.broadcasted_iota(jnp.int32, (blk, blk), 0)
    q_pos = lax.broadcasted_iota(jnp.int32, (blk, blk), 1)
    causal = k_pos <= q_pos
    kn2 = kn2_ref[0]
    kn2_row = lax.broadcasted_iota(jnp.int32, kn2.shape, 0)
    kn2_lane = lax.broadcasted_iota(jnp.int32, (1, kn2.shape[1]), 1)
    first_head = pl.program_id(1) * GROUP_HEADS

    def values_ext(ki, h):
        return jnp.concatenate([vt_ref[0, ki, h * HEAD_DIM:(h + 1) * HEAD_DIM, :], ones_rows], axis=0)

    bounds = []
    for t, (qt_ref, _, _, qi) in enumerate(tiles):
        qt = qt_ref[0, 0]
        q_sq = qt.astype(F32)
        q_sq = q_sq * q_sq
        kn2_seen = jnp.max(jnp.where(kn2_row <= qi, kn2, 0.0), axis=0, keepdims=True)
        for h in range(GROUP_HEADS):
            qn2 = (jnp.sum(q_sq[h * HALF:(h + 1) * HALF], axis=0, keepdims=True)
                   + jnp.sum(q_sq[128 + h * HALF:128 + (h + 1) * HALF], axis=0, keepdims=True))
            kn2_h = jnp.sum(jnp.where(kn2_lane == first_head + h, kn2_seen, 0.0), axis=1, keepdims=True)
            bounds.append(jnp.sqrt(kn2_h * qn2) * SCORE_BOUND_MARGIN)
            qh = jnp.where(head_of_feat == h, qt, jnp.zeros_like(qt))
            qh_ref[t, h] = qh
            gate = _dot(km_hi, qh) + _dot(km_lo, qh)
            gate = jnp.where(blk_id < qi, gate, -jnp.inf)
            sel = jnp.zeros((nb, blk), jnp.bool_)
            for r in range(min(MOBA_TOPK, nb)):
                top = jnp.max(gate, axis=0, keepdims=True)
                idx = jnp.min(jnp.where(gate == top, blk_id, nb), axis=0, keepdims=True)
                hit = (blk_id == idx) & (r < qi)
                sel = sel | hit
                gate = jnp.where(blk_id == idx, -jnp.inf, gate)
            bias_ref[t, h] = jnp.where(sel, 0.0, MASKED).astype(F32)

    def issue_scores(t, ki):
        kb = k_ref[0, pl.ds(pl.multiple_of(ki * blk, blk), blk), :]
        return [_dot(kb, qh_ref[t, h]) for h in range(GROUP_HEADS)]

    def finish_own(t, ki, scores):
        for h in range(GROUP_HEADS):
            s = jnp.where(causal, scores[h], MASKED)
            m = jnp.max(s, axis=0, keepdims=True)
            p = jnp.exp2(s - m).astype(BF16)
            m_ref[t, h] = m
            acc_ref[t, h] = _dot(values_ext(ki, h), p)

    def finish_past(t, ki, scores):
        for h in range(GROUP_HEADS):
            s = scores[h]
            bias = bias_ref[t, h, pl.ds(ki, 1), :]
            m_old = m_ref[t, h]
            m_new = jnp.maximum(m_old, jnp.max(s, axis=0, keepdims=True) + bias)
            alpha = jnp.exp2(m_old - m_new)
            p = jnp.exp2(s - (m_new - bias)).astype(BF16)
            m_ref[t, h] = m_new
            acc_ref[t, h] = alpha * acc_ref[t, h] + _dot(values_ext(ki, h), p)

    def bounded_own(t, ki, scores):
        for h in range(GROUP_HEADS):
            s = jnp.where(causal, scores[h], MASKED)
            p = jnp.exp2(s - m_ref[t, h]).astype(BF16)
            acc_ref[t, h] = _dot(values_ext(ki, h), p)

    def bounded_past(t, ki, scores):
        for h in range(GROUP_HEADS):
            bias = bias_ref[t, h, pl.ds(ki, 1), :]
            p = jnp.exp2(scores[h] - (m_ref[t, h] - bias)).astype(BF16)
            acc_ref[t, h] = acc_ref[t, h] + _dot(values_ext(ki, h), p)

    def attend(own, past):
        work = [(own, 0, j), (own, 1, nb - 1 - j)]
        for n in range(nb - 1):
            second = (n >= j).astype(jnp.int32)
            work.append((past, second, n - second * j))
        pending = None
        for fin, t, ki in work:
            scores = issue_scores(t, ki)
            if pending is not None:
                pending[0](*pending[1:])
            pending = (fin, t, ki, scores)
        pending[0](*pending[1:])

    bound_ok = jnp.max(jnp.concatenate(bounds, axis=0)) < SCORE_BOUND_SAFE

    @pl.when(bound_ok)
    def _():
        for i, bound in enumerate(bounds):
            m_ref[i // GROUP_HEADS, i % GROUP_HEADS] = bound
        attend(bounded_own, bounded_past)

    @pl.when(jnp.logical_not(bound_ok))
    def _():
        attend(finish_own, finish_past)

    for t, (_, gat_ref, out_ref, _) in enumerate(tiles):
        for h in range(GROUP_HEADS):
            rows = slice(h * HEAD_DIM, (h + 1) * HEAD_DIM)
            acc = acc_ref[t, h]
            y = acc[0:HEAD_DIM] / acc[HEAD_DIM:HEAD_DIM + 1]
            out_ref[0, 0, rows, :] = (y * gat_ref[0, 0, rows, :].astype(F32)).astype(BF16)


def _moba(qt, k, kmean, kn2, vt, gat, batch, seq):
    nb = seq // MOBA_BLOCK
    assert nb % 2 == 0
    blk = MOBA_BLOCK
    groups = ATTN_WIDTH // GROUP_WIDTH
    k3 = k.reshape(batch, seq, ATTN_WIDTH)
    km3 = kmean.reshape(batch, nb, ATTN_WIDTH)
    kn3 = kn2.reshape(batch, nb, 128)
    tile_a = pl.BlockSpec((1, 1, GROUP_WIDTH, blk), lambda b, g, j: (b, j, g, 0))
    tile_b = pl.BlockSpec((1, 1, GROUP_WIDTH, blk), lambda b, g, j: (b, nb - 1 - j, g, 0))
    out_shape = jax.ShapeDtypeStruct((batch, nb // 2, ATTN_WIDTH, blk), BF16)
    return pl.pallas_call(
        functools.partial(_moba_kernel, nb=nb),
        grid=(batch, groups, nb // 2),
        in_specs=[
            tile_a,
            tile_b,
            pl.BlockSpec((1, seq, GROUP_WIDTH), lambda b, g, j: (b, 0, g)),
            pl.BlockSpec((1, nb, GROUP_WIDTH), lambda b, g, j: (b, 0, g)),
            pl.BlockSpec((1, nb, 128), lambda b, g, j: (b, 0, 0)),
            pl.BlockSpec((1, nb, GROUP_WIDTH, blk), lambda b, g, j: (b, 0, g, 0)),
            tile_a,
            tile_b,
        ],
        out_specs=[tile_a, tile_a],
        out_shape=[out_shape, out_shape],
        scratch_shapes=[
            pltpu.VMEM((2, GROUP_HEADS, GROUP_WIDTH, blk), BF16),
            pltpu.VMEM((2, GROUP_HEADS, nb, blk), F32),
            pltpu.VMEM((2, GROUP_HEADS, 1, blk), F32),
            pltpu.VMEM((2, GROUP_HEADS, ACC_ROWS, blk), F32),
        ],
        compiler_params=pltpu.CompilerParams(
            dimension_semantics=("arbitrary", "arbitrary", "arbitrary"), vmem_limit_bytes=VMEM_LIMIT_BYTES),
        name="moba",
    )(qt, qt, k3, km3, kn3, vt, gat, gat)


def _block_diag(x_bf16, mask):
    tiled = jnp.concatenate([x_bf16] * GROUP_HEADS, axis=0)
    return jnp.where(mask, tiled, jnp.zeros_like(tiled))


def _rwkv_kernel(rw_ref, edec_ref, gb_ref, gng_ref, gnb_ref, yat1_ref, yat2_ref, x_ref, woa_ref, wob_ref,
                 gain_ref, out_ref, state_ref, y_ref, *, nb):
    tile = rw_ref.shape[0]
    c_w = RWKV_WIDTH

    @pl.when(pl.program_id(1) == 0)
    def _():
        state_ref[...] = jnp.zeros_like(state_ref)

    gr = lax.broadcasted_iota(jnp.int32, (GROUP_WIDTH, GROUP_WIDTH), 0) // HEAD_DIM
    gc = lax.broadcasted_iota(jnp.int32, (GROUP_WIDTH, GROUP_WIDTH), 1) // HEAD_DIM
    diag_mask = gr == gc
    t_idx = lax.broadcasted_iota(jnp.int32, (CHUNK, GROUP_WIDTH), 0)
    j_idx = lax.broadcasted_iota(jnp.int32, (CHUNK, GROUP_WIDTH), 1) % HEAD_DIM
    strict_lower = j_idx < t_idx
    lower = j_idx <= t_idx
    eye = (j_idx == t_idx).astype(F32)
    lane_head = lax.broadcasted_iota(jnp.int32, (1, GROUP_WIDTH), 1) // HEAD_DIM

    def bd(x):
        return _block_diag(x.astype(BF16), diag_mask)

    def diag_blocks(full):
        out = full[(GROUP_HEADS - 1) * HEAD_DIM:, :]
        for h in range(GROUP_HEADS - 2, -1, -1):
            out = jnp.where(lane_head == h, full[h * HEAD_DIM:(h + 1) * HEAD_DIM, :], out)
        return out

    n_groups = c_w // GROUP_WIDTH
    items = []
    for c in range(tile // CHUNK):
        rs = slice(c * CHUNK, (c + 1) * CHUNK)
        end_decay = edec_ref[c]
        for gi in range(n_groups):
            ls = slice(gi * GROUP_WIDTH, (gi + 1) * GROUP_WIDTH)
            operand = lambda n: rw_ref[rs, n * c_w + gi * GROUP_WIDTH:n * c_w + (gi + 1) * GROUP_WIDTH]
            it = dict(rs=rs, ls=ls, gi=gi)
            it["a_t"] = operand(0)
            it["r_t"] = operand(1)
            it["bk_e"] = jnp.concatenate([operand(4), operand(5)], axis=0)
            it["v_c"] = operand(6)
            it["m_diag"] = eye * end_decay[:, ls]
            ar = jnp.concatenate([it["a_t"], it["r_t"]], axis=0)
            pb = _dot_nt(ar, _block_diag(operand(2), diag_mask))
            pk = _dot_nt(ar, _block_diag(operand(3), diag_mask))
            it["pw"] = jnp.where(strict_lower, pb[0:CHUNK], 0.0)
            it["t_inv"] = eye + it["pw"]
            it["a_rb"] = jnp.where(lower, pb[CHUNK:], 0.0).astype(BF16)
            it["a_k"] = jnp.concatenate([jnp.where(strict_lower, pk[0:CHUNK], 0.0),
                                         jnp.where(lower, pk[CHUNK:], 0.0)], axis=0).astype(BF16)
            items.append(it)

    steps = CHUNK.bit_length() - 1
    for s in range(1, steps + 1):
        for it in items:
            w_p = bd(it["pw"])
            if s == 1:
                it["pw"] = _dot(it["pw"].astype(BF16), w_p)
            elif s < steps:
                both = _dot(jnp.concatenate([it["pw"], it["t_inv"]], axis=0).astype(BF16), w_p)
                it["t_inv"] = it["t_inv"] + both[CHUNK:]
                it["pw"] = both[0:CHUNK]
            else:
                it["t_inv"] = (it["t_inv"] + _dot(it["t_inv"].astype(BF16), w_p)).astype(BF16)

    for it in items:
        kv = _dot(it["a_k"], bd(it["v_c"]))
        it["akv"] = kv[0:CHUNK]
        it["arkv"] = kv[CHUNK:]
    for it in items:
        it["w_t"] = _dot(it["t_inv"], bd(it["a_t"])).astype(BF16)
        it["u_t"] = _dot(it["t_inv"], bd(it["akv"])).astype(BF16)
    for it in items:
        it["r_hat"] = (it["r_t"].astype(F32) + _dot(it["a_rb"], bd(it["w_t"]))).astype(BF16)
        it["y_hat"] = _dot(it["a_rb"], bd(it["u_t"])) + it["arkv"]
    for it in items:
        m_full = _dot_tn(it["bk_e"][0:CHUNK], it["w_t"])
        it["m"] = (it["m_diag"] + diag_blocks(m_full)).astype(BF16)
        n_full = _dot_tn(it["bk_e"], jnp.concatenate([it["u_t"], it["v_c"]], axis=0))
        it["n"] = diag_blocks(n_full)

    in_first = pl.program_id(1) < nb // 2
    per = yat1_ref.shape[1]
    yat_t = jnp.concatenate([jnp.transpose(jnp.where(in_first, yat1_ref[0, s], yat2_ref[0, per - 1 - s]))
                             for s in range(per)], axis=0)
    n_chunks = tile // CHUNK
    n_pieces = woa_ref.shape[1] // GROUP_WIDTH
    attn_out = []
    states = [state_ref[gi] for gi in range(n_groups)]
    for it in items:
        gi = it["gi"]
        res = _dot(jnp.concatenate([it["m"], it["r_hat"]], axis=0), bd(states[gi]))
        y_ref[it["rs"], it["ls"]] = res[CHUNK:] + it["y_hat"]
        states[gi] = res[0:CHUNK] + it["n"]
        done = it["rs"].stop // CHUNK
        if gi == n_groups - 1 and done * n_pieces % n_chunks == 0:
            c0 = len(attn_out) * GROUP_WIDTH
            attn_out.append(_dot(yat_t, woa_ref[:, c0:c0 + GROUP_WIDTH]))
    for gi in range(n_groups):
        state_ref[gi] = states[gi]

    y = y_ref[...]
    mean = _head_sum(y) * (1.0 / HEAD_DIM)
    d = y - mean
    var = _head_sum(d * d) * (1.0 / HEAD_DIM)
    yn = d * lax.rsqrt(var + GN_EPS) * gng_ref[...] + gnb_ref[...]
    bonus = rw_ref[:, (RW_OPERANDS - 1) * c_w:].astype(F32)
    yb = ((yn + bonus) * gb_ref[...].astype(F32)).astype(BF16)

    h = x_ref[...] + jnp.concatenate(attn_out, axis=-1) + _dot(yb, wob_ref[...])
    ms = jnp.mean(h * h, axis=-1, keepdims=True)
    out_ref[...] = (h * lax.rsqrt(ms + RMS_EPS) * gain_ref[...]).astype(out_ref.dtype)


def _rwkv(rw, edec, gb, gn_gain, gn_bias, yat_first, yat_second, x2d, woa, wob, out_gain, batch, seq):
    tile, per = ROW_TILE, BLOCKS_PER_TILE
    nt = seq // tile
    assert nt % 2 == 0
    half = nt // 2
    d_model = x2d.shape[1]
    row_spec = lambda width: pl.BlockSpec((tile, width), lambda b, i: (b * nt + i, 0))
    const = lambda shape: pl.BlockSpec(shape, lambda b, i: (0,) * len(shape))
    vec = lambda width: const((1, width))
    blk_t = lambda pick: pl.BlockSpec((1, per, ATTN_WIDTH, MOBA_BLOCK),
                                      lambda b, i: (b, jnp.minimum(pick(i), half - 1), 0, 0))
    return pl.pallas_call(
        functools.partial(_rwkv_kernel, nb=nt),
        grid=(batch, nt),
        in_specs=[
            row_spec(RW_OPERANDS * RWKV_WIDTH),
            pl.BlockSpec((tile // CHUNK, 1, RWKV_WIDTH), lambda b, i: (b * nt + i, 0, 0)),
            row_spec(RWKV_WIDTH),
            vec(RWKV_WIDTH),
            vec(RWKV_WIDTH),
            blk_t(lambda i: i),
            blk_t(lambda i: nt - 1 - i),
            row_spec(d_model),
            const((ATTN_WIDTH, d_model)),
            const((RWKV_WIDTH, d_model)),
            vec(d_model),
        ],
        out_specs=row_spec(d_model),
        out_shape=jax.ShapeDtypeStruct(x2d.shape, x2d.dtype),
        scratch_shapes=[
            pltpu.VMEM((RWKV_WIDTH // GROUP_WIDTH, HEAD_DIM, GROUP_WIDTH), F32),
            pltpu.VMEM((tile, RWKV_WIDTH), F32),
        ],
        compiler_params=pltpu.CompilerParams(
            dimension_semantics=("arbitrary", "arbitrary"), vmem_limit_bytes=VMEM_LIMIT_BYTES),
        name="rwkv_out",
    )(rw, edec, gb, gn_gain, gn_bias, yat_first, yat_second, x2d, woa, wob, out_gain)


def _rotary_column_order():
    order = []
    for g in range(ATTN_WIDTH // GROUP_WIDTH):
        for part in range(2):
            for hh in range(GROUP_HEADS):
                head = g * GROUP_HEADS + hh
                order.extend(head * HEAD_DIM + part * HALF + j for j in range(HALF))
    return np.asarray(order, np.int32)


def _layer(h2d, batch, seq, norm_gain, w_in, shift_mu, decay_bias, decay_up, iclr_bias, iclr_up,
           k_k, k_a, r_k, gn_gain, gn_bias, w_out, out_gain, cos, sin):
    nb = seq // MOBA_BLOCK
    aw, rw = ATTN_WIDTH, RWKV_WIDTH
    order = _rotary_column_order()
    b0 = 4 * aw
    w_in = w_in.astype(BF16)
    wnn = jnp.concatenate([w_in[:, aw:2 * aw][:, order], w_in[:, b0:]], axis=1)
    wnt = jnp.concatenate([w_in[:, 0:aw][:, order], w_in[:, 2 * aw:4 * aw]], axis=1).T
    zeros = jnp.zeros((LORA_RANK, rw), F32)
    wl = jnp.concatenate([jnp.concatenate([decay_up.astype(F32), zeros], axis=1),
                          jnp.concatenate([zeros, iclr_up.astype(F32)], axis=1)], axis=0)
    wl_hi = wl.astype(BF16)
    wl_lo = (wl - wl_hi.astype(F32)).astype(BF16)
    wlora = jnp.stack([wl_hi, wl_lo])
    row = lambda t: t.astype(F32).reshape(1, -1)

    cos_t = cos.reshape(nb, MOBA_BLOCK, 128).transpose(0, 2, 1)
    sin_t = sin.reshape(nb, MOBA_BLOCK, 128).transpose(0, 2, 1)
    lane = np.arange(aw)
    head_of_lane = (lane // GROUP_WIDTH) * GROUP_HEADS + (lane % 128) // HALF
    hsel = jnp.asarray(head_of_lane[:, None] == np.arange(128)[None, :], BF16)
    rw_vecs = [row(decay_bias), row(iclr_bias), row(k_k), row(k_a), row(r_k)]
    qt, k, kmean, kn2, vt, gat, rw, edec, gb = _in_proj(h2d, row(norm_gain), wnn, wnt, cos, sin, cos_t, sin_t,
                                                        row(shift_mu), hsel, wlora, rw_vecs, batch, nb)
    yat_first, yat_second = _moba(qt, k, kmean, kn2, vt, gat, batch, seq)
    w_out = w_out.astype(BF16)
    return _rwkv(rw, edec, gb, row(gn_gain), row(gn_bias), yat_first, yat_second, h2d, w_out[0:aw], w_out[aw:],
                 out_gain, batch, seq)


def kernel(x, norm_gain, w_in, shift_mu, decay_bias, decay_up, iclr_bias, iclr_up,
           k_k, k_a, r_k, gn_gain, gn_bias, w_out, final_gain):
    batch, seq, d_model = x.shape
    depth = norm_gain.shape[0]
    assert depth == 1, "the final RMSNorm is fused into the single layer's output projection"
    assert seq % MOBA_BLOCK == 0 and w_in.shape[-1] == 4 * ATTN_WIDTH + SHIFT_WIDTH + RWKV_WIDTH
    inv_freq = 1.0 / (ROPE_THETA ** (jnp.arange(0, HEAD_DIM, 2, dtype=F32) / HEAD_DIM))
    ang = jnp.arange(seq, dtype=F32)[:, None] * inv_freq[None, :]
    cos = jnp.tile(jnp.cos(ang), (1, GROUP_HEADS))
    sin = jnp.tile(jnp.sin(ang), (1, GROUP_HEADS))
    out = _layer(x.reshape(batch * seq, d_model).astype(F32), batch, seq, norm_gain[0], w_in[0],
                 shift_mu[0], decay_bias[0], decay_up[0], iclr_bias[0], iclr_up[0], k_k[0], k_a[0], r_k[0],
                 gn_gain[0], gn_bias[0], w_out[0], final_gain.astype(F32).reshape(1, -1), cos, sin)
    return out.reshape(batch, seq, d_model).astype(x.dtype)
```

```python
import functools

import jax
import jax.numpy as jnp
import numpy as np
from jax import lax
from jax.experimental import pallas as pl
from jax.experimental.pallas import tpu as pltpu

HEAD_DIM = 64
HALF = HEAD_DIM // 2
ATTN_HEADS = 8
ATTN_WIDTH = ATTN_HEADS * HEAD_DIM
RWKV_HEADS = 8
RWKV_WIDTH = RWKV_HEADS * HEAD_DIM
LORA_RANK = 64
SHIFT_WIDTH = 3 * RWKV_WIDTH + 2 * LORA_RANK
MOBA_BLOCK = 256
MOBA_TOPK = 3
ROPE_THETA = 10000.0
RMS_EPS = 1e-6
GN_EPS = 64e-5
NORMALIZE_EPS_SQ = 1e-24

GROUP_HEADS = 4
GROUP_WIDTH = GROUP_HEADS * HEAD_DIM
CHUNK = 64
ROW_TILE = 2 * MOBA_BLOCK
BLOCKS_PER_TILE = ROW_TILE // MOBA_BLOCK
RW_OPERANDS = 8
MASKED = -1e30
DECAY_SCALE = -float(np.exp(-0.5))
SCORE_BOUND_MARGIN = 1.0 + 2.0 ** -6
SCORE_BOUND_SAFE = 56.0
ACC_ROWS = HEAD_DIM + 16
LOG2E = 1.4426950408889634
Q_SCALE = HEAD_DIM ** -0.5 * LOG2E
VMEM_LIMIT_BYTES = 48 * 1024 * 1024

F32 = jnp.float32
BF16 = jnp.bfloat16


def _dot(a, b):
    return jnp.dot(a, b, preferred_element_type=F32)


def _dot_nt(a, b):
    return lax.dot_general(a, b, (((1,), (1,)), ((), ())), preferred_element_type=F32)


def _dot_tn(a, b):
    return lax.dot_general(a, b, (((0,), (0,)), ((), ())), preferred_element_type=F32)


def _split_bf16(x):
    hi = x.astype(BF16)
    lo = (x - hi.astype(F32)).astype(BF16)
    return hi, lo


def _silu(z):
    return z / (1.0 + jnp.exp(-z))


def _rotary(acc, cos, sin):
    outs = []
    for g in range(ATTN_WIDTH // GROUP_WIDTH):
        x1 = acc[:, g * GROUP_WIDTH:g * GROUP_WIDTH + 128]
        x2 = acc[:, g * GROUP_WIDTH + 128:(g + 1) * GROUP_WIDTH]
        outs.append(x1 * cos - x2 * sin)
        outs.append(x2 * cos + x1 * sin)
    return jnp.concatenate(outs, axis=-1)


def _rotary_t(acc, cos_t, sin_t):
    outs = []
    for g in range(ATTN_WIDTH // GROUP_WIDTH):
        x1 = acc[g * GROUP_WIDTH:g * GROUP_WIDTH + 128, :]
        x2 = acc[g * GROUP_WIDTH + 128:(g + 1) * GROUP_WIDTH, :]
        outs.append(x1 * cos_t - x2 * sin_t)
        outs.append(x2 * cos_t + x1 * sin_t)
    return jnp.concatenate(outs, axis=0)


def _head_sum(t):
    gr = lax.broadcasted_iota(jnp.int32, (GROUP_WIDTH, GROUP_WIDTH), 0) // HEAD_DIM
    gc = lax.broadcasted_iota(jnp.int32, (GROUP_WIDTH, GROUP_WIDTH), 1) // HEAD_DIM
    head_ones = jnp.where(gr == gc, 1.0, 0.0).astype(BF16)
    t = t.astype(BF16)
    return jnp.concatenate([_dot(t[:, i:i + GROUP_WIDTH], head_ones)
                            for i in range(0, t.shape[1], GROUP_WIDTH)], axis=-1)


def _in_proj_block(s, x_ref, gain_ref, wnn_ref, wnt_ref, cos_ref, sin_ref, cost_ref, sint_ref, mu_ref, hsel_ref,
                   wlora_ref, dbias_ref, ibias_ref, kk_ref, ka_ref, rk_ref,
                   qt_ref, k_ref, kmean_ref, kn2_ref, vt_ref, gat_ref, rw_ref, edec_ref, gb_ref, prev_ref):
    tile = MOBA_BLOCK
    rows = slice(s * tile, (s + 1) * tile)
    x = x_ref[rows, :]
    ms = jnp.mean(x * x, axis=-1, keepdims=True)
    u = (x * lax.rsqrt(ms + RMS_EPS) * gain_ref[...]).astype(BF16)

    k_att = _rotary(_dot(u, wnn_ref[:, 0:ATTN_WIDTH]), cos_ref[rows, :], sin_ref[rows, :])
    k_bf = k_att.astype(BF16)
    for g in range(ATTN_WIDTH // GROUP_WIDTH):
        k_ref[g, rows, :] = k_bf[:, g * GROUP_WIDTH:(g + 1) * GROUP_WIDTH]
    kmean_ref[s] = jnp.mean(k_att, axis=0, keepdims=True)
    k_sq = k_bf.astype(F32)
    k_sq = (k_sq * k_sq).astype(BF16)

    base = ATTN_WIDTH
    c_w = RWKV_WIDTH
    row = lax.broadcasted_iota(jnp.int32, (tile, 1), 0)

    def shifted_cols(c, w):
        p = _dot(u, wnn_ref[:, base + c:base + c + w])
        shifted = jnp.where(row == 0, prev_ref[0:1, c:c + w], pltpu.roll(p, 1, axis=0))
        prev_ref[0:1, c:c + w] = p[tile - 1:tile, :]
        return p + (shifted - p) * mu_ref[:, c:c + w]

    lora_in = shifted_cols(3 * c_w, 2 * LORA_RANK)
    lane128 = lax.broadcasted_iota(jnp.int32, (1, 2 * LORA_RANK), 1)
    lora_in = jnp.where(lane128 < LORA_RANK, jnp.tanh(lora_in), lora_in)
    li_hi, li_lo = _split_bf16(lora_in)
    k = shifted_cols(c_w, c_w)
    kk = k * kk_ref[...]
    r = shifted_cols(0, c_w)
    v = shifted_cols(2 * c_w, c_w)

    kn2_ref[s] = jnp.max(_dot(k_sq, hsel_ref[...]), axis=0, keepdims=True)
    w_hi = wlora_ref[...]
    lora = _dot(li_hi, w_hi) + _dot(li_lo, w_hi)
    kk = kk * lax.rsqrt(jnp.maximum(_head_sum(kk * kk), NORMALIZE_EPS_SQ))

    gb = _dot(u, wnn_ref[:, base + SHIFT_WIDTH:base + SHIFT_WIDTH + RWKV_WIDTH])
    gb_ref[rows, :] = _silu(gb).astype(BF16)

    g = DECAY_SCALE / (1.0 + jnp.exp(-(dbias_ref[...] + lora[:, 0:c_w])))
    a = 1.0 / (1.0 + jnp.exp(-(ibias_ref[...] + lora[:, c_w:2 * c_w])))
    g_hi, g_lo = _split_bf16(g)
    k2 = k * (1.0 + (a - 1.0) * ka_ref[...])
    b = kk * a

    qt = _rotary_t(_dot_nt(wnt_ref[0:ATTN_WIDTH, :], u), cost_ref[s], sint_ref[s])
    qt_ref[0, s] = (qt * Q_SCALE).astype(BF16)

    ti = lax.broadcasted_iota(jnp.int32, (CHUNK, CHUNK), 0)
    tj = lax.broadcasted_iota(jnp.int32, (CHUNK, CHUNK), 1)
    tri_incl = jnp.where(tj <= ti, 1.0, 0.0).astype(BF16)
    cums = [_dot(tri_incl, g_hi[c:c + CHUNK]) + _dot(tri_incl, g_lo[c:c + CHUNK])
            for c in range(0, tile, CHUNK)]
    bonus = _head_sum(r * k2 * rk_ref[...]) * v

    for i, cum in enumerate(cums):
        rs = slice(i * CHUNK, (i + 1) * CHUNK)
        dec_in = jnp.exp(cum)
        dec_ex = jnp.exp(cum - g[rs])
        inv = 1.0 / dec_in
        end_decay = dec_in[CHUNK - 1:CHUNK, :]
        to_end = inv * end_decay
        edec_ref[s * (tile // CHUNK) + i] = end_decay
        cols = (-kk[rs] * dec_ex, r[rs] * dec_in, b[rs] * inv, k2[rs] * inv,
                b[rs] * to_end, k2[rs] * to_end, v[rs], bonus[rs])
        for n, col in enumerate(cols):
            rw_ref[s * tile + i * CHUNK:s * tile + (i + 1) * CHUNK, n * c_w:(n + 1) * c_w] = col.astype(BF16)

    vt_ref[0, s] = _dot_nt(wnt_ref[ATTN_WIDTH:2 * ATTN_WIDTH, :], u).astype(BF16)
    gat = _dot_nt(wnt_ref[2 * ATTN_WIDTH:3 * ATTN_WIDTH, :], u)
    gat_ref[0, s] = _silu(gat).astype(BF16)


def _in_proj_kernel(x_ref, gain_ref, wnn_ref, wnt_ref, cos_ref, sin_ref, cost_ref, sint_ref, mu_ref, hsel_ref,
                    wlora_ref, dbias_ref, ibias_ref, kk_ref, ka_ref, rk_ref,
                    qt_ref, k_ref, kmean_ref, kn2_ref, vt_ref, gat_ref, rw_ref, edec_ref, gb_ref, prev_ref,
                    *, nb):
    @pl.when(pl.program_id(0) % (nb // BLOCKS_PER_TILE) == 0)
    def _():
        prev_ref[...] = jnp.zeros_like(prev_ref)

    for s in range(BLOCKS_PER_TILE):
        _in_proj_block(s, x_ref, gain_ref, wnn_ref, wnt_ref, cos_ref, sin_ref, cost_ref, sint_ref, mu_ref,
                       hsel_ref, wlora_ref, dbias_ref, ibias_ref, kk_ref, ka_ref, rk_ref,
                       qt_ref, k_ref, kmean_ref, kn2_ref, vt_ref, gat_ref, rw_ref, edec_ref, gb_ref, prev_ref)


def _in_proj(x2d, gain, wnn, wnt, cos, sin, cos_t, sin_t, mu, hsel, wlora, rw_vecs, batch, nb):
    rows, d_model = x2d.shape
    tile, per = ROW_TILE, BLOCKS_PER_TILE
    assert nb % per == 0
    nt = nb // per
    row_spec = lambda width: pl.BlockSpec((tile, width), lambda i: (i, 0))
    const = lambda shape: pl.BlockSpec(shape, lambda i: (0,) * len(shape))
    blk_t = pl.BlockSpec((1, per, ATTN_WIDTH, MOBA_BLOCK), lambda i: (i // nt, i % nt, 0, 0))
    blk_t_shape = jax.ShapeDtypeStruct((batch, nb, ATTN_WIDTH, MOBA_BLOCK), BF16)
    return pl.pallas_call(
        functools.partial(_in_proj_kernel, nb=nb),
        grid=(rows // tile,),
        in_specs=[
            row_spec(d_model),
            const((1, d_model)),
            const(wnn.shape),
            const(wnt.shape),
            pl.BlockSpec((tile, 128), lambda i: (i % nt, 0)),
            pl.BlockSpec((tile, 128), lambda i: (i % nt, 0)),
            pl.BlockSpec((per, 128, MOBA_BLOCK), lambda i: (i % nt, 0, 0)),
            pl.BlockSpec((per, 128, MOBA_BLOCK), lambda i: (i % nt, 0, 0)),
            const((1, SHIFT_WIDTH)),
            const(hsel.shape),
            const(wlora.shape),
        ] + [const((1, RWKV_WIDTH))] * len(rw_vecs),
        out_specs=[
            blk_t,
            pl.BlockSpec((ATTN_WIDTH // GROUP_WIDTH, tile, GROUP_WIDTH), lambda i: (0, i, 0)),
            pl.BlockSpec((per, 1, ATTN_WIDTH), lambda i: (i, 0, 0)),
            pl.BlockSpec((per, 1, 128), lambda i: (i, 0, 0)),
            blk_t,
            blk_t,
            row_spec(RW_OPERANDS * RWKV_WIDTH),
            pl.BlockSpec((tile // CHUNK, 1, RWKV_WIDTH), lambda i: (i, 0, 0)),
            row_spec(RWKV_WIDTH),
        ],
        out_shape=[
            blk_t_shape,
            jax.ShapeDtypeStruct((ATTN_WIDTH // GROUP_WIDTH, rows, GROUP_WIDTH), BF16),
            jax.ShapeDtypeStruct((rows // MOBA_BLOCK, 1, ATTN_WIDTH), F32),
            jax.ShapeDtypeStruct((rows // MOBA_BLOCK, 1, 128), F32),
            blk_t_shape,
            blk_t_shape,
            jax.ShapeDtypeStruct((rows, RW_OPERANDS * RWKV_WIDTH), BF16),
            jax.ShapeDtypeStruct((rows // CHUNK, 1, RWKV_WIDTH), F32),
            jax.ShapeDtypeStruct((rows, RWKV_WIDTH), BF16),
        ],
        scratch_shapes=[pltpu.VMEM((8, SHIFT_WIDTH), F32)],
        compiler_params=pltpu.CompilerParams(
            dimension_semantics=("arbitrary",), vmem_limit_bytes=VMEM_LIMIT_BYTES),
        name="in_proj",
    )(x2d, gain, wnn, wnt, cos, sin, cos_t, sin_t, mu, hsel, wlora, *rw_vecs)


def _moba_kernel(qta_ref, qtb_ref, k_ref, kmean_ref, kn2_ref, vt_ref, gata_ref, gatb_ref, outa_ref, outb_ref,
                 qh_ref, bias_ref, m_ref, acc_ref, *, nb):
    j = pl.program_id(2)
    blk = MOBA_BLOCK
    tiles = ((qta_ref, gata_ref, outa_ref, j), (qtb_ref, gatb_ref, outb_ref, nb - 1 - j))
    feat = lax.broadcasted_iota(jnp.int32, (GROUP_WIDTH, 1), 0)
    head_of_feat = (feat % 128) // HALF
    ones_rows = jnp.ones((ACC_ROWS - HEAD_DIM, blk), BF16)
    km = kmean_ref[0]
    km_lane_head = (lax.broadcasted_iota(jnp.int32, (1, GROUP_WIDTH), 1) % 128) // HALF
    km_heads = jnp.concatenate([jnp.where(km_lane_head == h, km, 0.0) for h in range(GROUP_HEADS)], axis=0)
    km_hi, km_lo = _split_bf16(km_heads)
    blk_id = lax.broadcasted_iota(jnp.int32, (GROUP_HEADS, nb, blk), 1)
    k_pos = lax.broadcasted_iota(jnp.int32, (blk, blk), 0)
    q_pos = lax.broadcasted_iota(jnp.int32, (blk, blk), 1)
    causal = k_pos <= q_pos
    kn2 = kn2_ref[0]
    kn2_row = lax.broadcasted_iota(jnp.int32, kn2.shape, 0)
    kn2_lane = lax.broadcasted_iota(jnp.int32, (1, kn2.shape[1]), 1)
    first_head = pl.program_id(1) * GROUP_HEADS

    def values_ext(ki, h):
        return jnp.concatenate([vt_ref[0, ki, h * HEAD_DIM:(h + 1) * HEAD_DIM, :], ones_rows], axis=0)

    bounds = []
    for t, (qt_ref, _, _, qi) in enumerate(tiles):
        qt = qt_ref[0, 0]
        q_sq = qt.astype(F32)
        q_sq = q_sq * q_sq
        kn2_seen = jnp.max(jnp.where(kn2_row <= qi, kn2, 0.0), axis=0, keepdims=True)
        for h in range(GROUP_HEADS):
            qn2 = (jnp.sum(q_sq[h * HALF:(h + 1) * HALF], axis=0, keepdims=True)
                   + jnp.sum(q_sq[128 + h * HALF:128 + (h + 1) * HALF], axis=0, keepdims=True))
            kn2_h = jnp.sum(jnp.where(kn2_lane == first_head + h, kn2_seen, 0.0), axis=1, keepdims=True)
            bounds.append(jnp.sqrt(kn2_h * qn2) * SCORE_BOUND_MARGIN)
            qh_ref[t, h] = jnp.where(head_of_feat == h, qt, jnp.zeros_like(qt))
        gate = (_dot(km_hi, qt) + _dot(km_lo, qt)).reshape(GROUP_HEADS, nb, blk)
        gate = jnp.where(blk_id < qi, gate, -jnp.inf)
        bias = jnp.full(gate.shape, MASKED, F32)
        for r in range(min(MOBA_TOPK, nb)):
            top = jnp.max(gate, axis=1, keepdims=True)
            idx = jnp.min(jnp.where(gate == top, blk_id, nb), axis=1, keepdims=True)
            pick = blk_id == idx + jnp.where(r < qi, 0, nb)
            bias = jnp.where(pick, 0.0, bias)
            gate = jnp.where(pick, -jnp.inf, gate)
        bias_ref[t] = bias

    def issue_scores(t, ki):
        kb = k_ref[0, 0, pl.ds(pl.multiple_of(ki * blk, blk), blk), :]
        return [_dot(kb, qh_ref[t, h]) for h in range(GROUP_HEADS)]

    def finish_own(t, ki, scores):
        for h in range(GROUP_HEADS):
            s = jnp.where(causal, scores[h], MASKED)
            m = jnp.max(s, axis=0, keepdims=True)
            p = jnp.exp2(s - m).astype(BF16)
            m_ref[t, h] = m
            acc_ref[t, h] = _dot(values_ext(ki, h), p)

    def finish_past(t, ki, scores):
        for h in range(GROUP_HEADS):
            s = scores[h]
            bias = bias_ref[t, h, pl.ds(ki, 1), :]
            m_old = m_ref[t, h]
            m_new = jnp.maximum(m_old, jnp.max(s, axis=0, keepdims=True) + bias)
            alpha = jnp.exp2(m_old - m_new)
            p = jnp.exp2(s - (m_new - bias)).astype(BF16)
            m_ref[t, h] = m_new
            acc_ref[t, h] = alpha * acc_ref[t, h] + _dot(values_ext(ki, h), p)

    def bounded_own(t, ki, scores):
        for h in range(GROUP_HEADS):
            s = jnp.where(causal, scores[h], MASKED)
            p = jnp.exp2(s - m_ref[t, h]).astype(BF16)
            acc_ref[t, h] = _dot(values_ext(ki, h), p)

    def bounded_past(t, ki, scores):
        for h in range(GROUP_HEADS):
            bias = bias_ref[t, h, pl.ds(ki, 1), :]
            p = jnp.exp2(scores[h] - (m_ref[t, h] - bias)).astype(BF16)
            acc_ref[t, h] = acc_ref[t, h] + _dot(values_ext(ki, h), p)

    def attend(own, past):
        work = [(own, 0, j), (own, 1, nb - 1 - j)]
        for n in range(nb - 1):
            second = (n >= j).astype(jnp.int32)
            work.append((past, second, n - second * j))
        pending = None
        for fin, t, ki in work:
            scores = issue_scores(t, ki)
            if pending is not None:
                pending[0](*pending[1:])
            pending = (fin, t, ki, scores)
        pending[0](*pending[1:])

    bound_ok = jnp.max(jnp.concatenate(bounds, axis=0)) < SCORE_BOUND_SAFE

    @pl.when(bound_ok)
    def _():
        for i, bound in enumerate(bounds):
            m_ref[i // GROUP_HEADS, i % GROUP_HEADS] = bound
        attend(bounded_own, bounded_past)

    @pl.when(jnp.logical_not(bound_ok))
    def _():
        attend(finish_own, finish_past)

    for t, (_, gat_ref, out_ref, _) in enumerate(tiles):
        for h in range(GROUP_HEADS):
            rows = slice(h * HEAD_DIM, (h + 1) * HEAD_DIM)
            acc = acc_ref[t, h]
            y = acc[0:HEAD_DIM] / acc[HEAD_DIM:HEAD_DIM + 1]
            out_ref[0, 0, rows, :] = (y * gat_ref[0, 0, rows, :].astype(F32)).astype(BF16)


def _moba(qt, k, kmean, kn2, vt, gat, batch, seq):
    nb = seq // MOBA_BLOCK
    assert nb % 2 == 0
    blk = MOBA_BLOCK
    groups = ATTN_WIDTH // GROUP_WIDTH
    k4 = k.reshape(groups, batch, seq, GROUP_WIDTH)
    km3 = kmean.reshape(batch, nb, ATTN_WIDTH)
    kn3 = kn2.reshape(batch, nb, 128)
    tile_a = pl.BlockSpec((1, 1, GROUP_WIDTH, blk), lambda b, g, j: (b, j, g, 0))
    tile_b = pl.BlockSpec((1, 1, GROUP_WIDTH, blk), lambda b, g, j: (b, nb - 1 - j, g, 0))
    out_shape = jax.ShapeDtypeStruct((batch, nb // 2, ATTN_WIDTH, blk), BF16)
    return pl.pallas_call(
        functools.partial(_moba_kernel, nb=nb),
        grid=(batch, groups, nb // 2),
        in_specs=[
            tile_a,
            tile_b,
            pl.BlockSpec((1, 1, seq, GROUP_WIDTH), lambda b, g, j: (g, b, 0, 0)),
            pl.BlockSpec((1, nb, GROUP_WIDTH), lambda b, g, j: (b, 0, g)),
            pl.BlockSpec((1, nb, 128), lambda b, g, j: (b, 0, 0)),
            pl.BlockSpec((1, nb, GROUP_WIDTH, blk), lambda b, g, j: (b, 0, g, 0)),
            tile_a,
            tile_b,
        ],
        out_specs=[tile_a, tile_a],
        out_shape=[out_shape, out_shape],
        scratch_shapes=[
            pltpu.VMEM((2, GROUP_HEADS, GROUP_WIDTH, blk), BF16),
            pltpu.VMEM((2, GROUP_HEADS, nb, blk), F32),
            pltpu.VMEM((2, GROUP_HEADS, 1, blk), F32),
            pltpu.VMEM((2, GROUP_HEADS, ACC_ROWS, blk), F32),
        ],
        compiler_params=pltpu.CompilerParams(
            dimension_semantics=("arbitrary", "arbitrary", "arbitrary"), vmem_limit_bytes=VMEM_LIMIT_BYTES),
        name="moba",
    )(qt, qt, k4, km3, kn3, vt, gat, gat)


def _block_diag(x_bf16, mask):
    tiled = jnp.concatenate([x_bf16] * GROUP_HEADS, axis=0)
    return jnp.where(mask, tiled, jnp.zeros_like(tiled))


def _rwkv_kernel(rw_ref, edec_ref, gb_ref, gng_ref, gnb_ref, yat1_ref, yat2_ref, x_ref, woa_ref, wob_ref,
                 gain_ref, out_ref, state_ref, y_ref, *, nb):
    tile = rw_ref.shape[0]
    c_w = RWKV_WIDTH

    @pl.when(pl.program_id(1) == 0)
    def _():
        state_ref[...] = jnp.zeros_like(state_ref)

    gr = lax.broadcasted_iota(jnp.int32, (GROUP_WIDTH, GROUP_WIDTH), 0) // HEAD_DIM
    gc = lax.broadcasted_iota(jnp.int32, (GROUP_WIDTH, GROUP_WIDTH), 1) // HEAD_DIM
    diag_mask = gr == gc
    t_idx = lax.broadcasted_iota(jnp.int32, (CHUNK, GROUP_WIDTH), 0)
    j_idx = lax.broadcasted_iota(jnp.int32, (CHUNK, GROUP_WIDTH), 1) % HEAD_DIM
    strict_lower = j_idx < t_idx
    lower = j_idx <= t_idx
    eye = (j_idx == t_idx).astype(F32)
    lane_head = lax.broadcasted_iota(jnp.int32, (1, GROUP_WIDTH), 1) // HEAD_DIM

    def bd(x):
        return _block_diag(x.astype(BF16), diag_mask)

    def diag_blocks(full):
        out = full[(GROUP_HEADS - 1) * HEAD_DIM:, :]
        for h in range(GROUP_HEADS - 2, -1, -1):
            out = jnp.where(lane_head == h, full[h * HEAD_DIM:(h + 1) * HEAD_DIM, :], out)
        return out

    n_groups = c_w // GROUP_WIDTH
    items = []
    for c in range(tile // CHUNK):
        rs = slice(c * CHUNK, (c + 1) * CHUNK)
        end_decay = edec_ref[c]
        for gi in range(n_groups):
            ls = slice(gi * GROUP_WIDTH, (gi + 1) * GROUP_WIDTH)
            operand = lambda n: rw_ref[rs, n * c_w + gi * GROUP_WIDTH:n * c_w + (gi + 1) * GROUP_WIDTH]
            it = dict(rs=rs, ls=ls, gi=gi)
            it["a_t"] = operand(0)
            it["r_t"] = operand(1)
            it["bk_e"] = jnp.concatenate([operand(4), operand(5)], axis=0)
            it["v_c"] = operand(6)
            it["m_diag"] = eye * end_decay[:, ls]
            ar = jnp.concatenate([it["a_t"], it["r_t"]], axis=0)
            pb = _dot_nt(ar, _block_diag(operand(2), diag_mask))
            pk = _dot_nt(ar, _block_diag(operand(3), diag_mask))
            it["pw"] = jnp.where(strict_lower, pb[0:CHUNK], 0.0)
            it["t_inv"] = eye + it["pw"]
            it["a_rb"] = jnp.where(lower, pb[CHUNK:], 0.0).astype(BF16)
            it["a_k"] = jnp.concatenate([jnp.where(strict_lower, pk[0:CHUNK], 0.0),
                                         jnp.where(lower, pk[CHUNK:], 0.0)], axis=0).astype(BF16)
            items.append(it)

    steps = CHUNK.bit_length() - 1
    for s in range(1, steps + 1):
        for it in items:
            w_p = bd(it["pw"])
            if s == 1:
                it["pw"] = _dot(it["pw"].astype(BF16), w_p)
            elif s < steps:
                both = _dot(jnp.concatenate([it["pw"], it["t_inv"]], axis=0).astype(BF16), w_p)
                it["t_inv"] = it["t_inv"] + both[CHUNK:]
                it["pw"] = both[0:CHUNK]
            else:
                it["t_inv"] = (it["t_inv"] + _dot(it["t_inv"].astype(BF16), w_p)).astype(BF16)

    for it in items:
        kv = _dot(it["a_k"], bd(it["v_c"]))
        it["akv"] = kv[0:CHUNK]
        it["arkv"] = kv[CHUNK:]
    for it in items:
        it["w_t"] = _dot(it["t_inv"], bd(it["a_t"])).astype(BF16)
        it["u_t"] = _dot(it["t_inv"], bd(it["akv"])).astype(BF16)
    for it in items:
        it["r_hat"] = (it["r_t"].astype(F32) + _dot(it["a_rb"], bd(it["w_t"]))).astype(BF16)
        it["y_hat"] = _dot(it["a_rb"], bd(it["u_t"])) + it["arkv"]
    for it in items:
        m_full = _dot_tn(it["bk_e"][0:CHUNK], it["w_t"])
        it["m"] = (it["m_diag"] + diag_blocks(m_full)).astype(BF16)
        n_full = _dot_tn(it["bk_e"], jnp.concatenate([it["u_t"], it["v_c"]], axis=0))
        it["n"] = diag_blocks(n_full)

    in_first = pl.program_id(1) < nb // 2
    per = yat1_ref.shape[1]
    yat_t = jnp.concatenate([jnp.transpose(jnp.where(in_first, yat1_ref[0, s], yat2_ref[0, per - 1 - s]))
                             for s in range(per)], axis=0)
    n_chunks = tile // CHUNK
    n_pieces = woa_ref.shape[1] // GROUP_WIDTH
    attn_out = []
    states = [state_ref[gi] for gi in range(n_groups)]
    for it in items:
        gi = it["gi"]
        res = _dot(jnp.concatenate([it["m"], it["r_hat"]], axis=0), bd(states[gi]))
        y_ref[it["rs"], it["ls"]] = res[CHUNK:] + it["y_hat"]
        states[gi] = res[0:CHUNK] + it["n"]
        done = it["rs"].stop // CHUNK
        if gi == n_groups - 1 and done * n_pieces % n_chunks == 0:
            c0 = len(attn_out) * GROUP_WIDTH
            attn_out.append(_dot(yat_t, woa_ref[:, c0:c0 + GROUP_WIDTH]))
    for gi in range(n_groups):
        state_ref[gi] = states[gi]

    y = y_ref[...]
    mean = _head_sum(y) * (1.0 / HEAD_DIM)
    d = y - mean
    var = _head_sum(d * d) * (1.0 / HEAD_DIM)
    yn = d * lax.rsqrt(var + GN_EPS) * gng_ref[...] + gnb_ref[...]
    bonus = rw_ref[:, (RW_OPERANDS - 1) * c_w:].astype(F32)
    yb = ((yn + bonus) * gb_ref[...].astype(F32)).astype(BF16)

    h = x_ref[...] + jnp.concatenate(attn_out, axis=-1) + _dot(yb, wob_ref[...])
    ms = jnp.mean(h * h, axis=-1, keepdims=True)
    out_ref[...] = (h * lax.rsqrt(ms + RMS_EPS) * gain_ref[...]).astype(out_ref.dtype)


def _rwkv(rw, edec, gb, gn_gain, gn_bias, yat_first, yat_second, x2d, woa, wob, out_gain, batch, seq):
    tile, per = ROW_TILE, BLOCKS_PER_TILE
    nt = seq // tile
    assert nt % 2 == 0
    half = nt // 2
    d_model = x2d.shape[1]
    row_spec = lambda width: pl.BlockSpec((tile, width), lambda b, i: (b * nt + i, 0))
    const = lambda shape: pl.BlockSpec(shape, lambda b, i: (0,) * len(shape))
    vec = lambda width: const((1, width))
    blk_t = lambda pick: pl.BlockSpec((1, per, ATTN_WIDTH, MOBA_BLOCK),
                                      lambda b, i: (b, jnp.minimum(pick(i), half - 1), 0, 0))
    return pl.pallas_call(
        functools.partial(_rwkv_kernel, nb=nt),
        grid=(batch, nt),
        in_specs=[
            row_spec(RW_OPERANDS * RWKV_WIDTH),
            pl.BlockSpec((tile // CHUNK, 1, RWKV_WIDTH), lambda b, i: (b * nt + i, 0, 0)),
            row_spec(RWKV_WIDTH),
            vec(RWKV_WIDTH),
            vec(RWKV_WIDTH),
            blk_t(lambda i: i),
            blk_t(lambda i: nt - 1 - i),
            row_spec(d_model),
            const((ATTN_WIDTH, d_model)),
            const((RWKV_WIDTH, d_model)),
            vec(d_model),
        ],
        out_specs=row_spec(d_model),
        out_shape=jax.ShapeDtypeStruct(x2d.shape, x2d.dtype),
        scratch_shapes=[
            pltpu.VMEM((RWKV_WIDTH // GROUP_WIDTH, HEAD_DIM, GROUP_WIDTH), F32),
            pltpu.VMEM((tile, RWKV_WIDTH), F32),
        ],
        compiler_params=pltpu.CompilerParams(
            dimension_semantics=("arbitrary", "arbitrary"), vmem_limit_bytes=VMEM_LIMIT_BYTES),
        name="rwkv_out",
    )(rw, edec, gb, gn_gain, gn_bias, yat_first, yat_second, x2d, woa, wob, out_gain)


def _rotary_column_order():
    order = []
    for g in range(ATTN_WIDTH // GROUP_WIDTH):
        for part in range(2):
            for hh in range(GROUP_HEADS):
                head = g * GROUP_HEADS + hh
                order.extend(head * HEAD_DIM + part * HALF + j for j in range(HALF))
    return np.asarray(order, np.int32)


def _layer(h2d, batch, seq, norm_gain, w_in, shift_mu, decay_bias, decay_up, iclr_bias, iclr_up,
           k_k, k_a, r_k, gn_gain, gn_bias, w_out, out_gain, cos, sin):
    nb = seq // MOBA_BLOCK
    aw, rw = ATTN_WIDTH, RWKV_WIDTH
    order = _rotary_column_order()
    b0 = 4 * aw
    w_in = w_in.astype(BF16)
    wnn = jnp.concatenate([w_in[:, aw:2 * aw][:, order], w_in[:, b0:]], axis=1)
    wnt = jnp.concatenate([w_in[:, 0:aw][:, order], w_in[:, 2 * aw:4 * aw]], axis=1).T
    zeros = jnp.zeros((LORA_RANK, rw), F32)
    wl = jnp.concatenate([jnp.concatenate([decay_up.astype(F32), zeros], axis=1),
                          jnp.concatenate([zeros, iclr_up.astype(F32)], axis=1)], axis=0)
    wlora = wl.astype(BF16)
    row = lambda t: t.astype(F32).reshape(1, -1)

    cos_t = cos.reshape(nb, MOBA_BLOCK, 128).transpose(0, 2, 1)
    sin_t = sin.reshape(nb, MOBA_BLOCK, 128).transpose(0, 2, 1)
    lane = np.arange(aw)
    head_of_lane = (lane // GROUP_WIDTH) * GROUP_HEADS + (lane % 128) // HALF
    hsel = jnp.asarray(head_of_lane[:, None] == np.arange(128)[None, :], BF16)
    rw_vecs = [row(decay_bias), row(iclr_bias), row(k_k), row(k_a), row(r_k)]
    qt, k, kmean, kn2, vt, gat, rw, edec, gb = _in_proj(h2d, row(norm_gain), wnn, wnt, cos, sin, cos_t, sin_t,
                                                        row(shift_mu), hsel, wlora, rw_vecs, batch, nb)
    yat_first, yat_second = _moba(qt, k, kmean, kn2, vt, gat, batch, seq)
    w_out = w_out.astype(BF16)
    return _rwkv(rw, edec, gb, row(gn_gain), row(gn_bias), yat_first, yat_second, h2d, w_out[0:aw], w_out[aw:],
                 out_gain, batch, seq)


def kernel(x, norm_gain, w_in, shift_mu, decay_bias, decay_up, iclr_bias, iclr_up,
           k_k, k_a, r_k, gn_gain, gn_bias, w_out, final_gain):
    batch, seq, d_model = x.shape
    depth = norm_gain.shape[0]
    assert depth == 1, "the final RMSNorm is fused into the single layer's output projection"
    assert seq % MOBA_BLOCK == 0 and w_in.shape[-1] == 4 * ATTN_WIDTH + SHIFT_WIDTH + RWKV_WIDTH
    inv_freq = 1.0 / (ROPE_THETA ** (jnp.arange(0, HEAD_DIM, 2, dtype=F32) / HEAD_DIM))
    ang = jnp.arange(seq, dtype=F32)[:, None] * inv_freq[None, :]
    cos = jnp.tile(jnp.cos(ang), (1, GROUP_HEADS))
    sin = jnp.tile(jnp.sin(ang), (1, GROUP_HEADS))
    out = _layer(x.reshape(batch * seq, d_model).astype(F32), batch, seq, norm_gain[0], w_in[0],
                 shift_mu[0], decay_bias[0], decay_up[0], iclr_bias[0], iclr_up[0], k_k[0], k_a[0], r_k[0],
                 gn_gain[0], gn_bias[0], w_out[0], final_gain.astype(F32).reshape(1, -1), cos, sin)
    return out.reshape(batch, seq, d_model).astype(x.dtype)
```

```python
import functools

import jax
import jax.numpy as jnp
import numpy as np
from jax import lax
from jax.experimental import pallas as pl
from jax.experimental.pallas import tpu as pltpu

HEAD_DIM = 64
HALF = HEAD_DIM // 2
ATTN_HEADS = 8
ATTN_WIDTH = ATTN_HEADS * HEAD_DIM
RWKV_HEADS = 8
RWKV_WIDTH = RWKV_HEADS * HEAD_DIM
LORA_RANK = 64
SHIFT_WIDTH = 3 * RWKV_WIDTH + 2 * LORA_RANK
MOBA_BLOCK = 256
MOBA_TOPK = 3
ROPE_THETA = 10000.0
RMS_EPS = 1e-6
GN_EPS = 64e-5
NORMALIZE_EPS_SQ = 1e-24

GROUP_HEADS = 4
GROUP_WIDTH = GROUP_HEADS * HEAD_DIM
CHUNK = 64
ROW_TILE = 2 * MOBA_BLOCK
BLOCKS_PER_TILE = ROW_TILE // MOBA_BLOCK
RW_OPERANDS = 8
MASKED = -1e30
DECAY_SCALE = -float(np.exp(-0.5))
SCORE_BOUND_MARGIN = 1.0 + 2.0 ** -6
SCORE_BOUND_SAFE = 56.0
STAT_ROWS = 8
ACC_ROWS = HEAD_DIM + 16
LOG2E = 1.4426950408889634
Q_SCALE = HEAD_DIM ** -0.5 * LOG2E
VMEM_LIMIT_BYTES = 48 * 1024 * 1024

F32 = jnp.float32
BF16 = jnp.bfloat16


def _dot(a, b):
    return jnp.dot(a, b, preferred_element_type=F32)


def _dot_nt(a, b):
    return lax.dot_general(a, b, (((1,), (1,)), ((), ())), preferred_element_type=F32)


def _dot_tn(a, b):
    return lax.dot_general(a, b, (((0,), (0,)), ((), ())), preferred_element_type=F32)


def _split_bf16(x):
    hi = x.astype(BF16)
    lo = (x - hi.astype(F32)).astype(BF16)
    return hi, lo


def _silu(z):
    return z / (1.0 + jnp.exp(-z))


def _rotary(acc, cos, sin):
    outs = []
    for g in range(ATTN_WIDTH // GROUP_WIDTH):
        x1 = acc[:, g * GROUP_WIDTH:g * GROUP_WIDTH + 128]
        x2 = acc[:, g * GROUP_WIDTH + 128:(g + 1) * GROUP_WIDTH]
        outs.append(x1 * cos - x2 * sin)
        outs.append(x2 * cos + x1 * sin)
    return jnp.concatenate(outs, axis=-1)


def _rotary_t(acc, cos_t, sin_t):
    outs = []
    for g in range(ATTN_WIDTH // GROUP_WIDTH):
        x1 = acc[g * GROUP_WIDTH:g * GROUP_WIDTH + 128, :]
        x2 = acc[g * GROUP_WIDTH + 128:(g + 1) * GROUP_WIDTH, :]
        outs.append(x1 * cos_t - x2 * sin_t)
        outs.append(x2 * cos_t + x1 * sin_t)
    return jnp.concatenate(outs, axis=0)


def _head_sum(t):
    gr = lax.broadcasted_iota(jnp.int32, (GROUP_WIDTH, GROUP_WIDTH), 0) // HEAD_DIM
    gc = lax.broadcasted_iota(jnp.int32, (GROUP_WIDTH, GROUP_WIDTH), 1) // HEAD_DIM
    head_ones = jnp.where(gr == gc, 1.0, 0.0).astype(BF16)
    t = t.astype(BF16)
    return jnp.concatenate([_dot(t[:, i:i + GROUP_WIDTH], head_ones)
                            for i in range(0, t.shape[1], GROUP_WIDTH)], axis=-1)


def _in_proj_block(s, x_ref, gain_ref, wnn_ref, wnt_ref, cos_ref, sin_ref, cost_ref, sint_ref, mu_ref, hsel_ref,
                   wlora_ref, dbias_ref, ibias_ref, kk_ref, ka_ref, rk_ref,
                   qt_ref, k_ref, kmean_ref, kn2_ref, vt_ref, gat_ref, rw_ref, edec_ref, gb_ref, prev_ref):
    tile = MOBA_BLOCK
    rows = slice(s * tile, (s + 1) * tile)
    x = x_ref[rows, :]
    ms = jnp.mean(x * x, axis=-1, keepdims=True)
    u = (x * lax.rsqrt(ms + RMS_EPS) * gain_ref[...]).astype(BF16)

    k_att = _rotary(_dot(u, wnn_ref[:, 0:ATTN_WIDTH]), cos_ref[rows, :], sin_ref[rows, :])
    k_bf = k_att.astype(BF16)
    for g in range(ATTN_WIDTH // GROUP_WIDTH):
        k_ref[g, rows, :] = k_bf[:, g * GROUP_WIDTH:(g + 1) * GROUP_WIDTH]
    kmean_ref[s] = jnp.mean(k_att, axis=0, keepdims=True)
    k_sq = k_bf.astype(F32)
    k_sq = (k_sq * k_sq).astype(BF16)

    base = ATTN_WIDTH
    c_w = RWKV_WIDTH
    row = lax.broadcasted_iota(jnp.int32, (tile, 1), 0)

    def shifted_cols(c, w):
        p = _dot(u, wnn_ref[:, base + c:base + c + w])
        shifted = jnp.where(row == 0, prev_ref[0:1, c:c + w], pltpu.roll(p, 1, axis=0))
        prev_ref[0:1, c:c + w] = p[tile - 1:tile, :]
        return p + (shifted - p) * mu_ref[:, c:c + w]

    lora_in = shifted_cols(3 * c_w, 2 * LORA_RANK)
    lane128 = lax.broadcasted_iota(jnp.int32, (1, 2 * LORA_RANK), 1)
    lora_in = jnp.where(lane128 < LORA_RANK, jnp.tanh(lora_in), lora_in)
    li_hi, li_lo = _split_bf16(lora_in)
    k = shifted_cols(c_w, c_w)
    kk = k * kk_ref[...]
    r = shifted_cols(0, c_w)
    v = shifted_cols(2 * c_w, c_w)

    kn2_ref[s] = jnp.max(_dot(k_sq, hsel_ref[...]), axis=0, keepdims=True)
    w_hi = wlora_ref[...]
    lora = _dot(li_hi, w_hi) + _dot(li_lo, w_hi)
    kk = kk * lax.rsqrt(jnp.maximum(_head_sum(kk * kk), NORMALIZE_EPS_SQ))

    gb = _dot(u, wnn_ref[:, base + SHIFT_WIDTH:base + SHIFT_WIDTH + RWKV_WIDTH])
    gb_ref[rows, :] = _silu(gb).astype(BF16)

    g = DECAY_SCALE / (1.0 + jnp.exp(-(dbias_ref[...] + lora[:, 0:c_w])))
    a = 1.0 / (1.0 + jnp.exp(-(ibias_ref[...] + lora[:, c_w:2 * c_w])))
    g_hi, g_lo = _split_bf16(g)
    k2 = k * (1.0 + (a - 1.0) * ka_ref[...])
    b = kk * a

    qt = _rotary_t(_dot_nt(wnt_ref[0:ATTN_WIDTH, :], u), cost_ref[s], sint_ref[s])
    qt_ref[0, s] = (qt * Q_SCALE).astype(BF16)

    ti = lax.broadcasted_iota(jnp.int32, (CHUNK, CHUNK), 0)
    tj = lax.broadcasted_iota(jnp.int32, (CHUNK, CHUNK), 1)
    tri_incl = jnp.where(tj <= ti, 1.0, 0.0).astype(BF16)
    cums = [_dot(tri_incl, g_hi[c:c + CHUNK]) + _dot(tri_incl, g_lo[c:c + CHUNK])
            for c in range(0, tile, CHUNK)]
    bonus = _head_sum(r * k2 * rk_ref[...]) * v

    for i, cum in enumerate(cums):
        rs = slice(i * CHUNK, (i + 1) * CHUNK)
        dec_in = jnp.exp(cum)
        dec_ex = jnp.exp(cum - g[rs])
        inv = 1.0 / dec_in
        end_decay = dec_in[CHUNK - 1:CHUNK, :]
        to_end = inv * end_decay
        edec_ref[s * (tile // CHUNK) + i] = end_decay
        cols = (-kk[rs] * dec_ex, r[rs] * dec_in, b[rs] * inv, k2[rs] * inv,
                b[rs] * to_end, k2[rs] * to_end, v[rs], bonus[rs])
        for n, col in enumerate(cols):
            rw_ref[s * tile + i * CHUNK:s * tile + (i + 1) * CHUNK, n * c_w:(n + 1) * c_w] = col.astype(BF16)

    vt_ref[0, s] = _dot_nt(wnt_ref[ATTN_WIDTH:2 * ATTN_WIDTH, :], u).astype(BF16)
    gat = _dot_nt(wnt_ref[2 * ATTN_WIDTH:3 * ATTN_WIDTH, :], u)
    gat_ref[0, s] = _silu(gat).astype(BF16)


def _in_proj_kernel(x_ref, gain_ref, wnn_ref, wnt_ref, cos_ref, sin_ref, cost_ref, sint_ref, mu_ref, hsel_ref,
                    wlora_ref, dbias_ref, ibias_ref, kk_ref, ka_ref, rk_ref,
                    qt_ref, k_ref, kmean_ref, kn2_ref, vt_ref, gat_ref, rw_ref, edec_ref, gb_ref, prev_ref,
                    *, nb):
    @pl.when(pl.program_id(0) % (nb // BLOCKS_PER_TILE) == 0)
    def _():
        prev_ref[...] = jnp.zeros_like(prev_ref)

    for s in range(BLOCKS_PER_TILE):
        _in_proj_block(s, x_ref, gain_ref, wnn_ref, wnt_ref, cos_ref, sin_ref, cost_ref, sint_ref, mu_ref,
                       hsel_ref, wlora_ref, dbias_ref, ibias_ref, kk_ref, ka_ref, rk_ref,
                       qt_ref, k_ref, kmean_ref, kn2_ref, vt_ref, gat_ref, rw_ref, edec_ref, gb_ref, prev_ref)


def _in_proj(x2d, gain, wnn, wnt, cos, sin, cos_t, sin_t, mu, hsel, wlora, rw_vecs, batch, nb):
    rows, d_model = x2d.shape
    tile, per = ROW_TILE, BLOCKS_PER_TILE
    assert nb % per == 0
    nt = nb // per
    row_spec = lambda width: pl.BlockSpec((tile, width), lambda i: (i, 0))
    const = lambda shape: pl.BlockSpec(shape, lambda i: (0,) * len(shape))
    blk_t = pl.BlockSpec((1, per, ATTN_WIDTH, MOBA_BLOCK), lambda i: (i // nt, i % nt, 0, 0))
    blk_t_shape = jax.ShapeDtypeStruct((batch, nb, ATTN_WIDTH, MOBA_BLOCK), BF16)
    return pl.pallas_call(
        functools.partial(_in_proj_kernel, nb=nb),
        grid=(rows // tile,),
        in_specs=[
            row_spec(d_model),
            const((1, d_model)),
            const(wnn.shape),
            const(wnt.shape),
            pl.BlockSpec((tile, 128), lambda i: (i % nt, 0)),
            pl.BlockSpec((tile, 128), lambda i: (i % nt, 0)),
            pl.BlockSpec((per, 128, MOBA_BLOCK), lambda i: (i % nt, 0, 0)),
            pl.BlockSpec((per, 128, MOBA_BLOCK), lambda i: (i % nt, 0, 0)),
            const((1, SHIFT_WIDTH)),
            const(hsel.shape),
            const(wlora.shape),
        ] + [const((1, RWKV_WIDTH))] * len(rw_vecs),
        out_specs=[
            blk_t,
            pl.BlockSpec((ATTN_WIDTH // GROUP_WIDTH, tile, GROUP_WIDTH), lambda i: (0, i, 0)),
            pl.BlockSpec((per, 1, ATTN_WIDTH), lambda i: (i, 0, 0)),
            pl.BlockSpec((per, 1, 128), lambda i: (i, 0, 0)),
            blk_t,
            blk_t,
            row_spec(RW_OPERANDS * RWKV_WIDTH),
            pl.BlockSpec((tile // CHUNK, 1, RWKV_WIDTH), lambda i: (i, 0, 0)),
            row_spec(RWKV_WIDTH),
        ],
        out_shape=[
            blk_t_shape,
            jax.ShapeDtypeStruct((ATTN_WIDTH // GROUP_WIDTH, rows, GROUP_WIDTH), BF16),
            jax.ShapeDtypeStruct((rows // MOBA_BLOCK, 1, ATTN_WIDTH), F32),
            jax.ShapeDtypeStruct((rows // MOBA_BLOCK, 1, 128), F32),
            blk_t_shape,
            blk_t_shape,
            jax.ShapeDtypeStruct((rows, RW_OPERANDS * RWKV_WIDTH), BF16),
            jax.ShapeDtypeStruct((rows // CHUNK, 1, RWKV_WIDTH), F32),
            jax.ShapeDtypeStruct((rows, RWKV_WIDTH), BF16),
        ],
        scratch_shapes=[pltpu.VMEM((8, SHIFT_WIDTH), F32)],
        compiler_params=pltpu.CompilerParams(
            dimension_semantics=("arbitrary",), vmem_limit_bytes=VMEM_LIMIT_BYTES),
        name="in_proj",
    )(x2d, gain, wnn, wnt, cos, sin, cos_t, sin_t, mu, hsel, wlora, *rw_vecs)


def _moba_kernel(qta_ref, qtb_ref, k_ref, kmean_ref, kn2_ref, vt_ref, gata_ref, gatb_ref, outa_ref, outb_ref,
                 qh_ref, bias_ref, m_ref, acc_ref, *, nb):
    j = pl.program_id(2)
    blk = MOBA_BLOCK
    tiles = ((qta_ref, gata_ref, outa_ref, j), (qtb_ref, gatb_ref, outb_ref, nb - 1 - j))
    feat = lax.broadcasted_iota(jnp.int32, (GROUP_WIDTH, 1), 0)
    head_of_feat = (feat % 128) // HALF
    ones_rows = jnp.ones((ACC_ROWS - HEAD_DIM, blk), BF16)
    km = kmean_ref[0]
    km_lane_head = (lax.broadcasted_iota(jnp.int32, (1, GROUP_WIDTH), 1) % 128) // HALF
    km_heads = jnp.concatenate([jnp.where(km_lane_head == h, km, 0.0) for h in range(GROUP_HEADS)], axis=0)
    km_hi, km_lo = _split_bf16(km_heads)
    blk_id = lax.broadcasted_iota(jnp.int32, (GROUP_HEADS, nb, blk), 1)
    k_pos = lax.broadcasted_iota(jnp.int32, (blk, blk), 0)
    q_pos = lax.broadcasted_iota(jnp.int32, (blk, blk), 1)
    causal = k_pos <= q_pos
    kn2 = kn2_ref[0]
    kn2_row = lax.broadcasted_iota(jnp.int32, kn2.shape, 0)
    kn2_lane = lax.broadcasted_iota(jnp.int32, (1, kn2.shape[1]), 1)
    first_head = pl.program_id(1) * GROUP_HEADS

    def values_ext(ki, h):
        return jnp.concatenate([vt_ref[0, ki, h * HEAD_DIM:(h + 1) * HEAD_DIM, :], ones_rows], axis=0)

    bounds = []
    for t, (qt_ref, _, _, qi) in enumerate(tiles):
        qt = qt_ref[0, 0]
        q_sq = qt.astype(F32)
        q_sq = q_sq * q_sq
        kn2_seen = jnp.max(jnp.where(kn2_row <= qi, kn2, 0.0), axis=0, keepdims=True)
        for h in range(GROUP_HEADS):
            qn2 = (jnp.sum(q_sq[h * HALF:(h + 1) * HALF], axis=0, keepdims=True)
                   + jnp.sum(q_sq[128 + h * HALF:128 + (h + 1) * HALF], axis=0, keepdims=True))
            kn2_h = jnp.sum(jnp.where(kn2_lane == first_head + h, kn2_seen, 0.0), axis=1, keepdims=True)
            bounds.append(jnp.sqrt(kn2_h * qn2) * SCORE_BOUND_MARGIN)
            qh_ref[t, h] = jnp.where(head_of_feat == h, qt, jnp.zeros_like(qt))
        gate = (_dot(km_hi, qt) + _dot(km_lo, qt)).reshape(GROUP_HEADS, nb, blk)
        gate = jnp.where(blk_id < qi, gate, -jnp.inf)
        bias = jnp.full(gate.shape, MASKED, F32)
        for r in range(min(MOBA_TOPK, nb)):
            top = jnp.max(gate, axis=1, keepdims=True)
            idx = jnp.min(jnp.where(gate == top, blk_id, nb), axis=1, keepdims=True)
            pick = blk_id == idx + jnp.where(r < qi, 0, nb)
            bias = jnp.where(pick, 0.0, bias)
            gate = jnp.where(pick, -jnp.inf, gate)
        bias_ref[t] = bias

    def issue_scores(t, ki):
        kb = k_ref[0, 0, pl.ds(pl.multiple_of(ki * blk, blk), blk), :]
        return [_dot(kb, qh_ref[t, h]) for h in range(GROUP_HEADS)]

    def finish_own(t, ki, scores):
        for h in range(GROUP_HEADS):
            s = jnp.where(causal, scores[h], MASKED)
            m = jnp.max(s, axis=0, keepdims=True)
            p = jnp.exp2(s - m).astype(BF16)
            m_ref[t, h] = jnp.broadcast_to(m, (STAT_ROWS, blk))
            acc_ref[t, h] = _dot(values_ext(ki, h), p)

    def finish_past(t, ki, scores):
        for h in range(GROUP_HEADS):
            s = scores[h]
            bias = bias_ref[t, h, pl.ds(ki, 1), :]
            m_old = m_ref[t, h, 0:1, :]
            m_new = jnp.maximum(m_old, jnp.max(s, axis=0, keepdims=True) + bias)
            alpha = jnp.exp2(m_old - m_new)
            p = jnp.exp2(s - (m_new - bias)).astype(BF16)
            m_ref[t, h] = jnp.broadcast_to(m_new, (STAT_ROWS, blk))
            acc_ref[t, h] = alpha * acc_ref[t, h] + _dot(values_ext(ki, h), p)

    def bounded_own(t, ki, scores):
        for h in range(GROUP_HEADS):
            s = jnp.where(causal, scores[h], MASKED)
            p = jnp.exp2(s - m_ref[t, h, 0:1, :]).astype(BF16)
            acc_ref[t, h] = _dot(values_ext(ki, h), p)

    def bounded_past(t, ki, scores):
        for h in range(GROUP_HEADS):
            bias = bias_ref[t, h, pl.ds(ki, 1), :]
            p = jnp.exp2(scores[h] - (m_ref[t, h, 0:1, :] - bias)).astype(BF16)
            acc_ref[t, h] = acc_ref[t, h] + _dot(values_ext(ki, h), p)

    def attend(own, past):
        work = [(own, 0, j), (own, 1, nb - 1 - j)]
        for n in range(nb - 1):
            second = (n >= j).astype(jnp.int32)
            work.append((past, second, n - second * j))
        pending = None
        for fin, t, ki in work:
            scores = issue_scores(t, ki)
            if pending is not None:
                pending[0](*pending[1:])
            pending = (fin, t, ki, scores)
        pending[0](*pending[1:])

    bound_ok = jnp.max(jnp.concatenate(bounds, axis=0)) < SCORE_BOUND_SAFE

    @pl.when(bound_ok)
    def _():
        for i, bound in enumerate(bounds):
            m_ref[i // GROUP_HEADS, i % GROUP_HEADS] = jnp.broadcast_to(bound, (STAT_ROWS, blk))
        attend(bounded_own, bounded_past)

    @pl.when(jnp.logical_not(bound_ok))
    def _():
        attend(finish_own, finish_past)

    for t, (_, gat_ref, out_ref, _) in enumerate(tiles):
        for h in range(GROUP_HEADS):
            rows = slice(h * HEAD_DIM, (h + 1) * HEAD_DIM)
            acc = acc_ref[t, h]
            y = acc[0:HEAD_DIM] / acc[HEAD_DIM:HEAD_DIM + 1]
            out_ref[0, 0, rows, :] = (y * gat_ref[0, 0, rows, :].astype(F32)).astype(BF16)


def _moba(qt, k, kmean, kn2, vt, gat, batch, seq):
    nb = seq // MOBA_BLOCK
    assert nb % 2 == 0
    blk = MOBA_BLOCK
    groups = ATTN_WIDTH // GROUP_WIDTH
    k4 = k.reshape(groups, batch, seq, GROUP_WIDTH)
    km3 = kmean.reshape(batch, nb, ATTN_WIDTH)
    kn3 = kn2.reshape(batch, nb, 128)
    tile_a = pl.BlockSpec((1, 1, GROUP_WIDTH, blk), lambda b, g, j: (b, j, g, 0))
    tile_b = pl.BlockSpec((1, 1, GROUP_WIDTH, blk), lambda b, g, j: (b, nb - 1 - j, g, 0))
    out_shape = jax.ShapeDtypeStruct((batch, nb // 2, ATTN_WIDTH, blk), BF16)
    return pl.pallas_call(
        functools.partial(_moba_kernel, nb=nb),
        grid=(batch, groups, nb // 2),
        in_specs=[
            tile_a,
            tile_b,
            pl.BlockSpec((1, 1, seq, GROUP_WIDTH), lambda b, g, j: (g, b, 0, 0)),
            pl.BlockSpec((1, nb, GROUP_WIDTH), lambda b, g, j: (b, 0, g)),
            pl.BlockSpec((1, nb, 128), lambda b, g, j: (b, 0, 0)),
            pl.BlockSpec((1, nb, GROUP_WIDTH, blk), lambda b, g, j: (b, 0, g, 0)),
            tile_a,
            tile_b,
        ],
        out_specs=[tile_a, tile_a],
        out_shape=[out_shape, out_shape],
        scratch_shapes=[
            pltpu.VMEM((2, GROUP_HEADS, GROUP_WIDTH, blk), BF16),
            pltpu.VMEM((2, GROUP_HEADS, nb, blk), F32),
            pltpu.VMEM((2, GROUP_HEADS, STAT_ROWS, blk), F32),
            pltpu.VMEM((2, GROUP_HEADS, ACC_ROWS, blk), F32),
        ],
        compiler_params=pltpu.CompilerParams(
            dimension_semantics=("arbitrary", "arbitrary", "arbitrary"), vmem_limit_bytes=VMEM_LIMIT_BYTES),
        name="moba",
    )(qt, qt, k4, km3, kn3, vt, gat, gat)


def _block_diag(x_bf16, mask):
    tiled = jnp.concatenate([x_bf16] * GROUP_HEADS, axis=0)
    return jnp.where(mask, tiled, jnp.zeros_like(tiled))


def _rwkv_kernel(rw_ref, edec_ref, gb_ref, gng_ref, gnb_ref, yat1_ref, yat2_ref, x_ref, woa_ref, wob_ref,
                 gain_ref, out_ref, state_ref, y_ref, *, nb):
    tile = rw_ref.shape[0]
    c_w = RWKV_WIDTH

    @pl.when(pl.program_id(1) == 0)
    def _():
        state_ref[...] = jnp.zeros_like(state_ref)

    gr = lax.broadcasted_iota(jnp.int32, (GROUP_WIDTH, GROUP_WIDTH), 0) // HEAD_DIM
    gc = lax.broadcasted_iota(jnp.int32, (GROUP_WIDTH, GROUP_WIDTH), 1) // HEAD_DIM
    diag_mask = gr == gc
    t_idx = lax.broadcasted_iota(jnp.int32, (CHUNK, GROUP_WIDTH), 0)
    j_idx = lax.broadcasted_iota(jnp.int32, (CHUNK, GROUP_WIDTH), 1) % HEAD_DIM
    strict_lower = j_idx < t_idx
    lower = j_idx <= t_idx
    eye = (j_idx == t_idx).astype(F32)
    lane_head = lax.broadcasted_iota(jnp.int32, (1, GROUP_WIDTH), 1) // HEAD_DIM

    def bd(x):
        return _block_diag(x.astype(BF16), diag_mask)

    def diag_blocks(full):
        out = full[(GROUP_HEADS - 1) * HEAD_DIM:, :]
        for h in range(GROUP_HEADS - 2, -1, -1):
            out = jnp.where(lane_head == h, full[h * HEAD_DIM:(h + 1) * HEAD_DIM, :], out)
        return out

    n_groups = c_w // GROUP_WIDTH
    items = []
    for c in range(tile // CHUNK):
        rs = slice(c * CHUNK, (c + 1) * CHUNK)
        end_decay = edec_ref[c]
        for gi in range(n_groups):
            ls = slice(gi * GROUP_WIDTH, (gi + 1) * GROUP_WIDTH)
            operand = lambda n: rw_ref[rs, n * c_w + gi * GROUP_WIDTH:n * c_w + (gi + 1) * GROUP_WIDTH]
            it = dict(rs=rs, ls=ls, gi=gi)
            it["a_t"] = operand(0)
            it["r_t"] = operand(1)
            it["bk_e"] = jnp.concatenate([operand(4), operand(5)], axis=0)
            it["v_c"] = operand(6)
            it["m_diag"] = eye * end_decay[:, ls]
            ar = jnp.concatenate([it["a_t"], it["r_t"]], axis=0)
            pb = _dot_nt(ar, _block_diag(operand(2), diag_mask))
            pk = _dot_nt(ar, _block_diag(operand(3), diag_mask))
            it["pw"] = jnp.where(strict_lower, pb[0:CHUNK], 0.0)
            it["t_inv"] = eye + it["pw"]
            it["a_rb"] = jnp.where(lower, pb[CHUNK:], 0.0).astype(BF16)
            it["a_k"] = jnp.concatenate([jnp.where(strict_lower, pk[0:CHUNK], 0.0),
                                         jnp.where(lower, pk[CHUNK:], 0.0)], axis=0).astype(BF16)
            items.append(it)

    steps = CHUNK.bit_length() - 1
    for s in range(1, steps + 1):
        for it in items:
            w_p = bd(it["pw"])
            if s == 1:
                it["pw"] = _dot(it["pw"].astype(BF16), w_p)
            elif s < steps:
                both = _dot(jnp.concatenate([it["pw"], it["t_inv"]], axis=0).astype(BF16), w_p)
                it["t_inv"] = it["t_inv"] + both[CHUNK:]
                it["pw"] = both[0:CHUNK]
            else:
                it["t_inv"] = (it["t_inv"] + _dot(it["t_inv"].astype(BF16), w_p)).astype(BF16)

    for it in items:
        kv = _dot(it["a_k"], bd(it["v_c"]))
        it["akv"] = kv[0:CHUNK]
        it["arkv"] = kv[CHUNK:]
    for it in items:
        it["w_t"] = _dot(it["t_inv"], bd(it["a_t"])).astype(BF16)
        it["u_t"] = _dot(it["t_inv"], bd(it["akv"])).astype(BF16)
    for it in items:
        it["r_hat"] = (it["r_t"].astype(F32) + _dot(it["a_rb"], bd(it["w_t"]))).astype(BF16)
        it["y_hat"] = _dot(it["a_rb"], bd(it["u_t"])) + it["arkv"]
    for it in items:
        m_full = _dot_tn(it["bk_e"][0:CHUNK], it["w_t"])
        it["m"] = (it["m_diag"] + diag_blocks(m_full)).astype(BF16)
        n_full = _dot_tn(it["bk_e"], jnp.concatenate([it["u_t"], it["v_c"]], axis=0))
        it["n"] = diag_blocks(n_full)

    in_first = pl.program_id(1) < nb // 2
    per = yat1_ref.shape[1]
    yat_t = jnp.concatenate([jnp.transpose(jnp.where(in_first, yat1_ref[0, s], yat2_ref[0, per - 1 - s]))
                             for s in range(per)], axis=0)
    n_chunks = tile // CHUNK
    n_pieces = woa_ref.shape[1] // GROUP_WIDTH
    attn_out = []
    states = [state_ref[gi] for gi in range(n_groups)]
    for it in items:
        gi = it["gi"]
        res = _dot(jnp.concatenate([it["m"], it["r_hat"]], axis=0), bd(states[gi]))
        y_ref[it["rs"], it["ls"]] = res[CHUNK:] + it["y_hat"]
        states[gi] = res[0:CHUNK] + it["n"]
        done = it["rs"].stop // CHUNK
        if gi == n_groups - 1 and done * n_pieces % n_chunks == 0:
            c0 = len(attn_out) * GROUP_WIDTH
            attn_out.append(_dot(yat_t, woa_ref[:, c0:c0 + GROUP_WIDTH]))
    for gi in range(n_groups):
        state_ref[gi] = states[gi]

    y = y_ref[...]
    mean = _head_sum(y) * (1.0 / HEAD_DIM)
    d = y - mean
    var = _head_sum(d * d) * (1.0 / HEAD_DIM)
    yn = d * lax.rsqrt(var + GN_EPS) * gng_ref[...] + gnb_ref[...]
    bonus = rw_ref[:, (RW_OPERANDS - 1) * c_w:].astype(F32)
    yb = ((yn + bonus) * gb_ref[...].astype(F32)).astype(BF16)

    h = x_ref[...] + jnp.concatenate(attn_out, axis=-1) + _dot(yb, wob_ref[...])
    ms = jnp.mean(h * h, axis=-1, keepdims=True)
    out_ref[...] = (h * lax.rsqrt(ms + RMS_EPS) * gain_ref[...]).astype(out_ref.dtype)


def _rwkv(rw, edec, gb, gn_gain, gn_bias, yat_first, yat_second, x2d, woa, wob, out_gain, batch, seq):
    tile, per = ROW_TILE, BLOCKS_PER_TILE
    nt = seq // tile
    assert nt % 2 == 0
    half = nt // 2
    d_model = x2d.shape[1]
    row_spec = lambda width: pl.BlockSpec((tile, width), lambda b, i: (b * nt + i, 0))
    const = lambda shape: pl.BlockSpec(shape, lambda b, i: (0,) * len(shape))
    vec = lambda width: const((1, width))
    blk_t = lambda pick: pl.BlockSpec((1, per, ATTN_WIDTH, MOBA_BLOCK),
                                      lambda b, i: (b, jnp.minimum(pick(i), half - 1), 0, 0))
    return pl.pallas_call(
        functools.partial(_rwkv_kernel, nb=nt),
        grid=(batch, nt),
        in_specs=[
            row_spec(RW_OPERANDS * RWKV_WIDTH),
            pl.BlockSpec((tile // CHUNK, 1, RWKV_WIDTH), lambda b, i: (b * nt + i, 0, 0)),
            row_spec(RWKV_WIDTH),
            vec(RWKV_WIDTH),
            vec(RWKV_WIDTH),
            blk_t(lambda i: i),
            blk_t(lambda i: nt - 1 - i),
            row_spec(d_model),
            const((ATTN_WIDTH, d_model)),
            const((RWKV_WIDTH, d_model)),
            vec(d_model),
        ],
        out_specs=row_spec(d_model),
        out_shape=jax.ShapeDtypeStruct(x2d.shape, x2d.dtype),
        scratch_shapes=[
            pltpu.VMEM((RWKV_WIDTH // GROUP_WIDTH, HEAD_DIM, GROUP_WIDTH), F32),
            pltpu.VMEM((tile, RWKV_WIDTH), F32),
        ],
        compiler_params=pltpu.CompilerParams(
            dimension_semantics=("arbitrary", "arbitrary"), vmem_limit_bytes=VMEM_LIMIT_BYTES),
        name="rwkv_out",
    )(rw, edec, gb, gn_gain, gn_bias, yat_first, yat_second, x2d, woa, wob, out_gain)


def _rotary_column_order():
    order = []
    for g in range(ATTN_WIDTH // GROUP_WIDTH):
        for part in range(2):
            for hh in range(GROUP_HEADS):
                head = g * GROUP_HEADS + hh
                order.extend(head * HEAD_DIM + part * HALF + j for j in range(HALF))
    return np.asarray(order, np.int32)


def _layer(h2d, batch, seq, norm_gain, w_in, shift_mu, decay_bias, decay_up, iclr_bias, iclr_up,
           k_k, k_a, r_k, gn_gain, gn_bias, w_out, out_gain, cos, sin):
    nb = seq // MOBA_BLOCK
    aw, rw = ATTN_WIDTH, RWKV_WIDTH
    order = _rotary_column_order()
    b0 = 4 * aw
    w_in = w_in.astype(BF16)
    wnn = jnp.concatenate([w_in[:, aw:2 * aw][:, order], w_in[:, b0:]], axis=1)
    wnt = jnp.concatenate([w_in[:, 0:aw][:, order], w_in[:, 2 * aw:4 * aw]], axis=1).T
    zeros = jnp.zeros((LORA_RANK, rw), F32)
    wl = jnp.concatenate([jnp.concatenate([decay_up.astype(F32), zeros], axis=1),
                          jnp.concatenate([zeros, iclr_up.astype(F32)], axis=1)], axis=0)
    wlora = wl.astype(BF16)
    row = lambda t: t.astype(F32).reshape(1, -1)

    cos_t = cos.reshape(nb, MOBA_BLOCK, 128).transpose(0, 2, 1)
    sin_t = sin.reshape(nb, MOBA_BLOCK, 128).transpose(0, 2, 1)
    lane = np.arange(aw)
    head_of_lane = (lane // GROUP_WIDTH) * GROUP_HEADS + (lane % 128) // HALF
    hsel = jnp.asarray(head_of_lane[:, None] == np.arange(128)[None, :], BF16)
    rw_vecs = [row(decay_bias), row(iclr_bias), row(k_k), row(k_a), row(r_k)]
    qt, k, kmean, kn2, vt, gat, rw, edec, gb = _in_proj(h2d, row(norm_gain), wnn, wnt, cos, sin, cos_t, sin_t,
                                                        row(shift_mu), hsel, wlora, rw_vecs, batch, nb)
    yat_first, yat_second = _moba(qt, k, kmean, kn2, vt, gat, batch, seq)
    w_out = w_out.astype(BF16)
    return _rwkv(rw, edec, gb, row(gn_gain), row(gn_bias), yat_first, yat_second, h2d, w_out[0:aw], w_out[aw:],
                 out_gain, batch, seq)


def kernel(x, norm_gain, w_in, shift_mu, decay_bias, decay_up, iclr_bias, iclr_up,
           k_k, k_a, r_k, gn_gain, gn_bias, w_out, final_gain):
    batch, seq, d_model = x.shape
    depth = norm_gain.shape[0]
    assert depth == 1, "the final RMSNorm is fused into the single layer's output projection"
    assert seq % MOBA_BLOCK == 0 and w_in.shape[-1] == 4 * ATTN_WIDTH + SHIFT_WIDTH + RWKV_WIDTH
    inv_freq = 1.0 / (ROPE_THETA ** (jnp.arange(0, HEAD_DIM, 2, dtype=F32) / HEAD_DIM))
    ang = jnp.arange(seq, dtype=F32)[:, None] * inv_freq[None, :]
    cos = jnp.tile(jnp.cos(ang), (1, GROUP_HEADS))
    sin = jnp.tile(jnp.sin(ang), (1, GROUP_HEADS))
    out = _layer(x.reshape(batch * seq, d_model).astype(F32), batch, seq, norm_gain[0], w_in[0],
                 shift_mu[0], decay_bias[0], decay_up[0], iclr_bias[0], iclr_up[0], k_k[0], k_a[0], r_k[0],
                 gn_gain[0], gn_bias[0], w_out[0], final_gain.astype(F32).reshape(1, -1), cos, sin)
    return out.reshape(batch, seq, d_model).astype(x.dtype)
```

```python
import functools

import jax
import jax.numpy as jnp
import numpy as np
from jax import lax
from jax.experimental import pallas as pl
from jax.experimental.pallas import tpu as pltpu

HEAD_DIM = 64
HALF = HEAD_DIM // 2
ATTN_HEADS = 8
ATTN_WIDTH = ATTN_HEADS * HEAD_DIM
RWKV_HEADS = 8
RWKV_WIDTH = RWKV_HEADS * HEAD_DIM
LORA_RANK = 64
SHIFT_WIDTH = 3 * RWKV_WIDTH + 2 * LORA_RANK
MOBA_BLOCK = 256
MOBA_TOPK = 3
ROPE_THETA = 10000.0
RMS_EPS = 1e-6
GN_EPS = 64e-5
NORMALIZE_EPS_SQ = 1e-24

GROUP_HEADS = 4
GROUP_WIDTH = GROUP_HEADS * HEAD_DIM
CHUNK = 64
ROW_TILE = 2 * MOBA_BLOCK
BLOCKS_PER_TILE = ROW_TILE // MOBA_BLOCK
RW_OPERANDS = 8
MASKED = -1e30
DECAY_SCALE = -float(np.exp(-0.5))
SCORE_BOUND_MARGIN = 1.0 + 2.0 ** -6
SCORE_BOUND_SAFE = 56.0
STAT_ROWS = 8
ACC_ROWS = HEAD_DIM + 16
LOG2E = 1.4426950408889634
Q_SCALE = HEAD_DIM ** -0.5 * LOG2E
VMEM_LIMIT_BYTES = 48 * 1024 * 1024

F32 = jnp.float32
BF16 = jnp.bfloat16


def _dot(a, b):
    return jnp.dot(a, b, preferred_element_type=F32)


def _dot_nt(a, b):
    return lax.dot_general(a, b, (((1,), (1,)), ((), ())), preferred_element_type=F32)


def _dot_tn(a, b):
    return lax.dot_general(a, b, (((0,), (0,)), ((), ())), preferred_element_type=F32)


def _split_bf16(x):
    hi = x.astype(BF16)
    lo = (x - hi.astype(F32)).astype(BF16)
    return hi, lo


def _silu(z):
    return z / (1.0 + jnp.exp(-z))


def _rotary(acc, cos, sin):
    outs = []
    for g in range(ATTN_WIDTH // GROUP_WIDTH):
        x1 = acc[:, g * GROUP_WIDTH:g * GROUP_WIDTH + 128]
        x2 = acc[:, g * GROUP_WIDTH + 128:(g + 1) * GROUP_WIDTH]
        outs.append(x1 * cos - x2 * sin)
        outs.append(x2 * cos + x1 * sin)
    return jnp.concatenate(outs, axis=-1)


def _rotary_t(acc, cos_t, sin_t):
    outs = []
    for g in range(ATTN_WIDTH // GROUP_WIDTH):
        x1 = acc[g * GROUP_WIDTH:g * GROUP_WIDTH + 128, :]
        x2 = acc[g * GROUP_WIDTH + 128:(g + 1) * GROUP_WIDTH, :]
        outs.append(x1 * cos_t - x2 * sin_t)
        outs.append(x2 * cos_t + x1 * sin_t)
    return jnp.concatenate(outs, axis=0)


def _head_sum(t):
    gr = lax.broadcasted_iota(jnp.int32, (GROUP_WIDTH, GROUP_WIDTH), 0) // HEAD_DIM
    gc = lax.broadcasted_iota(jnp.int32, (GROUP_WIDTH, GROUP_WIDTH), 1) // HEAD_DIM
    head_ones = jnp.where(gr == gc, 1.0, 0.0).astype(BF16)
    t = t.astype(BF16)
    return jnp.concatenate([_dot(t[:, i:i + GROUP_WIDTH], head_ones)
                            for i in range(0, t.shape[1], GROUP_WIDTH)], axis=-1)


def _in_proj_block(s, x_ref, gain_ref, wnn_ref, wnt_ref, cos_ref, sin_ref, cost_ref, sint_ref, mu_ref, hsel_ref,
                   wlora_ref, dbias_ref, ibias_ref, kk_ref, ka_ref, rk_ref,
                   qt_ref, k_ref, kmean_ref, kn2_ref, vt_ref, gat_ref, rw_ref, edec_ref, gb_ref, prev_ref):
    tile = MOBA_BLOCK
    rows = slice(s * tile, (s + 1) * tile)
    x = x_ref[rows, :]
    ms = jnp.mean(x * x, axis=-1, keepdims=True)
    u = (x * lax.rsqrt(ms + RMS_EPS) * gain_ref[...]).astype(BF16)

    k_att = _rotary(_dot(u, wnn_ref[:, 0:ATTN_WIDTH]), cos_ref[rows, :], sin_ref[rows, :])
    k_bf = k_att.astype(BF16)
    for g in range(ATTN_WIDTH // GROUP_WIDTH):
        k_ref[g, rows, :] = k_bf[:, g * GROUP_WIDTH:(g + 1) * GROUP_WIDTH]
    kmean_ref[s] = jnp.mean(k_att, axis=0, keepdims=True)
    k_sq = k_bf.astype(F32)
    k_sq = (k_sq * k_sq).astype(BF16)

    base = ATTN_WIDTH
    c_w = RWKV_WIDTH
    row = lax.broadcasted_iota(jnp.int32, (tile, 1), 0)

    def shifted_cols(c, w):
        p = _dot(u, wnn_ref[:, base + c:base + c + w])
        shifted = jnp.where(row == 0, prev_ref[0:1, c:c + w], pltpu.roll(p, 1, axis=0))
        prev_ref[0:1, c:c + w] = p[tile - 1:tile, :]
        return p + (shifted - p) * mu_ref[:, c:c + w]

    lora_in = shifted_cols(3 * c_w, 2 * LORA_RANK)
    lane128 = lax.broadcasted_iota(jnp.int32, (1, 2 * LORA_RANK), 1)
    lora_in = jnp.where(lane128 < LORA_RANK, jnp.tanh(lora_in), lora_in)
    li_hi, li_lo = _split_bf16(lora_in)
    k = shifted_cols(c_w, c_w)
    kk = k * kk_ref[...]
    r = shifted_cols(0, c_w)
    v = shifted_cols(2 * c_w, c_w)

    kn2_ref[s] = jnp.max(_dot(k_sq, hsel_ref[...]), axis=0, keepdims=True)
    w_hi = wlora_ref[...]
    lora = _dot(li_hi, w_hi) + _dot(li_lo, w_hi)
    kk = kk * lax.rsqrt(jnp.maximum(_head_sum(kk * kk), NORMALIZE_EPS_SQ))

    gb = _dot(u, wnn_ref[:, base + SHIFT_WIDTH:base + SHIFT_WIDTH + RWKV_WIDTH])
    gb_ref[rows, :] = _silu(gb).astype(BF16)

    g = DECAY_SCALE / (1.0 + jnp.exp(-(dbias_ref[...] + lora[:, 0:c_w])))
    a = 1.0 / (1.0 + jnp.exp(-(ibias_ref[...] + lora[:, c_w:2 * c_w])))
    g_hi, g_lo = _split_bf16(g)
    k2 = k * (1.0 + (a - 1.0) * ka_ref[...])
    b = kk * a

    qt = _rotary_t(_dot_nt(wnt_ref[0:ATTN_WIDTH, :], u), cost_ref[s], sint_ref[s])
    qt_ref[0, s] = (qt * Q_SCALE).astype(BF16)

    ti = lax.broadcasted_iota(jnp.int32, (CHUNK, CHUNK), 0)
    tj = lax.broadcasted_iota(jnp.int32, (CHUNK, CHUNK), 1)
    tri_incl = jnp.where(tj <= ti, 1.0, 0.0).astype(BF16)
    cums = [_dot(tri_incl, g_hi[c:c + CHUNK]) + _dot(tri_incl, g_lo[c:c + CHUNK])
            for c in range(0, tile, CHUNK)]
    bonus = _head_sum(r * k2 * rk_ref[...]) * v

    for i, cum in enumerate(cums):
        rs = slice(i * CHUNK, (i + 1) * CHUNK)
        dec_in = jnp.exp(cum)
        dec_ex = jnp.exp(cum - g[rs])
        inv = 1.0 / dec_in
        end_decay = dec_in[CHUNK - 1:CHUNK, :]
        to_end = inv * end_decay
        edec_ref[s * (tile // CHUNK) + i] = end_decay
        cols = (-kk[rs] * dec_ex, r[rs] * dec_in, b[rs] * inv, k2[rs] * inv,
                b[rs] * to_end, k2[rs] * to_end, v[rs], bonus[rs])
        for n, col in enumerate(cols):
            rw_ref[s * tile + i * CHUNK:s * tile + (i + 1) * CHUNK, n * c_w:(n + 1) * c_w] = col.astype(BF16)

    vt_ref[0, s] = _dot_nt(wnt_ref[ATTN_WIDTH:2 * ATTN_WIDTH, :], u).astype(BF16)
    gat = _dot_nt(wnt_ref[2 * ATTN_WIDTH:3 * ATTN_WIDTH, :], u)
    gat_ref[0, s] = _silu(gat).astype(BF16)


def _in_proj_kernel(x_ref, gain_ref, wnn_ref, wnt_ref, cos_ref, sin_ref, cost_ref, sint_ref, mu_ref, hsel_ref,
                    wlora_ref, dbias_ref, ibias_ref, kk_ref, ka_ref, rk_ref,
                    qt_ref, k_ref, kmean_ref, kn2_ref, vt_ref, gat_ref, rw_ref, edec_ref, gb_ref, prev_ref,
                    *, nb):
    @pl.when(pl.program_id(0) % (nb // BLOCKS_PER_TILE) == 0)
    def _():
        prev_ref[...] = jnp.zeros_like(prev_ref)

    for s in range(BLOCKS_PER_TILE):
        _in_proj_block(s, x_ref, gain_ref, wnn_ref, wnt_ref, cos_ref, sin_ref, cost_ref, sint_ref, mu_ref,
                       hsel_ref, wlora_ref, dbias_ref, ibias_ref, kk_ref, ka_ref, rk_ref,
                       qt_ref, k_ref, kmean_ref, kn2_ref, vt_ref, gat_ref, rw_ref, edec_ref, gb_ref, prev_ref)


def _in_proj(x2d, gain, wnn, wnt, cos, sin, cos_t, sin_t, mu, hsel, wlora, rw_vecs, batch, nb):
    rows, d_model = x2d.shape
    tile, per = ROW_TILE, BLOCKS_PER_TILE
    assert nb % per == 0
    nt = nb // per
    row_spec = lambda width: pl.BlockSpec((tile, width), lambda i: (i, 0))
    const = lambda shape: pl.BlockSpec(shape, lambda i: (0,) * len(shape))
    blk_t = pl.BlockSpec((1, per, ATTN_WIDTH, MOBA_BLOCK), lambda i: (i // nt, i % nt, 0, 0))
    blk_t_shape = jax.ShapeDtypeStruct((batch, nb, ATTN_WIDTH, MOBA_BLOCK), BF16)
    return pl.pallas_call(
        functools.partial(_in_proj_kernel, nb=nb),
        grid=(rows // tile,),
        in_specs=[
            row_spec(d_model),
            const((1, d_model)),
            const(wnn.shape),
            const(wnt.shape),
            pl.BlockSpec((tile, 128), lambda i: (i % nt, 0)),
            pl.BlockSpec((tile, 128), lambda i: (i % nt, 0)),
            pl.BlockSpec((per, 128, MOBA_BLOCK), lambda i: (i % nt, 0, 0)),
            pl.BlockSpec((per, 128, MOBA_BLOCK), lambda i: (i % nt, 0, 0)),
            const((1, SHIFT_WIDTH)),
            const(hsel.shape),
            const(wlora.shape),
        ] + [const((1, RWKV_WIDTH))] * len(rw_vecs),
        out_specs=[
            blk_t,
            pl.BlockSpec((ATTN_WIDTH // GROUP_WIDTH, tile, GROUP_WIDTH), lambda i: (0, i, 0)),
            pl.BlockSpec((per, 1, ATTN_WIDTH), lambda i: (i, 0, 0)),
            pl.BlockSpec((per, 1, 128), lambda i: (i, 0, 0)),
            blk_t,
            blk_t,
            row_spec(RW_OPERANDS * RWKV_WIDTH),
            pl.BlockSpec((tile // CHUNK, 1, RWKV_WIDTH), lambda i: (i, 0, 0)),
            row_spec(RWKV_WIDTH),
        ],
        out_shape=[
            blk_t_shape,
            jax.ShapeDtypeStruct((ATTN_WIDTH // GROUP_WIDTH, rows, GROUP_WIDTH), BF16),
            jax.ShapeDtypeStruct((rows // MOBA_BLOCK, 1, ATTN_WIDTH), F32),
            jax.ShapeDtypeStruct((rows // MOBA_BLOCK, 1, 128), F32),
            blk_t_shape,
            blk_t_shape,
            jax.ShapeDtypeStruct((rows, RW_OPERANDS * RWKV_WIDTH), BF16),
            jax.ShapeDtypeStruct((rows // CHUNK, 1, RWKV_WIDTH), F32),
            jax.ShapeDtypeStruct((rows, RWKV_WIDTH), BF16),
        ],
        scratch_shapes=[pltpu.VMEM((8, SHIFT_WIDTH), F32)],
        compiler_params=pltpu.CompilerParams(
            dimension_semantics=("arbitrary",), vmem_limit_bytes=VMEM_LIMIT_BYTES),
        name="in_proj",
    )(x2d, gain, wnn, wnt, cos, sin, cos_t, sin_t, mu, hsel, wlora, *rw_vecs)


def _moba_kernel(qta_ref, qtb_ref, k_ref, kmean_ref, kn2_ref, vt_ref, gata_ref, gatb_ref, outa_ref, outb_ref,
                 qh_ref, bias_ref, m_ref, acc_ref, *, nb):
    j = pl.program_id(2)
    blk = MOBA_BLOCK
    tiles = ((qta_ref, gata_ref, outa_ref, j), (qtb_ref, gatb_ref, outb_ref, nb - 1 - j))
    feat = lax.broadcasted_iota(jnp.int32, (GROUP_WIDTH, 1), 0)
    head_of_feat = (feat % 128) // HALF
    ones_rows = jnp.ones((ACC_ROWS - HEAD_DIM, blk), BF16)
    km = kmean_ref[0]
    km_lane_head = (lax.broadcasted_iota(jnp.int32, (1, GROUP_WIDTH), 1) % 128) // HALF
    km_heads = jnp.concatenate([jnp.where(km_lane_head == h, km, 0.0) for h in range(GROUP_HEADS)], axis=0)
    km_hi, km_lo = _split_bf16(km_heads)
    blk_id = lax.broadcasted_iota(jnp.int32, (GROUP_HEADS, nb, blk), 1)
    k_pos = lax.broadcasted_iota(jnp.int32, (blk, blk), 0)
    q_pos = lax.broadcasted_iota(jnp.int32, (blk, blk), 1)
    causal = k_pos <= q_pos
    kn2 = kn2_ref[0]
    kn2_row = lax.broadcasted_iota(jnp.int32, kn2.shape, 0)
    kn2_lane = lax.broadcasted_iota(jnp.int32, (1, kn2.shape[1]), 1)
    first_head = pl.program_id(1) * GROUP_HEADS

    def values_ext(ki, h):
        return jnp.concatenate([vt_ref[0, ki, h * HEAD_DIM:(h + 1) * HEAD_DIM, :], ones_rows], axis=0)

    bounds = []
    for t, (qt_ref, _, _, qi) in enumerate(tiles):
        qt = qt_ref[0, 0]
        q_sq = qt.astype(F32)
        q_sq = q_sq * q_sq
        kn2_seen = jnp.max(jnp.where(kn2_row <= qi, kn2, 0.0), axis=0, keepdims=True)
        for h in range(GROUP_HEADS):
            qn2 = (jnp.sum(q_sq[h * HALF:(h + 1) * HALF], axis=0, keepdims=True)
                   + jnp.sum(q_sq[128 + h * HALF:128 + (h + 1) * HALF], axis=0, keepdims=True))
            kn2_h = jnp.sum(jnp.where(kn2_lane == first_head + h, kn2_seen, 0.0), axis=1, keepdims=True)
            bounds.append(jnp.sqrt(kn2_h * qn2) * SCORE_BOUND_MARGIN)
            qh_ref[t, h] = jnp.where(head_of_feat == h, qt, jnp.zeros_like(qt))
        gate = (_dot(km_hi, qt) + _dot(km_lo, qt)).reshape(GROUP_HEADS, nb, blk)
        gate = jnp.where(blk_id < qi, gate, -jnp.inf)
        bias = jnp.full(gate.shape, MASKED, F32)
        for r in range(min(MOBA_TOPK, nb)):
            top = jnp.max(gate, axis=1, keepdims=True)
            idx = jnp.min(jnp.where(gate == top, blk_id, nb), axis=1, keepdims=True)
            pick = blk_id == idx + jnp.where(r < qi, 0, nb)
            bias = jnp.where(pick, 0.0, bias)
            gate = jnp.where(pick, -jnp.inf, gate)
        bias_ref[t] = bias

    def issue_scores(t, ki):
        kb = k_ref[0, 0, pl.ds(pl.multiple_of(ki * blk, blk), blk), :]
        return [_dot(kb, qh_ref[t, h]) for h in range(GROUP_HEADS)]

    def finish_own(t, ki, scores):
        for h in range(GROUP_HEADS):
            s = jnp.where(causal, scores[h], MASKED)
            m = jnp.max(s, axis=0, keepdims=True)
            p = jnp.exp2(s - m).astype(BF16)
            m_ref[t, h] = jnp.broadcast_to(m, (STAT_ROWS, blk))
            acc_ref[t, h] = _dot(values_ext(ki, h), p)

    def finish_past(t, ki, scores):
        for h in range(GROUP_HEADS):
            s = scores[h]
            bias = bias_ref[t, h, pl.ds(ki, 1), :]
            m_old = m_ref[t, h, 0:1, :]
            m_new = jnp.maximum(m_old, jnp.max(s, axis=0, keepdims=True) + bias)
            alpha = jnp.exp2(m_old - m_new)
            p = jnp.exp2(s - (m_new - bias)).astype(BF16)
            m_ref[t, h] = jnp.broadcast_to(m_new, (STAT_ROWS, blk))
            acc_ref[t, h] = alpha * acc_ref[t, h] + _dot(values_ext(ki, h), p)

    def bounded_own(t, ki, scores):
        for h in range(GROUP_HEADS):
            s = jnp.where(causal, scores[h], MASKED)
            p = jnp.exp2(s - m_ref[t, h, 0:1, :]).astype(BF16)
            acc_ref[t, h] = _dot(values_ext(ki, h), p)

    def bounded_past(t, ki, scores):
        for h in range(GROUP_HEADS):
            bias = bias_ref[t, h, pl.ds(ki, 1), :]
            p = jnp.exp2(scores[h] - (m_ref[t, h, 0:1, :] - bias)).astype(BF16)
            acc_ref[t, h] = acc_ref[t, h] + _dot(values_ext(ki, h), p)

    def attend(own, past):
        work = [(own, 0, j), (own, 1, nb - 1 - j)]
        for n in range(nb - 1):
            second = (n >= j).astype(jnp.int32)
            work.append((past, second, n - second * j))
        pending = None
        for fin, t, ki in work:
            scores = issue_scores(t, ki)
            if pending is not None:
                pending[0](*pending[1:])
            pending = (fin, t, ki, scores)
        pending[0](*pending[1:])

    bound_ok = jnp.max(jnp.concatenate(bounds, axis=0)) < SCORE_BOUND_SAFE

    @pl.when(bound_ok)
    def _():
        for i, bound in enumerate(bounds):
            m_ref[i // GROUP_HEADS, i % GROUP_HEADS] = jnp.broadcast_to(bound, (STAT_ROWS, blk))
        attend(bounded_own, bounded_past)

    @pl.when(jnp.logical_not(bound_ok))
    def _():
        attend(finish_own, finish_past)

    for t, (_, gat_ref, out_ref, _) in enumerate(tiles):
        for h in range(GROUP_HEADS):
            rows = slice(h * HEAD_DIM, (h + 1) * HEAD_DIM)
            acc = acc_ref[t, h]
            y = acc[0:HEAD_DIM] / acc[HEAD_DIM:HEAD_DIM + 1]
            out_ref[0, 0, rows, :] = (y * gat_ref[0, 0, rows, :].astype(F32)).astype(BF16)


def _moba(qt, k, kmean, kn2, vt, gat, batch, seq):
    nb = seq // MOBA_BLOCK
    assert nb % 2 == 0
    blk = MOBA_BLOCK
    groups = ATTN_WIDTH // GROUP_WIDTH
    k4 = k.reshape(groups, batch, seq, GROUP_WIDTH)
    km3 = kmean.reshape(batch, nb, ATTN_WIDTH)
    kn3 = kn2.reshape(batch, nb, 128)
    tile_a = pl.BlockSpec((1, 1, GROUP_WIDTH, blk), lambda b, g, j: (b, j, g, 0))
    tile_b = pl.BlockSpec((1, 1, GROUP_WIDTH, blk), lambda b, g, j: (b, nb - 1 - j, g, 0))
    out_shape = jax.ShapeDtypeStruct((batch, nb // 2, ATTN_WIDTH, blk), BF16)
    return pl.pallas_call(
        functools.partial(_moba_kernel, nb=nb),
        grid=(batch, groups, nb // 2),
        in_specs=[
            tile_a,
            tile_b,
            pl.BlockSpec((1, 1, seq, GROUP_WIDTH), lambda b, g, j: (g, b, 0, 0)),
            pl.BlockSpec((1, nb, GROUP_WIDTH), lambda b, g, j: (b, 0, g)),
            pl.BlockSpec((1, nb, 128), lambda b, g, j: (b, 0, 0)),
            pl.BlockSpec((1, nb, GROUP_WIDTH, blk), lambda b, g, j: (b, 0, g, 0)),
            tile_a,
            tile_b,
        ],
        out_specs=[tile_a, tile_a],
        out_shape=[out_shape, out_shape],
        scratch_shapes=[
            pltpu.VMEM((2, GROUP_HEADS, GROUP_WIDTH, blk), BF16),
            pltpu.VMEM((2, GROUP_HEADS, nb, blk), F32),
            pltpu.VMEM((2, GROUP_HEADS, STAT_ROWS, blk), F32),
            pltpu.VMEM((2, GROUP_HEADS, ACC_ROWS, blk), F32),
        ],
        compiler_params=pltpu.CompilerParams(
            dimension_semantics=("arbitrary", "arbitrary", "arbitrary"), vmem_limit_bytes=VMEM_LIMIT_BYTES),
        name="moba",
    )(qt, qt, k4, km3, kn3, vt, gat, gat)


def _block_diag(x_bf16, mask):
    tiled = jnp.concatenate([x_bf16] * GROUP_HEADS, axis=0)
    return jnp.where(mask, tiled, jnp.zeros_like(tiled))


def _rwkv_kernel(rw_ref, edec_ref, gb_ref, gng_ref, gnb_ref, yat1_ref, yat2_ref, x_ref, woa_ref, wob_ref,
                 gain_ref, out_ref, state_ref, y_ref, *, nb):
    tile = rw_ref.shape[0]
    c_w = RWKV_WIDTH

    @pl.when(pl.program_id(1) == 0)
    def _():
        state_ref[...] = jnp.zeros_like(state_ref)

    gr = lax.broadcasted_iota(jnp.int32, (GROUP_WIDTH, GROUP_WIDTH), 0) // HEAD_DIM
    gc = lax.broadcasted_iota(jnp.int32, (GROUP_WIDTH, GROUP_WIDTH), 1) // HEAD_DIM
    diag_mask = gr == gc
    t_idx = lax.broadcasted_iota(jnp.int32, (CHUNK, GROUP_WIDTH), 0)
    j_idx = lax.broadcasted_iota(jnp.int32, (CHUNK, GROUP_WIDTH), 1) % HEAD_DIM
    strict_lower = j_idx < t_idx
    lower = j_idx <= t_idx
    eye = (j_idx == t_idx).astype(F32)
    lane_head = lax.broadcasted_iota(jnp.int32, (1, GROUP_WIDTH), 1) // HEAD_DIM

    def bd(x):
        return _block_diag(x.astype(BF16), diag_mask)

    def diag_blocks(full):
        out = full[(GROUP_HEADS - 1) * HEAD_DIM:, :]
        for h in range(GROUP_HEADS - 2, -1, -1):
            out = jnp.where(lane_head == h, full[h * HEAD_DIM:(h + 1) * HEAD_DIM, :], out)
        return out

    n_groups = c_w // GROUP_WIDTH
    items = []
    for c in range(tile // CHUNK):
        rs = slice(c * CHUNK, (c + 1) * CHUNK)
        end_decay = edec_ref[c]
        for gi in range(n_groups):
            ls = slice(gi * GROUP_WIDTH, (gi + 1) * GROUP_WIDTH)
            operand = lambda n: rw_ref[rs, n * c_w + gi * GROUP_WIDTH:n * c_w + (gi + 1) * GROUP_WIDTH]
            it = dict(rs=rs, ls=ls, gi=gi)
            it["a_t"] = operand(0)
            it["r_t"] = operand(1)
            it["bk_e"] = jnp.concatenate([operand(4), operand(5)], axis=0)
            it["v_c"] = operand(6)
            it["m_diag"] = eye * end_decay[:, ls]
            ar = jnp.concatenate([it["a_t"], it["r_t"]], axis=0)
            pb = _dot_nt(ar, _block_diag(operand(2), diag_mask))
            pk = _dot_nt(ar, _block_diag(operand(3), diag_mask))
            it["pw"] = jnp.where(strict_lower, pb[0:CHUNK], 0.0)
            it["t_inv"] = eye + it["pw"]
            it["a_rb"] = jnp.where(lower, pb[CHUNK:], 0.0).astype(BF16)
            it["a_k"] = jnp.concatenate([jnp.where(strict_lower, pk[0:CHUNK], 0.0),
                                         jnp.where(lower, pk[CHUNK:], 0.0)], axis=0).astype(BF16)
            items.append(it)

    def affine_maps(its):
        steps = CHUNK.bit_length() - 1
        for s in range(1, steps + 1):
            for it in its:
                w_p = bd(it["pw"])
                if s == 1:
                    it["pw"] = _dot(it["pw"].astype(BF16), w_p)
                elif s < steps:
                    both = _dot(jnp.concatenate([it["pw"], it["t_inv"]], axis=0).astype(BF16), w_p)
                    it["t_inv"] = it["t_inv"] + both[CHUNK:]
                    it["pw"] = both[0:CHUNK]
                else:
                    it["t_inv"] = (it["t_inv"] + _dot(it["t_inv"].astype(BF16), w_p)).astype(BF16)
            yield
        for it in its:
            kv = _dot(it["a_k"], bd(it["v_c"]))
            it["akv"] = kv[0:CHUNK]
            it["arkv"] = kv[CHUNK:]
        yield
        for it in its:
            it["w_t"] = _dot(it["t_inv"], bd(it["a_t"])).astype(BF16)
            it["u_t"] = _dot(it["t_inv"], bd(it["akv"])).astype(BF16)
        yield
        for it in its:
            it["r_hat"] = (it["r_t"].astype(F32) + _dot(it["a_rb"], bd(it["w_t"]))).astype(BF16)
            it["y_hat"] = _dot(it["a_rb"], bd(it["u_t"])) + it["arkv"]
        yield
        for it in its:
            m_full = _dot_tn(it["bk_e"][0:CHUNK], it["w_t"])
            it["m"] = (it["m_diag"] + diag_blocks(m_full)).astype(BF16)
            n_full = _dot_tn(it["bk_e"], jnp.concatenate([it["u_t"], it["v_c"]], axis=0))
            it["n"] = diag_blocks(n_full)
        yield

    in_first = pl.program_id(1) < nb // 2
    per = yat1_ref.shape[1]
    n_chunks = tile // CHUNK
    states = [state_ref[gi] for gi in range(n_groups)]

    def advance_chunk(chunk_items):
        for it in chunk_items:
            gi = it["gi"]
            res = _dot(jnp.concatenate([it["m"], it["r_hat"]], axis=0), bd(states[gi]))
            y_ref[it["rs"], it["ls"]] = res[CHUNK:] + it["y_hat"]
            states[gi] = res[0:CHUNK] + it["n"]

    def output_rows(blk):
        rows = slice(blk * MOBA_BLOCK, (blk + 1) * MOBA_BLOCK)
        y = y_ref[rows, :]
        mean = _head_sum(y) * (1.0 / HEAD_DIM)
        d = y - mean
        var = _head_sum(d * d) * (1.0 / HEAD_DIM)
        yn = d * lax.rsqrt(var + GN_EPS) * gng_ref[...] + gnb_ref[...]
        bonus = rw_ref[rows, (RW_OPERANDS - 1) * c_w:].astype(F32)
        yb = ((yn + bonus) * gb_ref[rows, :].astype(F32)).astype(BF16)
        yat_t = jnp.transpose(jnp.where(in_first, yat1_ref[0, blk], yat2_ref[0, per - 1 - blk]))
        pieces = []
        for c0 in range(0, woa_ref.shape[1], GROUP_WIDTH):
            cols = slice(c0, c0 + GROUP_WIDTH)
            pieces.append(x_ref[rows, cols] + _dot(yat_t, woa_ref[:, cols]) + _dot(yb, wob_ref[:, cols]))
            yield
        h = jnp.concatenate(pieces, axis=-1)
        ms = jnp.mean(h * h, axis=-1, keepdims=True)
        out_ref[rows, :] = (h * lax.rsqrt(ms + RMS_EPS) * gain_ref[...]).astype(out_ref.dtype)

    assert per == 2 and n_chunks % 2 == 0
    half = (n_chunks // 2) * n_groups
    first, second = items[:half], items[half:]
    for _ in affine_maps(first):
        pass
    first_chunks = [first[c:c + n_groups] for c in range(0, len(first), n_groups)]
    for _ in affine_maps(second):
        if first_chunks:
            advance_chunk(first_chunks.pop(0))
    for chunk_items in first_chunks:
        advance_chunk(chunk_items)
    first_out = output_rows(0)
    for c in range(0, len(second), n_groups):
        advance_chunk(second[c:c + n_groups])
        next(first_out, None)
    for _ in first_out:
        pass
    for gi in range(n_groups):
        state_ref[gi] = states[gi]
    for _ in output_rows(1):
        pass


def _rwkv(rw, edec, gb, gn_gain, gn_bias, yat_first, yat_second, x2d, woa, wob, out_gain, batch, seq):
    tile, per = ROW_TILE, BLOCKS_PER_TILE
    nt = seq // tile
    assert nt % 2 == 0
    half = nt // 2
    d_model = x2d.shape[1]
    row_spec = lambda width: pl.BlockSpec((tile, width), lambda b, i: (b * nt + i, 0))
    const = lambda shape: pl.BlockSpec(shape, lambda b, i: (0,) * len(shape))
    vec = lambda width: const((1, width))
    blk_t = lambda pick: pl.BlockSpec((1, per, ATTN_WIDTH, MOBA_BLOCK),
                                      lambda b, i: (b, jnp.minimum(pick(i), half - 1), 0, 0))
    return pl.pallas_call(
        functools.partial(_rwkv_kernel, nb=nt),
        grid=(batch, nt),
        in_specs=[
            row_spec(RW_OPERANDS * RWKV_WIDTH),
            pl.BlockSpec((tile // CHUNK, 1, RWKV_WIDTH), lambda b, i: (b * nt + i, 0, 0)),
            row_spec(RWKV_WIDTH),
            vec(RWKV_WIDTH),
            vec(RWKV_WIDTH),
            blk_t(lambda i: i),
            blk_t(lambda i: nt - 1 - i),
            row_spec(d_model),
            const((ATTN_WIDTH, d_model)),
            const((RWKV_WIDTH, d_model)),
            vec(d_model),
        ],
        out_specs=row_spec(d_model),
        out_shape=jax.ShapeDtypeStruct(x2d.shape, x2d.dtype),
        scratch_shapes=[
            pltpu.VMEM((RWKV_WIDTH // GROUP_WIDTH, HEAD_DIM, GROUP_WIDTH), F32),
            pltpu.VMEM((tile, RWKV_WIDTH), F32),
        ],
        compiler_params=pltpu.CompilerParams(
            dimension_semantics=("arbitrary", "arbitrary"), vmem_limit_bytes=VMEM_LIMIT_BYTES),
        name="rwkv_out",
    )(rw, edec, gb, gn_gain, gn_bias, yat_first, yat_second, x2d, woa, wob, out_gain)


def _rotary_column_order():
    order = []
    for g in range(ATTN_WIDTH // GROUP_WIDTH):
        for part in range(2):
            for hh in range(GROUP_HEADS):
                head = g * GROUP_HEADS + hh
                order.extend(head * HEAD_DIM + part * HALF + j for j in range(HALF))
    return np.asarray(order, np.int32)


def _layer(h2d, batch, seq, norm_gain, w_in, shift_mu, decay_bias, decay_up, iclr_bias, iclr_up,
           k_k, k_a, r_k, gn_gain, gn_bias, w_out, out_gain, cos, sin):
    nb = seq // MOBA_BLOCK
    aw, rw = ATTN_WIDTH, RWKV_WIDTH
    order = _rotary_column_order()
    b0 = 4 * aw
    w_in = w_in.astype(BF16)
    wnn = jnp.concatenate([w_in[:, aw:2 * aw][:, order], w_in[:, b0:]], axis=1)
    wnt = jnp.concatenate([w_in[:, 0:aw][:, order], w_in[:, 2 * aw:4 * aw]], axis=1).T
    zeros = jnp.zeros((LORA_RANK, rw), F32)
    wl = jnp.concatenate([jnp.concatenate([decay_up.astype(F32), zeros], axis=1),
                          jnp.concatenate([zeros, iclr_up.astype(F32)], axis=1)], axis=0)
    wlora = wl.astype(BF16)
    row = lambda t: t.astype(F32).reshape(1, -1)

    cos_t = cos.reshape(nb, MOBA_BLOCK, 128).transpose(0, 2, 1)
    sin_t = sin.reshape(nb, MOBA_BLOCK, 128).transpose(0, 2, 1)
    lane = np.arange(aw)
    head_of_lane = (lane // GROUP_WIDTH) * GROUP_HEADS + (lane % 128) // HALF
    hsel = jnp.asarray(head_of_lane[:, None] == np.arange(128)[None, :], BF16)
    rw_vecs = [row(decay_bias), row(iclr_bias), row(k_k), row(k_a), row(r_k)]
    qt, k, kmean, kn2, vt, gat, rw, edec, gb = _in_proj(h2d, row(norm_gain), wnn, wnt, cos, sin, cos_t, sin_t,
                                                        row(shift_mu), hsel, wlora, rw_vecs, batch, nb)
    yat_first, yat_second = _moba(qt, k, kmean, kn2, vt, gat, batch, seq)
    w_out = w_out.astype(BF16)
    return _rwkv(rw, edec, gb, row(gn_gain), row(gn_bias), yat_first, yat_second, h2d, w_out[0:aw], w_out[aw:],
                 out_gain, batch, seq)


def kernel(x, norm_gain, w_in, shift_mu, decay_bias, decay_up, iclr_bias, iclr_up,
           k_k, k_a, r_k, gn_gain, gn_bias, w_out, final_gain):
    batch, seq, d_model = x.shape
    depth = norm_gain.shape[0]
    assert depth == 1, "the final RMSNorm is fused into the single layer's output projection"
    assert seq % MOBA_BLOCK == 0 and w_in.shape[-1] == 4 * ATTN_WIDTH + SHIFT_WIDTH + RWKV_WIDTH
    inv_freq = 1.0 / (ROPE_THETA ** (jnp.arange(0, HEAD_DIM, 2, dtype=F32) / HEAD_DIM))
    ang = jnp.arange(seq, dtype=F32)[:, None] * inv_freq[None, :]
    cos = jnp.tile(jnp.cos(ang), (1, GROUP_HEADS))
    sin = jnp.tile(jnp.sin(ang), (1, GROUP_HEADS))
    out = _layer(x.reshape(batch * seq, d_model).astype(F32), batch, seq, norm_gain[0], w_in[0],
                 shift_mu[0], decay_bias[0], decay_up[0], iclr_bias[0], iclr_up[0], k_k[0], k_a[0], r_k[0],
                 gn_gain[0], gn_bias[0], w_out[0], final_gain.astype(F32).reshape(1, -1), cos, sin)
    return out.reshape(batch, seq, d_model).astype(x.dtype)
```

```python
import functools

import jax
import jax.numpy as jnp
import numpy as np
from jax import lax
from jax.experimental import pallas as pl
from jax.experimental.pallas import tpu as pltpu

HEAD_DIM = 64
HALF = HEAD_DIM // 2
ATTN_HEADS = 8
ATTN_WIDTH = ATTN_HEADS * HEAD_DIM
RWKV_HEADS = 8
RWKV_WIDTH = RWKV_HEADS * HEAD_DIM
LORA_RANK = 64
SHIFT_WIDTH = 3 * RWKV_WIDTH + 2 * LORA_RANK
MOBA_BLOCK = 256
MOBA_TOPK = 3
ROPE_THETA = 10000.0
RMS_EPS = 1e-6
GN_EPS = 64e-5
NORMALIZE_EPS_SQ = 1e-24

GROUP_HEADS = 4
GROUP_WIDTH = GROUP_HEADS * HEAD_DIM
CHUNK = 64
ROW_TILE = 2 * MOBA_BLOCK
BLOCKS_PER_TILE = ROW_TILE // MOBA_BLOCK
RW_OPERANDS = 8
MASKED = -1e30
DECAY_SCALE = -float(np.exp(-0.5))
SCORE_BOUND_MARGIN = 1.0 + 2.0 ** -6
SCORE_BOUND_SAFE = 56.0
STAT_ROWS = 8
ACC_ROWS = HEAD_DIM + 16
LOG2E = 1.4426950408889634
Q_SCALE = HEAD_DIM ** -0.5 * LOG2E
VMEM_LIMIT_BYTES = 48 * 1024 * 1024

F32 = jnp.float32
BF16 = jnp.bfloat16


def _dot(a, b):
    return jnp.dot(a, b, preferred_element_type=F32)


def _dot_nt(a, b):
    return lax.dot_general(a, b, (((1,), (1,)), ((), ())), preferred_element_type=F32)


def _dot_tn(a, b):
    return lax.dot_general(a, b, (((0,), (0,)), ((), ())), preferred_element_type=F32)


def _split_bf16(x):
    hi = x.astype(BF16)
    lo = (x - hi.astype(F32)).astype(BF16)
    return hi, lo


def _silu(z):
    return z / (1.0 + jnp.exp(-z))


def _rotary(acc, cos, sin):
    outs = []
    for g in range(ATTN_WIDTH // GROUP_WIDTH):
        x1 = acc[:, g * GROUP_WIDTH:g * GROUP_WIDTH + 128]
        x2 = acc[:, g * GROUP_WIDTH + 128:(g + 1) * GROUP_WIDTH]
        outs.append(x1 * cos - x2 * sin)
        outs.append(x2 * cos + x1 * sin)
    return jnp.concatenate(outs, axis=-1)


def _rotary_t(acc, cos_t, sin_t):
    outs = []
    for g in range(ATTN_WIDTH // GROUP_WIDTH):
        x1 = acc[g * GROUP_WIDTH:g * GROUP_WIDTH + 128, :]
        x2 = acc[g * GROUP_WIDTH + 128:(g + 1) * GROUP_WIDTH, :]
        outs.append(x1 * cos_t - x2 * sin_t)
        outs.append(x2 * cos_t + x1 * sin_t)
    return jnp.concatenate(outs, axis=0)


def _head_sum(t):
    gr = lax.broadcasted_iota(jnp.int32, (GROUP_WIDTH, GROUP_WIDTH), 0) // HEAD_DIM
    gc = lax.broadcasted_iota(jnp.int32, (GROUP_WIDTH, GROUP_WIDTH), 1) // HEAD_DIM
    head_ones = jnp.where(gr == gc, 1.0, 0.0).astype(BF16)
    t = t.astype(BF16)
    return jnp.concatenate([_dot(t[:, i:i + GROUP_WIDTH], head_ones)
                            for i in range(0, t.shape[1], GROUP_WIDTH)], axis=-1)


def _in_proj_block(s, x_ref, gain_ref, wnn_ref, wnt_ref, cos_ref, sin_ref, cost_ref, sint_ref, mu_ref, hsel_ref,
                   wlora_ref, dbias_ref, ibias_ref, kk_ref, ka_ref, rk_ref,
                   qt_ref, k_ref, kmean_ref, kn2_ref, vt_ref, gat_ref, rw_ref, edec_ref, gb_ref, prev_ref):
    tile = MOBA_BLOCK
    rows = slice(s * tile, (s + 1) * tile)
    x = x_ref[rows, :]
    ms = jnp.mean(x * x, axis=-1, keepdims=True)
    u = (x * lax.rsqrt(ms + RMS_EPS) * gain_ref[...]).astype(BF16)

    k_att = _rotary(_dot(u, wnn_ref[:, 0:ATTN_WIDTH]), cos_ref[rows, :], sin_ref[rows, :])
    k_bf = k_att.astype(BF16)
    for g in range(ATTN_WIDTH // GROUP_WIDTH):
        k_ref[g, rows, :] = k_bf[:, g * GROUP_WIDTH:(g + 1) * GROUP_WIDTH]
    kmean_ref[s] = jnp.mean(k_att, axis=0, keepdims=True)
    k_sq = k_bf.astype(F32)
    k_sq = (k_sq * k_sq).astype(BF16)

    base = ATTN_WIDTH
    c_w = RWKV_WIDTH
    row = lax.broadcasted_iota(jnp.int32, (tile, 1), 0)

    def shifted_cols(c, w):
        p = _dot(u, wnn_ref[:, base + c:base + c + w])
        shifted = jnp.where(row == 0, prev_ref[0:1, c:c + w], pltpu.roll(p, 1, axis=0))
        prev_ref[0:1, c:c + w] = p[tile - 1:tile, :]
        return p + (shifted - p) * mu_ref[:, c:c + w]

    lora_in = shifted_cols(3 * c_w, 2 * LORA_RANK)
    lane128 = lax.broadcasted_iota(jnp.int32, (1, 2 * LORA_RANK), 1)
    lora_in = jnp.where(lane128 < LORA_RANK, jnp.tanh(lora_in), lora_in)
    li_hi, li_lo = _split_bf16(lora_in)
    k = shifted_cols(c_w, c_w)
    kk = k * kk_ref[...]
    r = shifted_cols(0, c_w)
    v = shifted_cols(2 * c_w, c_w)

    kn2_ref[s] = jnp.max(_dot(k_sq, hsel_ref[...]), axis=0, keepdims=True)
    w_hi = wlora_ref[...]
    lora = _dot(li_hi, w_hi) + _dot(li_lo, w_hi)
    kk = kk * lax.rsqrt(jnp.maximum(_head_sum(kk * kk), NORMALIZE_EPS_SQ))

    gb = _dot(u, wnn_ref[:, base + SHIFT_WIDTH:base + SHIFT_WIDTH + RWKV_WIDTH])
    gb_ref[rows, :] = _silu(gb).astype(BF16)

    g = DECAY_SCALE / (1.0 + jnp.exp(-(dbias_ref[...] + lora[:, 0:c_w])))
    a = 1.0 / (1.0 + jnp.exp(-(ibias_ref[...] + lora[:, c_w:2 * c_w])))
    g_hi, g_lo = _split_bf16(g)
    k2 = k * (1.0 + (a - 1.0) * ka_ref[...])
    b = kk * a

    qt = _rotary_t(_dot_nt(wnt_ref[0:ATTN_WIDTH, :], u), cost_ref[s], sint_ref[s])
    qt_ref[0, s] = (qt * Q_SCALE).astype(BF16)

    ti = lax.broadcasted_iota(jnp.int32, (CHUNK, CHUNK), 0)
    tj = lax.broadcasted_iota(jnp.int32, (CHUNK, CHUNK), 1)
    tri_incl = jnp.where(tj <= ti, 1.0, 0.0).astype(BF16)
    cums = [_dot(tri_incl, g_hi[c:c + CHUNK]) + _dot(tri_incl, g_lo[c:c + CHUNK])
            for c in range(0, tile, CHUNK)]
    bonus = _head_sum(r * k2 * rk_ref[...]) * v

    for i, cum in enumerate(cums):
        rs = slice(i * CHUNK, (i + 1) * CHUNK)
        dec_in = jnp.exp(cum)
        dec_ex = jnp.exp(cum - g[rs])
        inv = 1.0 / dec_in
        end_decay = dec_in[CHUNK - 1:CHUNK, :]
        to_end = inv * end_decay
        edec_ref[s * (tile // CHUNK) + i] = end_decay
        cols = (-kk[rs] * dec_ex, r[rs] * dec_in, b[rs] * inv, k2[rs] * inv,
                b[rs] * to_end, k2[rs] * to_end, v[rs], bonus[rs])
        for n, col in enumerate(cols):
            rw_ref[s * tile + i * CHUNK:s * tile + (i + 1) * CHUNK, n * c_w:(n + 1) * c_w] = col.astype(BF16)

    vt_ref[0, s] = _dot_nt(wnt_ref[ATTN_WIDTH:2 * ATTN_WIDTH, :], u).astype(BF16)
    gat = _dot_nt(wnt_ref[2 * ATTN_WIDTH:3 * ATTN_WIDTH, :], u)
    gat_ref[0, s] = _silu(gat).astype(BF16)


def _in_proj_kernel(x_ref, gain_ref, wnn_ref, wnt_ref, cos_ref, sin_ref, cost_ref, sint_ref, mu_ref, hsel_ref,
                    wlora_ref, dbias_ref, ibias_ref, kk_ref, ka_ref, rk_ref,
                    qt_ref, k_ref, kmean_ref, kn2_ref, vt_ref, gat_ref, rw_ref, edec_ref, gb_ref, prev_ref,
                    *, nb):
    @pl.when(pl.program_id(0) % (nb // BLOCKS_PER_TILE) == 0)
    def _():
        prev_ref[...] = jnp.zeros_like(prev_ref)

    for s in range(BLOCKS_PER_TILE):
        _in_proj_block(s, x_ref, gain_ref, wnn_ref, wnt_ref, cos_ref, sin_ref, cost_ref, sint_ref, mu_ref,
                       hsel_ref, wlora_ref, dbias_ref, ibias_ref, kk_ref, ka_ref, rk_ref,
                       qt_ref, k_ref, kmean_ref, kn2_ref, vt_ref, gat_ref, rw_ref, edec_ref, gb_ref, prev_ref)


def _in_proj(x2d, gain, wnn, wnt, cos, sin, cos_t, sin_t, mu, hsel, wlora, rw_vecs, batch, nb):
    rows, d_model = x2d.shape
    tile, per = ROW_TILE, BLOCKS_PER_TILE
    assert nb % per == 0
    nt = nb // per
    row_spec = lambda width: pl.BlockSpec((tile, width), lambda i: (i, 0))
    const = lambda shape: pl.BlockSpec(shape, lambda i: (0,) * len(shape))
    blk_t = pl.BlockSpec((1, per, ATTN_WIDTH, MOBA_BLOCK), lambda i: (i // nt, i % nt, 0, 0))
    blk_t_shape = jax.ShapeDtypeStruct((batch, nb, ATTN_WIDTH, MOBA_BLOCK), BF16)
    return pl.pallas_call(
        functools.partial(_in_proj_kernel, nb=nb),
        grid=(rows // tile,),
        in_specs=[
            row_spec(d_model),
            const((1, d_model)),
            const(wnn.shape),
            const(wnt.shape),
            pl.BlockSpec((tile, 128), lambda i: (i % nt, 0)),
            pl.BlockSpec((tile, 128), lambda i: (i % nt, 0)),
            pl.BlockSpec((per, 128, MOBA_BLOCK), lambda i: (i % nt, 0, 0)),
            pl.BlockSpec((per, 128, MOBA_BLOCK), lambda i: (i % nt, 0, 0)),
            const((1, SHIFT_WIDTH)),
            const(hsel.shape),
            const(wlora.shape),
        ] + [const((1, RWKV_WIDTH))] * len(rw_vecs),
        out_specs=[
            blk_t,
            pl.BlockSpec((ATTN_WIDTH // GROUP_WIDTH, tile, GROUP_WIDTH), lambda i: (0, i, 0)),
            pl.BlockSpec((per, 1, ATTN_WIDTH), lambda i: (i, 0, 0)),
            pl.BlockSpec((per, 1, 128), lambda i: (i, 0, 0)),
            blk_t,
            blk_t,
            row_spec(RW_OPERANDS * RWKV_WIDTH),
            pl.BlockSpec((tile // CHUNK, 1, RWKV_WIDTH), lambda i: (i, 0, 0)),
            row_spec(RWKV_WIDTH),
        ],
        out_shape=[
            blk_t_shape,
            jax.ShapeDtypeStruct((ATTN_WIDTH // GROUP_WIDTH, rows, GROUP_WIDTH), BF16),
            jax.ShapeDtypeStruct((rows // MOBA_BLOCK, 1, ATTN_WIDTH), F32),
            jax.ShapeDtypeStruct((rows // MOBA_BLOCK, 1, 128), F32),
            blk_t_shape,
            blk_t_shape,
            jax.ShapeDtypeStruct((rows, RW_OPERANDS * RWKV_WIDTH), BF16),
            jax.ShapeDtypeStruct((rows // CHUNK, 1, RWKV_WIDTH), F32),
            jax.ShapeDtypeStruct((rows, RWKV_WIDTH), BF16),
        ],
        scratch_shapes=[pltpu.VMEM((8, SHIFT_WIDTH), F32)],
        compiler_params=pltpu.CompilerParams(
            dimension_semantics=("arbitrary",), vmem_limit_bytes=VMEM_LIMIT_BYTES),
        name="in_proj",
    )(x2d, gain, wnn, wnt, cos, sin, cos_t, sin_t, mu, hsel, wlora, *rw_vecs)


def _moba_kernel(qta_ref, qtb_ref, k_ref, kmean_ref, kn2_ref, vt_ref, gata_ref, gatb_ref, outa_ref, outb_ref,
                 qh_ref, bias_ref, m_ref, acc_ref, *, nb):
    j = pl.program_id(2)
    blk = MOBA_BLOCK
    tiles = ((qta_ref, gata_ref, outa_ref, j), (qtb_ref, gatb_ref, outb_ref, nb - 1 - j))
    feat = lax.broadcasted_iota(jnp.int32, (GROUP_WIDTH, 1), 0)
    head_of_feat = (feat % 128) // HALF
    ones_rows = jnp.ones((ACC_ROWS - HEAD_DIM, blk), BF16)
    km = kmean_ref[0]
    km_lane_head = (lax.broadcasted_iota(jnp.int32, (1, GROUP_WIDTH), 1) % 128) // HALF
    km_heads = jnp.concatenate([jnp.where(km_lane_head == h, km, 0.0) for h in range(GROUP_HEADS)], axis=0)
    km_hi, km_lo = _split_bf16(km_heads)
    blk_id = lax.broadcasted_iota(jnp.int32, (GROUP_HEADS, nb, blk), 1)
    k_pos = lax.broadcasted_iota(jnp.int32, (blk, blk), 0)
    q_pos = lax.broadcasted_iota(jnp.int32, (blk, blk), 1)
    causal = k_pos <= q_pos
    kn2 = kn2_ref[0]
    kn2_row = lax.broadcasted_iota(jnp.int32, kn2.shape, 0)
    kn2_lane = lax.broadcasted_iota(jnp.int32, (1, kn2.shape[1]), 1)
    first_head = pl.program_id(1) * GROUP_HEADS

    def values_ext(ki, h):
        return jnp.concatenate([vt_ref[0, ki, h * HEAD_DIM:(h + 1) * HEAD_DIM, :], ones_rows], axis=0)

    bounds = []
    for t, (qt_ref, _, _, qi) in enumerate(tiles):
        qt = qt_ref[0, 0]
        q_sq = qt.astype(F32)
        q_sq = q_sq * q_sq
        kn2_seen = jnp.max(jnp.where(kn2_row <= qi, kn2, 0.0), axis=0, keepdims=True)
        for h in range(GROUP_HEADS):
            qn2 = (jnp.sum(q_sq[h * HALF:(h + 1) * HALF], axis=0, keepdims=True)
                   + jnp.sum(q_sq[128 + h * HALF:128 + (h + 1) * HALF], axis=0, keepdims=True))
            kn2_h = jnp.sum(jnp.where(kn2_lane == first_head + h, kn2_seen, 0.0), axis=1, keepdims=True)
            bounds.append(jnp.sqrt(kn2_h * qn2) * SCORE_BOUND_MARGIN)
            qh_ref[t, h] = jnp.where(head_of_feat == h, qt, jnp.zeros_like(qt))
        gate = (_dot(km_hi, qt) + _dot(km_lo, qt)).reshape(GROUP_HEADS, nb, blk)
        gate = jnp.where(blk_id < qi, gate, -jnp.inf)
        bias = jnp.full(gate.shape, MASKED, F32)
        for r in range(min(MOBA_TOPK, nb)):
            top = jnp.max(gate, axis=1, keepdims=True)
            idx = jnp.min(jnp.where(gate == top, blk_id, nb), axis=1, keepdims=True)
            pick = blk_id == idx + jnp.where(r < qi, 0, nb)
            bias = jnp.where(pick, 0.0, bias)
            gate = jnp.where(pick, -jnp.inf, gate)
        bias_ref[t] = bias

    def issue_scores(t, ki):
        kb = k_ref[0, 0, pl.ds(pl.multiple_of(ki * blk, blk), blk), :]
        return [_dot(kb, qh_ref[t, h]) for h in range(GROUP_HEADS)]

    def finish_own(t, ki, scores):
        for h in range(GROUP_HEADS):
            s = jnp.where(causal, scores[h], MASKED)
            m = jnp.max(s, axis=0, keepdims=True)
            p = jnp.exp2(s - m).astype(BF16)
            m_ref[t, h] = jnp.broadcast_to(m, (STAT_ROWS, blk))
            acc_ref[t, h] = _dot(values_ext(ki, h), p)

    def finish_past(t, ki, scores):
        for h in range(GROUP_HEADS):
            s = scores[h]
            bias = bias_ref[t, h, pl.ds(ki, 1), :]
            m_old = m_ref[t, h, 0:1, :]
            m_new = jnp.maximum(m_old, jnp.max(s, axis=0, keepdims=True) + bias)
            alpha = jnp.exp2(m_old - m_new)
            p = jnp.exp2(s - (m_new - bias)).astype(BF16)
            m_ref[t, h] = jnp.broadcast_to(m_new, (STAT_ROWS, blk))
            acc_ref[t, h] = alpha * acc_ref[t, h] + _dot(values_ext(ki, h), p)

    def bounded_own(t, ki, scores):
        for h in range(GROUP_HEADS):
            s = jnp.where(causal, scores[h], MASKED)
            p = jnp.exp2(s - m_ref[t, h, 0:1, :]).astype(BF16)
            acc_ref[t, h] = _dot(values_ext(ki, h), p)

    def bounded_past(t, ki, scores):
        for h in range(GROUP_HEADS):
            bias = bias_ref[t, h, pl.ds(ki, 1), :]
            p = jnp.exp2(scores[h] - (m_ref[t, h, 0:1, :] - bias)).astype(BF16)
            acc_ref[t, h] = acc_ref[t, h] + _dot(values_ext(ki, h), p)

    def attend(own, past):
        work = [(own, 0, j), (own, 1, nb - 1 - j)]
        for n in range(nb - 1):
            second = (n >= j).astype(jnp.int32)
            work.append((past, second, n - second * j))
        pending = None
        for fin, t, ki in work:
            scores = issue_scores(t, ki)
            if pending is not None:
                pending[0](*pending[1:])
            pending = (fin, t, ki, scores)
        pending[0](*pending[1:])

    bound_ok = jnp.max(jnp.concatenate(bounds, axis=0)) < SCORE_BOUND_SAFE

    @pl.when(bound_ok)
    def _():
        for i, bound in enumerate(bounds):
            m_ref[i // GROUP_HEADS, i % GROUP_HEADS] = jnp.broadcast_to(bound, (STAT_ROWS, blk))
        attend(bounded_own, bounded_past)

    @pl.when(jnp.logical_not(bound_ok))
    def _():
        attend(finish_own, finish_past)

    for t, (_, gat_ref, out_ref, _) in enumerate(tiles):
        for h in range(GROUP_HEADS):
            rows = slice(h * HEAD_DIM, (h + 1) * HEAD_DIM)
            acc = acc_ref[t, h]
            y = acc[0:HEAD_DIM] / acc[HEAD_DIM:HEAD_DIM + 1]
            out_ref[0, 0, rows, :] = (y * gat_ref[0, 0, rows, :].astype(F32)).astype(BF16)


def _moba(qt, k, kmean, kn2, vt, gat, batch, seq):
    nb = seq // MOBA_BLOCK
    assert nb % 2 == 0
    blk = MOBA_BLOCK
    groups = ATTN_WIDTH // GROUP_WIDTH
    k4 = k.reshape(groups, batch, seq, GROUP_WIDTH)
    km3 = kmean.reshape(batch, nb, ATTN_WIDTH)
    kn3 = kn2.reshape(batch, nb, 128)
    tile_a = pl.BlockSpec((1, 1, GROUP_WIDTH, blk), lambda b, g, j: (b, j, g, 0))
    tile_b = pl.BlockSpec((1, 1, GROUP_WIDTH, blk), lambda b, g, j: (b, nb - 1 - j, g, 0))
    out_shape = jax.ShapeDtypeStruct((batch, nb // 2, ATTN_WIDTH, blk), BF16)
    return pl.pallas_call(
        functools.partial(_moba_kernel, nb=nb),
        grid=(batch, groups, nb // 2),
        in_specs=[
            tile_a,
            tile_b,
            pl.BlockSpec((1, 1, seq, GROUP_WIDTH), lambda b, g, j: (g, b, 0, 0)),
            pl.BlockSpec((1, nb, GROUP_WIDTH), lambda b, g, j: (b, 0, g)),
            pl.BlockSpec((1, nb, 128), lambda b, g, j: (b, 0, 0)),
            pl.BlockSpec((1, nb, GROUP_WIDTH, blk), lambda b, g, j: (b, 0, g, 0)),
            tile_a,
            tile_b,
        ],
        out_specs=[tile_a, tile_a],
        out_shape=[out_shape, out_shape],
        scratch_shapes=[
            pltpu.VMEM((2, GROUP_HEADS, GROUP_WIDTH, blk), BF16),
            pltpu.VMEM((2, GROUP_HEADS, nb, blk), F32),
            pltpu.VMEM((2, GROUP_HEADS, STAT_ROWS, blk), F32),
            pltpu.VMEM((2, GROUP_HEADS, ACC_ROWS, blk), F32),
        ],
        compiler_params=pltpu.CompilerParams(
            dimension_semantics=("arbitrary", "arbitrary", "arbitrary"), vmem_limit_bytes=VMEM_LIMIT_BYTES),
        name="moba",
    )(qt, qt, k4, km3, kn3, vt, gat, gat)


def _block_diag(x_bf16, mask):
    tiled = jnp.concatenate([x_bf16] * GROUP_HEADS, axis=0)
    return jnp.where(mask, tiled, jnp.zeros_like(tiled))


def _rwkv_kernel(rw_ref, edec_ref, gb_ref, gng_ref, gnb_ref, yat1_ref, yat2_ref, x_ref, woa_ref, wob_ref,
                 gain_ref, out_ref, state_ref, y_ref, *, nb):
    tile = rw_ref.shape[0]
    c_w = RWKV_WIDTH

    @pl.when(pl.program_id(1) == 0)
    def _():
        state_ref[...] = jnp.zeros_like(state_ref)

    gr = lax.broadcasted_iota(jnp.int32, (GROUP_WIDTH, GROUP_WIDTH), 0) // HEAD_DIM
    gc = lax.broadcasted_iota(jnp.int32, (GROUP_WIDTH, GROUP_WIDTH), 1) // HEAD_DIM
    diag_mask = gr == gc
    t_idx = lax.broadcasted_iota(jnp.int32, (CHUNK, GROUP_WIDTH), 0)
    j_idx = lax.broadcasted_iota(jnp.int32, (CHUNK, GROUP_WIDTH), 1) % HEAD_DIM
    strict_lower = j_idx < t_idx
    lower = j_idx <= t_idx
    eye = (j_idx == t_idx).astype(F32)

    def below_diag(size):
        return ((t_idx // (2 * size) == j_idx // (2 * size))
                & ((t_idx // size) % 2 == 1) & ((j_idx // size) % 2 == 0))
    lane_head = lax.broadcasted_iota(jnp.int32, (1, GROUP_WIDTH), 1) // HEAD_DIM

    def bd(x):
        return _block_diag(x.astype(BF16), diag_mask)

    def diag_blocks(full):
        out = full[(GROUP_HEADS - 1) * HEAD_DIM:, :]
        for h in range(GROUP_HEADS - 2, -1, -1):
            out = jnp.where(lane_head == h, full[h * HEAD_DIM:(h + 1) * HEAD_DIM, :], out)
        return out

    n_groups = c_w // GROUP_WIDTH
    items = []
    for c in range(tile // CHUNK):
        rs = slice(c * CHUNK, (c + 1) * CHUNK)
        end_decay = edec_ref[c]
        for gi in range(n_groups):
            ls = slice(gi * GROUP_WIDTH, (gi + 1) * GROUP_WIDTH)
            operand = lambda n: rw_ref[rs, n * c_w + gi * GROUP_WIDTH:n * c_w + (gi + 1) * GROUP_WIDTH]
            it = dict(rs=rs, ls=ls, gi=gi)
            it["a_t"] = operand(0)
            it["r_t"] = operand(1)
            it["bk_e"] = jnp.concatenate([operand(4), operand(5)], axis=0)
            it["v_c"] = operand(6)
            it["m_diag"] = eye * end_decay[:, ls]
            ar = jnp.concatenate([it["a_t"], it["r_t"]], axis=0)
            pb = _dot_nt(ar, _block_diag(operand(2), diag_mask))
            pk = _dot_nt(ar, _block_diag(operand(3), diag_mask))
            it["n_ab"] = jnp.where(strict_lower, pb[0:CHUNK], 0.0)
            it["t_inv"] = eye + jnp.where(below_diag(1), it["n_ab"], 0.0)
            it["a_rb"] = jnp.where(lower, pb[CHUNK:], 0.0).astype(BF16)
            it["a_k"] = jnp.concatenate([jnp.where(strict_lower, pk[0:CHUNK], 0.0),
                                         jnp.where(lower, pk[CHUNK:], 0.0)], axis=0).astype(BF16)
            items.append(it)

    def affine_maps(its):
        size = 2
        while size < CHUNK:
            for it in its:
                n21 = jnp.where(below_diag(size), it["n_ab"], 0.0)
                it["t_n21"] = _dot(it["t_inv"].astype(BF16), bd(n21))
            yield
            for it in its:
                it["t_inv"] = it["t_inv"] + _dot(it["t_n21"].astype(BF16), bd(it["t_inv"]))
            yield
            size *= 2
        for it in its:
            it["t_inv"] = it["t_inv"].astype(BF16)
        for it in its:
            kv = _dot(it["a_k"], bd(it["v_c"]))
            it["akv"] = kv[0:CHUNK]
            it["arkv"] = kv[CHUNK:]
        yield
        for it in its:
            it["w_t"] = _dot(it["t_inv"], bd(it["a_t"])).astype(BF16)
            it["u_t"] = _dot(it["t_inv"], bd(it["akv"])).astype(BF16)
        yield
        for it in its:
            it["r_hat"] = (it["r_t"].astype(F32) + _dot(it["a_rb"], bd(it["w_t"]))).astype(BF16)
            it["y_hat"] = _dot(it["a_rb"], bd(it["u_t"])) + it["arkv"]
        yield
        for it in its:
            m_full = _dot_tn(it["bk_e"][0:CHUNK], it["w_t"])
            it["m"] = (it["m_diag"] + diag_blocks(m_full)).astype(BF16)
            n_full = _dot_tn(it["bk_e"], jnp.concatenate([it["u_t"], it["v_c"]], axis=0))
            it["n"] = diag_blocks(n_full)
        yield

    in_first = pl.program_id(1) < nb // 2
    per = yat1_ref.shape[1]
    n_chunks = tile // CHUNK
    states = [state_ref[gi] for gi in range(n_groups)]

    def advance_chunk(chunk_items):
        for it in chunk_items:
            gi = it["gi"]
            res = _dot(jnp.concatenate([it["m"], it["r_hat"]], axis=0), bd(states[gi]))
            y_ref[it["rs"], it["ls"]] = res[CHUNK:] + it["y_hat"]
            states[gi] = res[0:CHUNK] + it["n"]

    def output_rows(blk):
        rows = slice(blk * MOBA_BLOCK, (blk + 1) * MOBA_BLOCK)
        y = y_ref[rows, :]
        mean = _head_sum(y) * (1.0 / HEAD_DIM)
        d = y - mean
        var = _head_sum(d * d) * (1.0 / HEAD_DIM)
        yn = d * lax.rsqrt(var + GN_EPS) * gng_ref[...] + gnb_ref[...]
        bonus = rw_ref[rows, (RW_OPERANDS - 1) * c_w:].astype(F32)
        yb = ((yn + bonus) * gb_ref[rows, :].astype(F32)).astype(BF16)
        yat_t = jnp.transpose(jnp.where(in_first, yat1_ref[0, blk], yat2_ref[0, per - 1 - blk]))
        pieces = []
        for c0 in range(0, woa_ref.shape[1], GROUP_WIDTH):
            cols = slice(c0, c0 + GROUP_WIDTH)
            pieces.append(x_ref[rows, cols] + _dot(yat_t, woa_ref[:, cols]) + _dot(yb, wob_ref[:, cols]))
            yield
        h = jnp.concatenate(pieces, axis=-1)
        ms = jnp.mean(h * h, axis=-1, keepdims=True)
        out_ref[rows, :] = (h * lax.rsqrt(ms + RMS_EPS) * gain_ref[...]).astype(out_ref.dtype)

    assert per == 2 and n_chunks % 2 == 0
    half = (n_chunks // 2) * n_groups
    first, second = items[:half], items[half:]
    for _ in affine_maps(first):
        pass
    first_chunks = [first[c:c + n_groups] for c in range(0, len(first), n_groups)]
    for _ in affine_maps(second):
        if first_chunks:
            advance_chunk(first_chunks.pop(0))
    for chunk_items in first_chunks:
        advance_chunk(chunk_items)
    first_out = output_rows(0)
    for c in range(0, len(second), n_groups):
        advance_chunk(second[c:c + n_groups])
        next(first_out, None)
    for _ in first_out:
        pass
    for gi in range(n_groups):
        state_ref[gi] = states[gi]
    for _ in output_rows(1):
        pass


def _rwkv(rw, edec, gb, gn_gain, gn_bias, yat_first, yat_second, x2d, woa, wob, out_gain, batch, seq):
    tile, per = ROW_TILE, BLOCKS_PER_TILE
    nt = seq // tile
    assert nt % 2 == 0
    half = nt // 2
    d_model = x2d.shape[1]
    row_spec = lambda width: pl.BlockSpec((tile, width), lambda b, i: (b * nt + i, 0))
    const = lambda shape: pl.BlockSpec(shape, lambda b, i: (0,) * len(shape))
    vec = lambda width: const((1, width))
    blk_t = lambda pick: pl.BlockSpec((1, per, ATTN_WIDTH, MOBA_BLOCK),
                                      lambda b, i: (b, jnp.minimum(pick(i), half - 1), 0, 0))
    return pl.pallas_call(
        functools.partial(_rwkv_kernel, nb=nt),
        grid=(batch, nt),
        in_specs=[
            row_spec(RW_OPERANDS * RWKV_WIDTH),
            pl.BlockSpec((tile // CHUNK, 1, RWKV_WIDTH), lambda b, i: (b * nt + i, 0, 0)),
            row_spec(RWKV_WIDTH),
            vec(RWKV_WIDTH),
            vec(RWKV_WIDTH),
            blk_t(lambda i: i),
            blk_t(lambda i: nt - 1 - i),
            row_spec(d_model),
            const((ATTN_WIDTH, d_model)),
            const((RWKV_WIDTH, d_model)),
            vec(d_model),
        ],
        out_specs=row_spec(d_model),
        out_shape=jax.ShapeDtypeStruct(x2d.shape, x2d.dtype),
        scratch_shapes=[
            pltpu.VMEM((RWKV_WIDTH // GROUP_WIDTH, HEAD_DIM, GROUP_WIDTH), F32),
            pltpu.VMEM((tile, RWKV_WIDTH), F32),
        ],
        compiler_params=pltpu.CompilerParams(
            dimension_semantics=("arbitrary", "arbitrary"), vmem_limit_bytes=VMEM_LIMIT_BYTES),
        name="rwkv_out",
    )(rw, edec, gb, gn_gain, gn_bias, yat_first, yat_second, x2d, woa, wob, out_gain)


def _rotary_column_order():
    order = []
    for g in range(ATTN_WIDTH // GROUP_WIDTH):
        for part in range(2):
            for hh in range(GROUP_HEADS):
                head = g * GROUP_HEADS + hh
                order.extend(head * HEAD_DIM + part * HALF + j for j in range(HALF))
    return np.asarray(order, np.int32)


def _layer(h2d, batch, seq, norm_gain, w_in, shift_mu, decay_bias, decay_up, iclr_bias, iclr_up,
           k_k, k_a, r_k, gn_gain, gn_bias, w_out, out_gain, cos, sin):
    nb = seq // MOBA_BLOCK
    aw, rw = ATTN_WIDTH, RWKV_WIDTH
    order = _rotary_column_order()
    b0 = 4 * aw
    w_in = w_in.astype(BF16)
    wnn = jnp.concatenate([w_in[:, aw:2 * aw][:, order], w_in[:, b0:]], axis=1)
    wnt = jnp.concatenate([w_in[:, 0:aw][:, order], w_in[:, 2 * aw:4 * aw]], axis=1).T
    zeros = jnp.zeros((LORA_RANK, rw), F32)
    wl = jnp.concatenate([jnp.concatenate([decay_up.astype(F32), zeros], axis=1),
                          jnp.concatenate([zeros, iclr_up.astype(F32)], axis=1)], axis=0)
    wlora = wl.astype(BF16)
    row = lambda t: t.astype(F32).reshape(1, -1)

    cos_t = cos.reshape(nb, MOBA_BLOCK, 128).transpose(0, 2, 1)
    sin_t = sin.reshape(nb, MOBA_BLOCK, 128).transpose(0, 2, 1)
    lane = np.arange(aw)
    head_of_lane = (lane // GROUP_WIDTH) * GROUP_HEADS + (lane % 128) // HALF
    hsel = jnp.asarray(head_of_lane[:, None] == np.arange(128)[None, :], BF16)
    rw_vecs = [row(decay_bias), row(iclr_bias), row(k_k), row(k_a), row(r_k)]
    qt, k, kmean, kn2, vt, gat, rw, edec, gb = _in_proj(h2d, row(norm_gain), wnn, wnt, cos, sin, cos_t, sin_t,
                                                        row(shift_mu), hsel, wlora, rw_vecs, batch, nb)
    yat_first, yat_second = _moba(qt, k, kmean, kn2, vt, gat, batch, seq)
    w_out = w_out.astype(BF16)
    return _rwkv(rw, edec, gb, row(gn_gain), row(gn_bias), yat_first, yat_second, h2d, w_out[0:aw], w_out[aw:],
                 out_gain, batch, seq)


def kernel(x, norm_gain, w_in, shift_mu, decay_bias, decay_up, iclr_bias, iclr_up,
           k_k, k_a, r_k, gn_gain, gn_bias, w_out, final_gain):
    batch, seq, d_model = x.shape
    depth = norm_gain.shape[0]
    assert depth == 1, "the final RMSNorm is fused into the single layer's output projection"
    assert seq % MOBA_BLOCK == 0 and w_in.shape[-1] == 4 * ATTN_WIDTH + SHIFT_WIDTH + RWKV_WIDTH
    inv_freq = 1.0 / (ROPE_THETA ** (jnp.arange(0, HEAD_DIM, 2, dtype=F32) / HEAD_DIM))
    ang = jnp.arange(seq, dtype=F32)[:, None] * inv_freq[None, :]
    cos = jnp.tile(jnp.cos(ang), (1, GROUP_HEADS))
    sin = jnp.tile(jnp.sin(ang), (1, GROUP_HEADS))
    out = _layer(x.reshape(batch * seq, d_model).astype(F32), batch, seq, norm_gain[0], w_in[0],
                 shift_mu[0], decay_bias[0], decay_up[0], iclr_bias[0], iclr_up[0], k_k[0], k_a[0], r_k[0],
                 gn_gain[0], gn_bias[0], w_out[0], final_gain.astype(F32).reshape(1, -1), cos, sin)
    return out.reshape(batch, seq, d_model).astype(x.dtype)
```

```python
import functools

import jax
import jax.numpy as jnp
import numpy as np
from jax import lax
from jax.experimental import pallas as pl
from jax.experimental.pallas import tpu as pltpu

HEAD_DIM = 64
HALF = HEAD_DIM // 2
ATTN_HEADS = 8
ATTN_WIDTH = ATTN_HEADS * HEAD_DIM
RWKV_HEADS = 8
RWKV_WIDTH = RWKV_HEADS * HEAD_DIM
LORA_RANK = 64
SHIFT_WIDTH = 3 * RWKV_WIDTH + 2 * LORA_RANK
MOBA_BLOCK = 256
MOBA_TOPK = 3
ROPE_THETA = 10000.0
RMS_EPS = 1e-6
GN_EPS = 64e-5
NORMALIZE_EPS_SQ = 1e-24

GROUP_HEADS = 4
GROUP_WIDTH = GROUP_HEADS * HEAD_DIM
CHUNK = 64
ROW_TILE = 2 * MOBA_BLOCK
BLOCKS_PER_TILE = ROW_TILE // MOBA_BLOCK
RW_OPERANDS = 8
MASKED = -1e30
DECAY_SCALE = -float(np.exp(-0.5))
SCORE_BOUND_MARGIN = 1.0 + 2.0 ** -6
SCORE_BOUND_SAFE = 56.0
STAT_ROWS = 8
ACC_ROWS = HEAD_DIM + 16
LOG2E = 1.4426950408889634
Q_SCALE = HEAD_DIM ** -0.5 * LOG2E
VMEM_LIMIT_BYTES = 48 * 1024 * 1024

F32 = jnp.float32
BF16 = jnp.bfloat16


def _dot(a, b):
    return jnp.dot(a, b, preferred_element_type=F32)


def _dot_nt(a, b):
    return lax.dot_general(a, b, (((1,), (1,)), ((), ())), preferred_element_type=F32)


def _split_bf16(x):
    hi = x.astype(BF16)
    lo = (x - hi.astype(F32)).astype(BF16)
    return hi, lo


def _silu(z):
    return z / (1.0 + jnp.exp(-z))


def _rotary(acc, cos, sin):
    outs = []
    for g in range(ATTN_WIDTH // GROUP_WIDTH):
        x1 = acc[:, g * GROUP_WIDTH:g * GROUP_WIDTH + 128]
        x2 = acc[:, g * GROUP_WIDTH + 128:(g + 1) * GROUP_WIDTH]
        outs.append(x1 * cos - x2 * sin)
        outs.append(x2 * cos + x1 * sin)
    return jnp.concatenate(outs, axis=-1)


def _rotary_t(acc, cos_t, sin_t):
    outs = []
    for g in range(ATTN_WIDTH // GROUP_WIDTH):
        x1 = acc[g * GROUP_WIDTH:g * GROUP_WIDTH + 128, :]
        x2 = acc[g * GROUP_WIDTH + 128:(g + 1) * GROUP_WIDTH, :]
        outs.append(x1 * cos_t - x2 * sin_t)
        outs.append(x2 * cos_t + x1 * sin_t)
    return jnp.concatenate(outs, axis=0)


def _head_sum(t):
    gr = lax.broadcasted_iota(jnp.int32, (GROUP_WIDTH, GROUP_WIDTH), 0) // HEAD_DIM
    gc = lax.broadcasted_iota(jnp.int32, (GROUP_WIDTH, GROUP_WIDTH), 1) // HEAD_DIM
    head_ones = jnp.where(gr == gc, 1.0, 0.0).astype(BF16)
    t = t.astype(BF16)
    return jnp.concatenate([_dot(t[:, i:i + GROUP_WIDTH], head_ones)
                            for i in range(0, t.shape[1], GROUP_WIDTH)], axis=-1)


def _in_proj_block(s, x_ref, gain_ref, wnn_ref, wnt_ref, cos_ref, sin_ref, cost_ref, sint_ref, mu_ref, hsel_ref,
                   wlora_ref, dbias_ref, ibias_ref, kk_ref, ka_ref, rk_ref,
                   qt_ref, k_ref, kmean_ref, kn2_ref, vt_ref, gat_ref, rw_ref, edec_ref, gb_ref, prev_ref):
    tile = MOBA_BLOCK
    rows = slice(s * tile, (s + 1) * tile)
    x = x_ref[rows, :]
    ms = jnp.mean(x * x, axis=-1, keepdims=True)
    u = (x * lax.rsqrt(ms + RMS_EPS) * gain_ref[...]).astype(BF16)

    k_att = _rotary(_dot(u, wnn_ref[:, 0:ATTN_WIDTH]), cos_ref[rows, :], sin_ref[rows, :])
    k_bf = k_att.astype(BF16)
    for g in range(ATTN_WIDTH // GROUP_WIDTH):
        k_ref[g, rows, :] = k_bf[:, g * GROUP_WIDTH:(g + 1) * GROUP_WIDTH]
    kmean_ref[s] = jnp.mean(k_att, axis=0, keepdims=True)
    k_sq = k_bf.astype(F32)
    k_sq = (k_sq * k_sq).astype(BF16)

    base = ATTN_WIDTH
    c_w = RWKV_WIDTH
    row = lax.broadcasted_iota(jnp.int32, (tile, 1), 0)

    def shifted_cols(c, w):
        p = _dot(u, wnn_ref[:, base + c:base + c + w])
        shifted = jnp.where(row == 0, prev_ref[0:1, c:c + w], pltpu.roll(p, 1, axis=0))
        prev_ref[0:1, c:c + w] = p[tile - 1:tile, :]
        return p + (shifted - p) * mu_ref[:, c:c + w]

    lora_in = shifted_cols(3 * c_w, 2 * LORA_RANK)
    lane128 = lax.broadcasted_iota(jnp.int32, (1, 2 * LORA_RANK), 1)
    lora_in = jnp.where(lane128 < LORA_RANK, jnp.tanh(lora_in), lora_in)
    li_hi, li_lo = _split_bf16(lora_in)
    k = shifted_cols(c_w, c_w)
    kk = k * kk_ref[...]
    r = shifted_cols(0, c_w)
    v = shifted_cols(2 * c_w, c_w)

    kn2_ref[s] = jnp.max(_dot(k_sq, hsel_ref[...]), axis=0, keepdims=True)
    w_hi = wlora_ref[...]
    lora = _dot(li_hi, w_hi) + _dot(li_lo, w_hi)
    kk = kk * lax.rsqrt(jnp.maximum(_head_sum(kk * kk), NORMALIZE_EPS_SQ))

    gb = _dot(u, wnn_ref[:, base + SHIFT_WIDTH:base + SHIFT_WIDTH + RWKV_WIDTH])
    gb_ref[rows, :] = _silu(gb).astype(BF16)

    g = DECAY_SCALE / (1.0 + jnp.exp(-(dbias_ref[...] + lora[:, 0:c_w])))
    a = 1.0 / (1.0 + jnp.exp(-(ibias_ref[...] + lora[:, c_w:2 * c_w])))
    g_hi, g_lo = _split_bf16(g)
    k2 = k * (1.0 + (a - 1.0) * ka_ref[...])
    b = kk * a

    qt = _rotary_t(_dot_nt(wnt_ref[0:ATTN_WIDTH, :], u), cost_ref[s], sint_ref[s])
    qt_ref[0, s] = (qt * Q_SCALE).astype(BF16)

    ti = lax.broadcasted_iota(jnp.int32, (CHUNK, CHUNK), 0)
    tj = lax.broadcasted_iota(jnp.int32, (CHUNK, CHUNK), 1)
    tri_incl = jnp.where(tj <= ti, 1.0, 0.0).astype(BF16)
    cums = [_dot(tri_incl, g_hi[c:c + CHUNK]) + _dot(tri_incl, g_lo[c:c + CHUNK])
            for c in range(0, tile, CHUNK)]
    bonus = _head_sum(r * k2 * rk_ref[...]) * v

    to_ends = []
    for i, cum in enumerate(cums):
        rs = slice(i * CHUNK, (i + 1) * CHUNK)
        out_rows = slice(s * tile + i * CHUNK, s * tile + (i + 1) * CHUNK)
        dec_in = jnp.exp(cum)
        dec_ex = jnp.exp(cum - g[rs])
        inv = 1.0 / dec_in
        end_decay = dec_in[CHUNK - 1:CHUNK, :]
        to_ends.append(inv * end_decay)
        edec_ref[s * (tile // CHUNK) + i] = end_decay
        cols = {0: -kk[rs] * dec_ex, 1: r[rs] * dec_in, 2: b[rs] * inv, 3: k2[rs] * inv, 6: v[rs], 7: bonus[rs]}
        for n, col in cols.items():
            rw_ref[out_rows, n * c_w:(n + 1) * c_w] = col.astype(BF16)
    to_end = jnp.concatenate(to_ends, axis=0)
    for n, src in ((4, b * to_end), (5, k2 * to_end)):
        for pair in range(0, tile // CHUNK, 2):
            src_t = jnp.transpose(src[pair * CHUNK:(pair + 2) * CHUNK, :])
            for i in (pair, pair + 1):
                lanes = slice((i - pair) * CHUNK, (i - pair + 1) * CHUNK)
                per_head = jnp.concatenate([src_t[h * HEAD_DIM:(h + 1) * HEAD_DIM, lanes]
                                            for h in range(RWKV_HEADS)], axis=1)
                rw_ref[s * tile + i * CHUNK:s * tile + (i + 1) * CHUNK, n * c_w:(n + 1) * c_w] = per_head.astype(BF16)

    vt_ref[0, s] = _dot_nt(wnt_ref[ATTN_WIDTH:2 * ATTN_WIDTH, :], u).astype(BF16)
    gat = _dot_nt(wnt_ref[2 * ATTN_WIDTH:3 * ATTN_WIDTH, :], u)
    gat_ref[0, s] = _silu(gat).astype(BF16)


def _in_proj_kernel(x_ref, gain_ref, wnn_ref, wnt_ref, cos_ref, sin_ref, cost_ref, sint_ref, mu_ref, hsel_ref,
                    wlora_ref, dbias_ref, ibias_ref, kk_ref, ka_ref, rk_ref,
                    qt_ref, k_ref, kmean_ref, kn2_ref, vt_ref, gat_ref, rw_ref, edec_ref, gb_ref, prev_ref,
                    *, nb):
    @pl.when(pl.program_id(0) % (nb // BLOCKS_PER_TILE) == 0)
    def _():
        prev_ref[...] = jnp.zeros_like(prev_ref)

    for s in range(BLOCKS_PER_TILE):
        _in_proj_block(s, x_ref, gain_ref, wnn_ref, wnt_ref, cos_ref, sin_ref, cost_ref, sint_ref, mu_ref,
                       hsel_ref, wlora_ref, dbias_ref, ibias_ref, kk_ref, ka_ref, rk_ref,
                       qt_ref, k_ref, kmean_ref, kn2_ref, vt_ref, gat_ref, rw_ref, edec_ref, gb_ref, prev_ref)


def _in_proj(x2d, gain, wnn, wnt, cos, sin, cos_t, sin_t, mu, hsel, wlora, rw_vecs, batch, nb):
    rows, d_model = x2d.shape
    tile, per = ROW_TILE, BLOCKS_PER_TILE
    assert nb % per == 0
    nt = nb // per
    row_spec = lambda width: pl.BlockSpec((tile, width), lambda i: (i, 0))
    const = lambda shape: pl.BlockSpec(shape, lambda i: (0,) * len(shape))
    blk_t = pl.BlockSpec((1, per, ATTN_WIDTH, MOBA_BLOCK), lambda i: (i // nt, i % nt, 0, 0))
    blk_t_shape = jax.ShapeDtypeStruct((batch, nb, ATTN_WIDTH, MOBA_BLOCK), BF16)
    return pl.pallas_call(
        functools.partial(_in_proj_kernel, nb=nb),
        grid=(rows // tile,),
        in_specs=[
            row_spec(d_model),
            const((1, d_model)),
            const(wnn.shape),
            const(wnt.shape),
            pl.BlockSpec((tile, 128), lambda i: (i % nt, 0)),
            pl.BlockSpec((tile, 128), lambda i: (i % nt, 0)),
            pl.BlockSpec((per, 128, MOBA_BLOCK), lambda i: (i % nt, 0, 0)),
            pl.BlockSpec((per, 128, MOBA_BLOCK), lambda i: (i % nt, 0, 0)),
            const((1, SHIFT_WIDTH)),
            const(hsel.shape),
            const(wlora.shape),
        ] + [const((1, RWKV_WIDTH))] * len(rw_vecs),
        out_specs=[
            blk_t,
            pl.BlockSpec((ATTN_WIDTH // GROUP_WIDTH, tile, GROUP_WIDTH), lambda i: (0, i, 0)),
            pl.BlockSpec((per, 1, ATTN_WIDTH), lambda i: (i, 0, 0)),
            pl.BlockSpec((per, 1, 128), lambda i: (i, 0, 0)),
            blk_t,
            blk_t,
            row_spec(RW_OPERANDS * RWKV_WIDTH),
            pl.BlockSpec((tile // CHUNK, 1, RWKV_WIDTH), lambda i: (i, 0, 0)),
            row_spec(RWKV_WIDTH),
        ],
        out_shape=[
            blk_t_shape,
            jax.ShapeDtypeStruct((ATTN_WIDTH // GROUP_WIDTH, rows, GROUP_WIDTH), BF16),
            jax.ShapeDtypeStruct((rows // MOBA_BLOCK, 1, ATTN_WIDTH), F32),
            jax.ShapeDtypeStruct((rows // MOBA_BLOCK, 1, 128), F32),
            blk_t_shape,
            blk_t_shape,
            jax.ShapeDtypeStruct((rows, RW_OPERANDS * RWKV_WIDTH), BF16),
            jax.ShapeDtypeStruct((rows // CHUNK, 1, RWKV_WIDTH), F32),
            jax.ShapeDtypeStruct((rows, RWKV_WIDTH), BF16),
        ],
        scratch_shapes=[pltpu.VMEM((8, SHIFT_WIDTH), F32)],
        compiler_params=pltpu.CompilerParams(
            dimension_semantics=("arbitrary",), vmem_limit_bytes=VMEM_LIMIT_BYTES),
        name="in_proj",
    )(x2d, gain, wnn, wnt, cos, sin, cos_t, sin_t, mu, hsel, wlora, *rw_vecs)


def _moba_kernel(qta_ref, qtb_ref, k_ref, kmean_ref, kn2_ref, vt_ref, gata_ref, gatb_ref, outa_ref, outb_ref,
                 qh_ref, bias_ref, m_ref, acc_ref, *, nb):
    j = pl.program_id(2)
    blk = MOBA_BLOCK
    tiles = ((qta_ref, gata_ref, outa_ref, j), (qtb_ref, gatb_ref, outb_ref, nb - 1 - j))
    feat = lax.broadcasted_iota(jnp.int32, (GROUP_WIDTH, 1), 0)
    head_of_feat = (feat % 128) // HALF
    ones_rows = jnp.ones((ACC_ROWS - HEAD_DIM, blk), BF16)
    km = kmean_ref[0]
    km_lane_head = (lax.broadcasted_iota(jnp.int32, (1, GROUP_WIDTH), 1) % 128) // HALF
    km_heads = jnp.concatenate([jnp.where(km_lane_head == h, km, 0.0) for h in range(GROUP_HEADS)], axis=0)
    km_hi, km_lo = _split_bf16(km_heads)
    blk_id = lax.broadcasted_iota(jnp.int32, (GROUP_HEADS, nb, blk), 1)
    k_pos = lax.broadcasted_iota(jnp.int32, (blk, blk), 0)
    q_pos = lax.broadcasted_iota(jnp.int32, (blk, blk), 1)
    causal = k_pos <= q_pos
    kn2 = kn2_ref[0]
    kn2_row = lax.broadcasted_iota(jnp.int32, kn2.shape, 0)
    kn2_lane = lax.broadcasted_iota(jnp.int32, (1, kn2.shape[1]), 1)
    first_head = pl.program_id(1) * GROUP_HEADS

    def values_ext(ki, h):
        return jnp.concatenate([vt_ref[0, ki, h * HEAD_DIM:(h + 1) * HEAD_DIM, :], ones_rows], axis=0)

    bounds = []
    for t, (qt_ref, _, _, qi) in enumerate(tiles):
        qt = qt_ref[0, 0]
        q_sq = qt.astype(F32)
        q_sq = q_sq * q_sq
        kn2_seen = jnp.max(jnp.where(kn2_row <= qi, kn2, 0.0), axis=0, keepdims=True)
        for h in range(GROUP_HEADS):
            qn2 = (jnp.sum(q_sq[h * HALF:(h + 1) * HALF], axis=0, keepdims=True)
                   + jnp.sum(q_sq[128 + h * HALF:128 + (h + 1) * HALF], axis=0, keepdims=True))
            kn2_h = jnp.sum(jnp.where(kn2_lane == first_head + h, kn2_seen, 0.0), axis=1, keepdims=True)
            bounds.append(jnp.sqrt(kn2_h * qn2) * SCORE_BOUND_MARGIN)
            qh_ref[t, h] = jnp.where(head_of_feat == h, qt, jnp.zeros_like(qt))
        gate = (_dot(km_hi, qt) + _dot(km_lo, qt)).reshape(GROUP_HEADS, nb, blk)
        gate = jnp.where(blk_id < qi, gate, -jnp.inf)
        bias = jnp.full(gate.shape, MASKED, F32)
        for r in range(min(MOBA_TOPK, nb)):
            top = jnp.max(gate, axis=1, keepdims=True)
            idx = jnp.min(jnp.where(gate == top, blk_id, nb), axis=1, keepdims=True)
            pick = blk_id == idx + jnp.where(r < qi, 0, nb)
            bias = jnp.where(pick, 0.0, bias)
            gate = jnp.where(pick, -jnp.inf, gate)
        bias_ref[t] = bias

    def issue_scores(t, ki):
        kb = k_ref[0, 0, pl.ds(pl.multiple_of(ki * blk, blk), blk), :]
        return [_dot(kb, qh_ref[t, h]) for h in range(GROUP_HEADS)]

    def finish_own(t, ki, scores):
        for h in range(GROUP_HEADS):
            s = jnp.where(causal, scores[h], MASKED)
            m = jnp.max(s, axis=0, keepdims=True)
            p = jnp.exp2(s - m).astype(BF16)
            m_ref[t, h] = jnp.broadcast_to(m, (STAT_ROWS, blk))
            acc_ref[t, h] = _dot(values_ext(ki, h), p)

    def finish_past(t, ki, scores):
        for h in range(GROUP_HEADS):
            s = scores[h]
            bias = bias_ref[t, h, pl.ds(ki, 1), :]
            m_old = m_ref[t, h, 0:1, :]
            m_new = jnp.maximum(m_old, jnp.max(s, axis=0, keepdims=True) + bias)
            alpha = jnp.exp2(m_old - m_new)
            p = jnp.exp2(s - (m_new - bias)).astype(BF16)
            m_ref[t, h] = jnp.broadcast_to(m_new, (STAT_ROWS, blk))
            acc_ref[t, h] = alpha * acc_ref[t, h] + _dot(values_ext(ki, h), p)

    def bounded_own(t, ki, scores):
        for h in range(GROUP_HEADS):
            s = jnp.where(causal, scores[h], MASKED)
            p = jnp.exp2(s - m_ref[t, h, 0:1, :]).astype(BF16)
            acc_ref[t, h] = _dot(values_ext(ki, h), p)

    def bounded_past(t, ki, scores):
        for h in range(GROUP_HEADS):
            bias = bias_ref[t, h, pl.ds(ki, 1), :]
            p = jnp.exp2(scores[h] - (m_ref[t, h, 0:1, :] - bias)).astype(BF16)
            acc_ref[t, h] = acc_ref[t, h] + _dot(values_ext(ki, h), p)

    def attend(own, past):
        work = [(own, 0, j), (own, 1, nb - 1 - j)]
        for n in range(nb - 1):
            second = (n >= j).astype(jnp.int32)
            work.append((past, second, n - second * j))
        pending = None
        for fin, t, ki in work:
            scores = issue_scores(t, ki)
            if pending is not None:
                pending[0](*pending[1:])
            pending = (fin, t, ki, scores)
        pending[0](*pending[1:])

    bound_ok = jnp.max(jnp.concatenate(bounds, axis=0)) < SCORE_BOUND_SAFE

    @pl.when(bound_ok)
    def _():
        for i, bound in enumerate(bounds):
            m_ref[i // GROUP_HEADS, i % GROUP_HEADS] = jnp.broadcast_to(bound, (STAT_ROWS, blk))
        attend(bounded_own, bounded_past)

    @pl.when(jnp.logical_not(bound_ok))
    def _():
        attend(finish_own, finish_past)

    for t, (_, gat_ref, out_ref, _) in enumerate(tiles):
        for h in range(GROUP_HEADS):
            rows = slice(h * HEAD_DIM, (h + 1) * HEAD_DIM)
            acc = acc_ref[t, h]
            y = acc[0:HEAD_DIM] / acc[HEAD_DIM:HEAD_DIM + 1]
            out_ref[0, 0, rows, :] = (y * gat_ref[0, 0, rows, :].astype(F32)).astype(BF16)


def _moba(qt, k, kmean, kn2, vt, gat, batch, seq):
    nb = seq // MOBA_BLOCK
    assert nb % 2 == 0
    blk = MOBA_BLOCK
    groups = ATTN_WIDTH // GROUP_WIDTH
    k4 = k.reshape(groups, batch, seq, GROUP_WIDTH)
    km3 = kmean.reshape(batch, nb, ATTN_WIDTH)
    kn3 = kn2.reshape(batch, nb, 128)
    tile_a = pl.BlockSpec((1, 1, GROUP_WIDTH, blk), lambda b, g, j: (b, j, g, 0))
    tile_b = pl.BlockSpec((1, 1, GROUP_WIDTH, blk), lambda b, g, j: (b, nb - 1 - j, g, 0))
    out_shape = jax.ShapeDtypeStruct((batch, nb // 2, ATTN_WIDTH, blk), BF16)
    return pl.pallas_call(
        functools.partial(_moba_kernel, nb=nb),
        grid=(batch, groups, nb // 2),
        in_specs=[
            tile_a,
            tile_b,
            pl.BlockSpec((1, 1, seq, GROUP_WIDTH), lambda b, g, j: (g, b, 0, 0)),
            pl.BlockSpec((1, nb, GROUP_WIDTH), lambda b, g, j: (b, 0, g)),
            pl.BlockSpec((1, nb, 128), lambda b, g, j: (b, 0, 0)),
            pl.BlockSpec((1, nb, GROUP_WIDTH, blk), lambda b, g, j: (b, 0, g, 0)),
            tile_a,
            tile_b,
        ],
        out_specs=[tile_a, tile_a],
        out_shape=[out_shape, out_shape],
        scratch_shapes=[
            pltpu.VMEM((2, GROUP_HEADS, GROUP_WIDTH, blk), BF16),
            pltpu.VMEM((2, GROUP_HEADS, nb, blk), F32),
            pltpu.VMEM((2, GROUP_HEADS, STAT_ROWS, blk), F32),
            pltpu.VMEM((2, GROUP_HEADS, ACC_ROWS, blk), F32),
        ],
        compiler_params=pltpu.CompilerParams(
            dimension_semantics=("arbitrary", "arbitrary", "arbitrary"), vmem_limit_bytes=VMEM_LIMIT_BYTES),
        name="moba",
    )(qt, qt, k4, km3, kn3, vt, gat, gat)


def _block_diag(x_bf16, mask):
    tiled = jnp.concatenate([x_bf16] * GROUP_HEADS, axis=0)
    return jnp.where(mask, tiled, jnp.zeros_like(tiled))


def _rwkv_kernel(rw_ref, edec_ref, gb_ref, gng_ref, gnb_ref, yat1_ref, yat2_ref, x_ref, woa_ref, wob_ref,
                 gain_ref, out_ref, state_ref, y_ref, *, nb):
    tile = rw_ref.shape[0]
    c_w = RWKV_WIDTH

    @pl.when(pl.program_id(1) == 0)
    def _():
        state_ref[...] = jnp.zeros_like(state_ref)

    gr = lax.broadcasted_iota(jnp.int32, (GROUP_WIDTH, GROUP_WIDTH), 0) // HEAD_DIM
    gc = lax.broadcasted_iota(jnp.int32, (GROUP_WIDTH, GROUP_WIDTH), 1) // HEAD_DIM
    diag_mask = gr == gc
    t_idx = lax.broadcasted_iota(jnp.int32, (CHUNK, GROUP_WIDTH), 0)
    j_idx = lax.broadcasted_iota(jnp.int32, (CHUNK, GROUP_WIDTH), 1) % HEAD_DIM
    strict_lower = j_idx < t_idx
    lower = j_idx <= t_idx
    eye = (j_idx == t_idx).astype(F32)

    def below_diag(size):
        return ((t_idx // (2 * size) == j_idx // (2 * size))
                & ((t_idx // size) % 2 == 1) & ((j_idx // size) % 2 == 0))

    def bd(x):
        return _block_diag(x.astype(BF16), diag_mask)

    n_groups = c_w // GROUP_WIDTH
    items = []
    for c in range(tile // CHUNK):
        rs = slice(c * CHUNK, (c + 1) * CHUNK)
        end_decay = edec_ref[c]
        for gi in range(n_groups):
            ls = slice(gi * GROUP_WIDTH, (gi + 1) * GROUP_WIDTH)
            operand = lambda n: rw_ref[rs, n * c_w + gi * GROUP_WIDTH:n * c_w + (gi + 1) * GROUP_WIDTH]
            it = dict(rs=rs, ls=ls, gi=gi)
            it["a_t"] = operand(0)
            it["r_t"] = operand(1)
            it["b_et"] = operand(4)
            it["k_et"] = operand(5)
            it["v_c"] = operand(6)
            it["m_diag"] = eye * end_decay[:, ls]
            ar = jnp.concatenate([it["a_t"], it["r_t"]], axis=0)
            pb = _dot_nt(ar, _block_diag(operand(2), diag_mask))
            pk = _dot_nt(ar, _block_diag(operand(3), diag_mask))
            it["n_ab"] = jnp.where(strict_lower, pb[0:CHUNK], 0.0)
            it["t_inv"] = eye + jnp.where(below_diag(1), it["n_ab"], 0.0)
            it["a_rb"] = jnp.where(lower, pb[CHUNK:], 0.0).astype(BF16)
            it["a_k"] = jnp.concatenate([jnp.where(strict_lower, pk[0:CHUNK], 0.0),
                                         jnp.where(lower, pk[CHUNK:], 0.0)], axis=0).astype(BF16)
            items.append(it)

    def affine_maps(its):
        size = 2
        while size < CHUNK:
            for it in its:
                n21 = jnp.where(below_diag(size), it["n_ab"], 0.0)
                it["t_n21"] = _dot(it["t_inv"].astype(BF16), bd(n21))
            yield
            for it in its:
                it["t_inv"] = it["t_inv"] + _dot(it["t_n21"].astype(BF16), bd(it["t_inv"]))
            yield
            size *= 2
        for it in its:
            it["t_inv"] = it["t_inv"].astype(BF16)
        for it in its:
            kv = _dot(jnp.concatenate([it["a_k"], it["k_et"]], axis=0), bd(it["v_c"]))
            it["akv"] = kv[0:CHUNK]
            it["arkv"] = kv[CHUNK:2 * CHUNK]
            it["kv_end"] = kv[2 * CHUNK:]
        yield
        for it in its:
            it["w_t"] = _dot(it["t_inv"], bd(it["a_t"])).astype(BF16)
            it["u_t"] = _dot(it["t_inv"], bd(it["akv"])).astype(BF16)
        yield
        for it in its:
            lhs = jnp.concatenate([it["a_rb"], it["b_et"]], axis=0)
            by_w = _dot(lhs, bd(it["w_t"]))
            by_u = _dot(lhs, bd(it["u_t"]))
            it["r_hat"] = (it["r_t"].astype(F32) + by_w[0:CHUNK]).astype(BF16)
            it["m"] = (it["m_diag"] + by_w[CHUNK:]).astype(BF16)
            it["y_hat"] = by_u[0:CHUNK] + it["arkv"]
            it["n"] = by_u[CHUNK:] + it["kv_end"]
        yield

    in_first = pl.program_id(1) < nb // 2
    per = yat1_ref.shape[1]
    n_chunks = tile // CHUNK
    states = [state_ref[gi] for gi in range(n_groups)]

    def advance_chunk(chunk_items):
        for it in chunk_items:
            gi = it["gi"]
            res = _dot(jnp.concatenate([it["m"], it["r_hat"]], axis=0), bd(states[gi]))
            y_ref[it["rs"], it["ls"]] = res[CHUNK:] + it["y_hat"]
            states[gi] = res[0:CHUNK] + it["n"]

    def output_rows(blk):
        rows = slice(blk * MOBA_BLOCK, (blk + 1) * MOBA_BLOCK)
        y = y_ref[rows, :]
        mean = _head_sum(y) * (1.0 / HEAD_DIM)
        d = y - mean
        var = _head_sum(d * d) * (1.0 / HEAD_DIM)
        yn = d * lax.rsqrt(var + GN_EPS) * gng_ref[...] + gnb_ref[...]
        bonus = rw_ref[rows, (RW_OPERANDS - 1) * c_w:].astype(F32)
        yb = ((yn + bonus) * gb_ref[rows, :].astype(F32)).astype(BF16)
        yat_t = jnp.transpose(jnp.where(in_first, yat1_ref[0, blk], yat2_ref[0, per - 1 - blk]))
        pieces = []
        for c0 in range(0, woa_ref.shape[1], GROUP_WIDTH):
            cols = slice(c0, c0 + GROUP_WIDTH)
            pieces.append(x_ref[rows, cols] + _dot(yat_t, woa_ref[:, cols]) + _dot(yb, wob_ref[:, cols]))
            yield
        h = jnp.concatenate(pieces, axis=-1)
        ms = jnp.mean(h * h, axis=-1, keepdims=True)
        out_ref[rows, :] = (h * lax.rsqrt(ms + RMS_EPS) * gain_ref[...]).astype(out_ref.dtype)

    assert per == 2 and n_chunks % 2 == 0
    half = (n_chunks // 2) * n_groups
    first, second = items[:half], items[half:]
    for _ in affine_maps(first):
        pass
    first_chunks = [first[c:c + n_groups] for c in range(0, len(first), n_groups)]
    for _ in affine_maps(second):
        if first_chunks:
            advance_chunk(first_chunks.pop(0))
    for chunk_items in first_chunks:
        advance_chunk(chunk_items)
    first_out = output_rows(0)
    for c in range(0, len(second), n_groups):
        advance_chunk(second[c:c + n_groups])
        next(first_out, None)
    for _ in first_out:
        pass
    for gi in range(n_groups):
        state_ref[gi] = states[gi]
    for _ in output_rows(1):
        pass


def _rwkv(rw, edec, gb, gn_gain, gn_bias, yat_first, yat_second, x2d, woa, wob, out_gain, batch, seq):
    tile, per = ROW_TILE, BLOCKS_PER_TILE
    nt = seq // tile
    assert nt % 2 == 0
    half = nt // 2
    d_model = x2d.shape[1]
    row_spec = lambda width: pl.BlockSpec((tile, width), lambda b, i: (b * nt + i, 0))
    const = lambda shape: pl.BlockSpec(shape, lambda b, i: (0,) * len(shape))
    vec = lambda width: const((1, width))
    blk_t = lambda pick: pl.BlockSpec((1, per, ATTN_WIDTH, MOBA_BLOCK),
                                      lambda b, i: (b, jnp.minimum(pick(i), half - 1), 0, 0))
    return pl.pallas_call(
        functools.partial(_rwkv_kernel, nb=nt),
        grid=(batch, nt),
        in_specs=[
            row_spec(RW_OPERANDS * RWKV_WIDTH),
            pl.BlockSpec((tile // CHUNK, 1, RWKV_WIDTH), lambda b, i: (b * nt + i, 0, 0)),
            row_spec(RWKV_WIDTH),
            vec(RWKV_WIDTH),
            vec(RWKV_WIDTH),
            blk_t(lambda i: i),
            blk_t(lambda i: nt - 1 - i),
            row_spec(d_model),
            const((ATTN_WIDTH, d_model)),
            const((RWKV_WIDTH, d_model)),
            vec(d_model),
        ],
        out_specs=row_spec(d_model),
        out_shape=jax.ShapeDtypeStruct(x2d.shape, x2d.dtype),
        scratch_shapes=[
            pltpu.VMEM((RWKV_WIDTH // GROUP_WIDTH, HEAD_DIM, GROUP_WIDTH), F32),
            pltpu.VMEM((tile, RWKV_WIDTH), F32),
        ],
        compiler_params=pltpu.CompilerParams(
            dimension_semantics=("arbitrary", "arbitrary"), vmem_limit_bytes=VMEM_LIMIT_BYTES),
        name="rwkv_out",
    )(rw, edec, gb, gn_gain, gn_bias, yat_first, yat_second, x2d, woa, wob, out_gain)


def _rotary_column_order():
    order = []
    for g in range(ATTN_WIDTH // GROUP_WIDTH):
        for part in range(2):
            for hh in range(GROUP_HEADS):
                head = g * GROUP_HEADS + hh
                order.extend(head * HEAD_DIM + part * HALF + j for j in range(HALF))
    return np.asarray(order, np.int32)


def _layer(h2d, batch, seq, norm_gain, w_in, shift_mu, decay_bias, decay_up, iclr_bias, iclr_up,
           k_k, k_a, r_k, gn_gain, gn_bias, w_out, out_gain, cos, sin):
    nb = seq // MOBA_BLOCK
    aw, rw = ATTN_WIDTH, RWKV_WIDTH
    order = _rotary_column_order()
    b0 = 4 * aw
    w_in = w_in.astype(BF16)
    wnn = jnp.concatenate([w_in[:, aw:2 * aw][:, order], w_in[:, b0:]], axis=1)
    wnt = jnp.concatenate([w_in[:, 0:aw][:, order], w_in[:, 2 * aw:4 * aw]], axis=1).T
    zeros = jnp.zeros((LORA_RANK, rw), F32)
    wl = jnp.concatenate([jnp.concatenate([decay_up.astype(F32), zeros], axis=1),
                          jnp.concatenate([zeros, iclr_up.astype(F32)], axis=1)], axis=0)
    wlora = wl.astype(BF16)
    row = lambda t: t.astype(F32).reshape(1, -1)

    cos_t = cos.reshape(nb, MOBA_BLOCK, 128).transpose(0, 2, 1)
    sin_t = sin.reshape(nb, MOBA_BLOCK, 128).transpose(0, 2, 1)
    lane = np.arange(aw)
    head_of_lane = (lane // GROUP_WIDTH) * GROUP_HEADS + (lane % 128) // HALF
    hsel = jnp.asarray(head_of_lane[:, None] == np.arange(128)[None, :], BF16)
    rw_vecs = [row(decay_bias), row(iclr_bias), row(k_k), row(k_a), row(r_k)]
    qt, k, kmean, kn2, vt, gat, rw, edec, gb = _in_proj(h2d, row(norm_gain), wnn, wnt, cos, sin, cos_t, sin_t,
                                                        row(shift_mu), hsel, wlora, rw_vecs, batch, nb)
    yat_first, yat_second = _moba(qt, k, kmean, kn2, vt, gat, batch, seq)
    w_out = w_out.astype(BF16)
    return _rwkv(rw, edec, gb, row(gn_gain), row(gn_bias), yat_first, yat_second, h2d, w_out[0:aw], w_out[aw:],
                 out_gain, batch, seq)


def kernel(x, norm_gain, w_in, shift_mu, decay_bias, decay_up, iclr_bias, iclr_up,
           k_k, k_a, r_k, gn_gain, gn_bias, w_out, final_gain):
    batch, seq, d_model = x.shape
    depth = norm_gain.shape[0]
    assert depth == 1, "the final RMSNorm is fused into the single layer's output projection"
    assert seq % MOBA_BLOCK == 0 and w_in.shape[-1] == 4 * ATTN_WIDTH + SHIFT_WIDTH + RWKV_WIDTH
    inv_freq = 1.0 / (ROPE_THETA ** (jnp.arange(0, HEAD_DIM, 2, dtype=F32) / HEAD_DIM))
    ang = jnp.arange(seq, dtype=F32)[:, None] * inv_freq[None, :]
    cos = jnp.tile(jnp.cos(ang), (1, GROUP_HEADS))
    sin = jnp.tile(jnp.sin(ang), (1, GROUP_HEADS))
    out = _layer(x.reshape(batch * seq, d_model).astype(F32), batch, seq, norm_gain[0], w_in[0],
                 shift_mu[0], decay_bias[0], decay_up[0], iclr_bias[0], iclr_up[0], k_k[0], k_a[0], r_k[0],
                 gn_gain[0], gn_bias[0], w_out[0], final_gain.astype(F32).reshape(1, -1), cos, sin)
    return out.reshape(batch, seq, d_model).astype(x.dtype)
```

```python
import functools

import jax
import jax.numpy as jnp
import numpy as np
from jax import lax
from jax.experimental import pallas as pl
from jax.experimental.pallas import tpu as pltpu

HEAD_DIM = 64
HALF = HEAD_DIM // 2
ATTN_HEADS = 8
ATTN_WIDTH = ATTN_HEADS * HEAD_DIM
RWKV_HEADS = 8
RWKV_WIDTH = RWKV_HEADS * HEAD_DIM
LORA_RANK = 64
SHIFT_WIDTH = 3 * RWKV_WIDTH + 2 * LORA_RANK
MOBA_BLOCK = 256
MOBA_TOPK = 3
ROPE_THETA = 10000.0
RMS_EPS = 1e-6
GN_EPS = 64e-5
NORMALIZE_EPS_SQ = 1e-24

GROUP_HEADS = 4
GROUP_WIDTH = GROUP_HEADS * HEAD_DIM
CHUNK = 64
ROW_TILE = 2 * MOBA_BLOCK
BLOCKS_PER_TILE = ROW_TILE // MOBA_BLOCK
RW_OPERANDS = 8
MASKED = -1e30
DECAY_SCALE = -float(np.exp(-0.5))
SCORE_BOUND_MARGIN = 1.0 + 2.0 ** -6
SCORE_BOUND_SAFE = 56.0
STAT_ROWS = 8
ACC_ROWS = HEAD_DIM + 16
LOG2E = 1.4426950408889634
Q_SCALE = HEAD_DIM ** -0.5 * LOG2E
VMEM_LIMIT_BYTES = 48 * 1024 * 1024

F32 = jnp.float32
BF16 = jnp.bfloat16


def _dot(a, b):
    return jnp.dot(a, b, preferred_element_type=F32)


def _dot_nt(a, b):
    return lax.dot_general(a, b, (((1,), (1,)), ((), ())), preferred_element_type=F32)


def _dot_tn(a, b):
    return lax.dot_general(a, b, (((0,), (0,)), ((), ())), preferred_element_type=F32)


def _split_bf16(x):
    hi = x.astype(BF16)
    lo = (x - hi.astype(F32)).astype(BF16)
    return hi, lo


def _silu(z):
    return z / (1.0 + jnp.exp(-z))


def _rotary(acc, cos, sin):
    outs = []
    for g in range(ATTN_WIDTH // GROUP_WIDTH):
        x1 = acc[:, g * GROUP_WIDTH:g * GROUP_WIDTH + 128]
        x2 = acc[:, g * GROUP_WIDTH + 128:(g + 1) * GROUP_WIDTH]
        outs.append(x1 * cos - x2 * sin)
        outs.append(x2 * cos + x1 * sin)
    return jnp.concatenate(outs, axis=-1)


def _rotary_t(acc, cos_t, sin_t):
    outs = []
    for g in range(ATTN_WIDTH // GROUP_WIDTH):
        x1 = acc[g * GROUP_WIDTH:g * GROUP_WIDTH + 128, :]
        x2 = acc[g * GROUP_WIDTH + 128:(g + 1) * GROUP_WIDTH, :]
        outs.append(x1 * cos_t - x2 * sin_t)
        outs.append(x2 * cos_t + x1 * sin_t)
    return jnp.concatenate(outs, axis=0)


def _head_sum(t):
    gr = lax.broadcasted_iota(jnp.int32, (GROUP_WIDTH, GROUP_WIDTH), 0) // HEAD_DIM
    gc = lax.broadcasted_iota(jnp.int32, (GROUP_WIDTH, GROUP_WIDTH), 1) // HEAD_DIM
    head_ones = jnp.where(gr == gc, 1.0, 0.0).astype(BF16)
    t = t.astype(BF16)
    return jnp.concatenate([_dot(t[:, i:i + GROUP_WIDTH], head_ones)
                            for i in range(0, t.shape[1], GROUP_WIDTH)], axis=-1)


def _in_proj_block(s, x_ref, gain_ref, wnn_ref, wnt_ref, cos_ref, sin_ref, cost_ref, sint_ref, mu_ref, hsel_ref,
                   wlora_ref, dbias_ref, ibias_ref, kk_ref, ka_ref, rk_ref,
                   qt_ref, k_ref, kmean_ref, kn2_ref, vt_ref, gat_ref, rw_ref, edec_ref, gb_ref, prev_ref):
    tile = MOBA_BLOCK
    rows = slice(s * tile, (s + 1) * tile)
    x = x_ref[rows, :]
    ms = jnp.mean(x * x, axis=-1, keepdims=True)
    u = (x * lax.rsqrt(ms + RMS_EPS) * gain_ref[...]).astype(BF16)

    k_att = _rotary(_dot(u, wnn_ref[:, 0:ATTN_WIDTH]), cos_ref[rows, :], sin_ref[rows, :])
    k_bf = k_att.astype(BF16)
    for g in range(ATTN_WIDTH // GROUP_WIDTH):
        k_ref[g, rows, :] = k_bf[:, g * GROUP_WIDTH:(g + 1) * GROUP_WIDTH]
    kmean_ref[s] = jnp.mean(k_att, axis=0, keepdims=True)
    k_sq = k_bf.astype(F32)
    k_sq = (k_sq * k_sq).astype(BF16)

    base = ATTN_WIDTH
    c_w = RWKV_WIDTH
    row = lax.broadcasted_iota(jnp.int32, (tile, 1), 0)

    def shifted_cols(c, w):
        p = _dot(u, wnn_ref[:, base + c:base + c + w])
        shifted = jnp.where(row == 0, prev_ref[0:1, c:c + w], pltpu.roll(p, 1, axis=0))
        prev_ref[0:1, c:c + w] = p[tile - 1:tile, :]
        return p + (shifted - p) * mu_ref[:, c:c + w]

    lora_in = shifted_cols(3 * c_w, 2 * LORA_RANK)
    lane128 = lax.broadcasted_iota(jnp.int32, (1, 2 * LORA_RANK), 1)
    lora_in = jnp.where(lane128 < LORA_RANK, jnp.tanh(lora_in), lora_in)
    li_hi, li_lo = _split_bf16(lora_in)
    k = shifted_cols(c_w, c_w)
    kk = k * kk_ref[...]
    r = shifted_cols(0, c_w)
    v = shifted_cols(2 * c_w, c_w)

    kn2_ref[s] = jnp.max(_dot(k_sq, hsel_ref[...]), axis=0, keepdims=True)
    w_hi = wlora_ref[...]
    lora = _dot(li_hi, w_hi) + _dot(li_lo, w_hi)
    kk = kk * lax.rsqrt(jnp.maximum(_head_sum(kk * kk), NORMALIZE_EPS_SQ))

    gb = _dot(u, wnn_ref[:, base + SHIFT_WIDTH:base + SHIFT_WIDTH + RWKV_WIDTH])
    gb_ref[rows, :] = _silu(gb).astype(BF16)

    g = DECAY_SCALE / (1.0 + jnp.exp(-(dbias_ref[...] + lora[:, 0:c_w])))
    a = 1.0 / (1.0 + jnp.exp(-(ibias_ref[...] + lora[:, c_w:2 * c_w])))
    g_hi, g_lo = _split_bf16(g)
    k2 = k * (1.0 + (a - 1.0) * ka_ref[...])
    b = kk * a

    qt = _rotary_t(_dot_nt(wnt_ref[0:ATTN_WIDTH, :], u), cost_ref[s], sint_ref[s])
    qt_ref[0, s] = (qt * Q_SCALE).astype(BF16)

    ti = lax.broadcasted_iota(jnp.int32, (CHUNK, CHUNK), 0)
    tj = lax.broadcasted_iota(jnp.int32, (CHUNK, CHUNK), 1)
    tri_incl = jnp.where(tj <= ti, 1.0, 0.0).astype(BF16)
    cums = [_dot(tri_incl, g_hi[c:c + CHUNK]) + _dot(tri_incl, g_lo[c:c + CHUNK])
            for c in range(0, tile, CHUNK)]
    bonus = _head_sum(r * k2 * rk_ref[...]) * v

    for i, cum in enumerate(cums):
        rs = slice(i * CHUNK, (i + 1) * CHUNK)
        dec_in = jnp.exp(cum)
        dec_ex = jnp.exp(cum - g[rs])
        inv = 1.0 / dec_in
        end_decay = dec_in[CHUNK - 1:CHUNK, :]
        to_end = inv * end_decay
        edec_ref[s * (tile // CHUNK) + i] = end_decay
        cols = (-kk[rs] * dec_ex, r[rs] * dec_in, b[rs] * inv, k2[rs] * inv,
                b[rs] * to_end, k2[rs] * to_end, v[rs], bonus[rs])
        for n, col in enumerate(cols):
            rw_ref[s * tile + i * CHUNK:s * tile + (i + 1) * CHUNK, n * c_w:(n + 1) * c_w] = col.astype(BF16)

    vt_ref[0, s] = _dot_nt(wnt_ref[ATTN_WIDTH:2 * ATTN_WIDTH, :], u).astype(BF16)
    gat = _dot_nt(wnt_ref[2 * ATTN_WIDTH:3 * ATTN_WIDTH, :], u)
    gat_ref[0, s] = _silu(gat).astype(BF16)


def _in_proj_kernel(x_ref, gain_ref, wnn_ref, wnt_ref, cos_ref, sin_ref, cost_ref, sint_ref, mu_ref, hsel_ref,
                    wlora_ref, dbias_ref, ibias_ref, kk_ref, ka_ref, rk_ref,
                    qt_ref, k_ref, kmean_ref, kn2_ref, vt_ref, gat_ref, rw_ref, edec_ref, gb_ref, prev_ref,
                    *, nb):
    @pl.when(pl.program_id(0) % (nb // BLOCKS_PER_TILE) == 0)
    def _():
        prev_ref[...] = jnp.zeros_like(prev_ref)

    for s in range(BLOCKS_PER_TILE):
        _in_proj_block(s, x_ref, gain_ref, wnn_ref, wnt_ref, cos_ref, sin_ref, cost_ref, sint_ref, mu_ref,
                       hsel_ref, wlora_ref, dbias_ref, ibias_ref, kk_ref, ka_ref, rk_ref,
                       qt_ref, k_ref, kmean_ref, kn2_ref, vt_ref, gat_ref, rw_ref, edec_ref, gb_ref, prev_ref)


def _in_proj(x2d, gain, wnn, wnt, cos, sin, cos_t, sin_t, mu, hsel, wlora, rw_vecs, batch, nb):
    rows, d_model = x2d.shape
    tile, per = ROW_TILE, BLOCKS_PER_TILE
    assert nb % per == 0
    nt = nb // per
    row_spec = lambda width: pl.BlockSpec((tile, width), lambda i: (i, 0))
    const = lambda shape: pl.BlockSpec(shape, lambda i: (0,) * len(shape))
    blk_t = pl.BlockSpec((1, per, ATTN_WIDTH, MOBA_BLOCK), lambda i: (i // nt, i % nt, 0, 0))
    blk_t_shape = jax.ShapeDtypeStruct((batch, nb, ATTN_WIDTH, MOBA_BLOCK), BF16)
    return pl.pallas_call(
        functools.partial(_in_proj_kernel, nb=nb),
        grid=(rows // tile,),
        in_specs=[
            row_spec(d_model),
            const((1, d_model)),
            const(wnn.shape),
            const(wnt.shape),
            pl.BlockSpec((tile, 128), lambda i: (i % nt, 0)),
            pl.BlockSpec((tile, 128), lambda i: (i % nt, 0)),
            pl.BlockSpec((per, 128, MOBA_BLOCK), lambda i: (i % nt, 0, 0)),
            pl.BlockSpec((per, 128, MOBA_BLOCK), lambda i: (i % nt, 0, 0)),
            const((1, SHIFT_WIDTH)),
            const(hsel.shape),
            const(wlora.shape),
        ] + [const((1, RWKV_WIDTH))] * len(rw_vecs),
        out_specs=[
            blk_t,
            pl.BlockSpec((ATTN_WIDTH // GROUP_WIDTH, tile, GROUP_WIDTH), lambda i: (0, i, 0)),
            pl.BlockSpec((per, 1, ATTN_WIDTH), lambda i: (i, 0, 0)),
            pl.BlockSpec((per, 1, 128), lambda i: (i, 0, 0)),
            blk_t,
            blk_t,
            row_spec(RW_OPERANDS * RWKV_WIDTH),
            pl.BlockSpec((tile // CHUNK, 1, RWKV_WIDTH), lambda i: (i, 0, 0)),
            row_spec(RWKV_WIDTH),
        ],
        out_shape=[
            blk_t_shape,
            jax.ShapeDtypeStruct((ATTN_WIDTH // GROUP_WIDTH, rows, GROUP_WIDTH), BF16),
            jax.ShapeDtypeStruct((rows // MOBA_BLOCK, 1, ATTN_WIDTH), F32),
            jax.ShapeDtypeStruct((rows // MOBA_BLOCK, 1, 128), F32),
            blk_t_shape,
            blk_t_shape,
            jax.ShapeDtypeStruct((rows, RW_OPERANDS * RWKV_WIDTH), BF16),
            jax.ShapeDtypeStruct((rows // CHUNK, 1, RWKV_WIDTH), F32),
            jax.ShapeDtypeStruct((rows, RWKV_WIDTH), BF16),
        ],
        scratch_shapes=[pltpu.VMEM((8, SHIFT_WIDTH), F32)],
        compiler_params=pltpu.CompilerParams(
            dimension_semantics=("arbitrary",), vmem_limit_bytes=VMEM_LIMIT_BYTES),
        name="in_proj",
    )(x2d, gain, wnn, wnt, cos, sin, cos_t, sin_t, mu, hsel, wlora, *rw_vecs)


def _moba_kernel(qta_ref, qtb_ref, k_ref, kmean_ref, kn2_ref, vt_ref, gata_ref, gatb_ref, outa_ref, outb_ref,
                 qh_ref, bias_ref, m_ref, acc_ref, *, nb):
    j = pl.program_id(2)
    blk = MOBA_BLOCK
    tiles = ((qta_ref, gata_ref, outa_ref, j), (qtb_ref, gatb_ref, outb_ref, nb - 1 - j))
    feat = lax.broadcasted_iota(jnp.int32, (GROUP_WIDTH, 1), 0)
    head_of_feat = (feat % 128) // HALF
    ones_rows = jnp.ones((ACC_ROWS - HEAD_DIM, blk), BF16)
    km = kmean_ref[0]
    km_lane_head = (lax.broadcasted_iota(jnp.int32, (1, GROUP_WIDTH), 1) % 128) // HALF
    km_heads = jnp.concatenate([jnp.where(km_lane_head == h, km, 0.0) for h in range(GROUP_HEADS)], axis=0)
    km_hi, km_lo = _split_bf16(km_heads)
    blk_id = lax.broadcasted_iota(jnp.int32, (GROUP_HEADS, nb, blk), 1)
    k_pos = lax.broadcasted_iota(jnp.int32, (blk, blk), 0)
    q_pos = lax.broadcasted_iota(jnp.int32, (blk, blk), 1)
    causal = k_pos <= q_pos
    kn2 = kn2_ref[0]
    kn2_row = lax.broadcasted_iota(jnp.int32, kn2.shape, 0)
    kn2_lane = lax.broadcasted_iota(jnp.int32, (1, kn2.shape[1]), 1)
    first_head = pl.program_id(1) * GROUP_HEADS

    def values_ext(ki, h):
        return jnp.concatenate([vt_ref[0, ki, h * HEAD_DIM:(h + 1) * HEAD_DIM, :], ones_rows], axis=0)

    bounds = []
    for t, (qt_ref, _, _, qi) in enumerate(tiles):
        qt = qt_ref[0, 0]
        q_sq = qt.astype(F32)
        q_sq = q_sq * q_sq
        kn2_seen = jnp.max(jnp.where(kn2_row <= qi, kn2, 0.0), axis=0, keepdims=True)
        for h in range(GROUP_HEADS):
            qn2 = (jnp.sum(q_sq[h * HALF:(h + 1) * HALF], axis=0, keepdims=True)
                   + jnp.sum(q_sq[128 + h * HALF:128 + (h + 1) * HALF], axis=0, keepdims=True))
            kn2_h = jnp.sum(jnp.where(kn2_lane == first_head + h, kn2_seen, 0.0), axis=1, keepdims=True)
            bounds.append(jnp.sqrt(kn2_h * qn2) * SCORE_BOUND_MARGIN)
            qh_ref[t, h] = jnp.where(head_of_feat == h, qt, jnp.zeros_like(qt))
        gate = (_dot(km_hi, qt) + _dot(km_lo, qt)).reshape(GROUP_HEADS, nb, blk)
        gate = jnp.where(blk_id < qi, gate, -jnp.inf)
        bias = jnp.full(gate.shape, MASKED, F32)
        for r in range(min(MOBA_TOPK, nb)):
            top = jnp.max(gate, axis=1, keepdims=True)
            idx = jnp.min(jnp.where(gate == top, blk_id, nb), axis=1, keepdims=True)
            pick = blk_id == idx + jnp.where(r < qi, 0, nb)
            bias = jnp.where(pick, 0.0, bias)
            gate = jnp.where(pick, -jnp.inf, gate)
        bias_ref[t] = bias

    def issue_scores(t, ki):
        kb = k_ref[0, 0, pl.ds(pl.multiple_of(ki * blk, blk), blk), :]
        return [_dot(kb, qh_ref[t, h]) for h in range(GROUP_HEADS)]

    def finish_own(t, ki, scores):
        for h in range(GROUP_HEADS):
            s = jnp.where(causal, scores[h], MASKED)
            m = jnp.max(s, axis=0, keepdims=True)
            p = jnp.exp2(s - m).astype(BF16)
            m_ref[t, h] = jnp.broadcast_to(m, (STAT_ROWS, blk))
            acc_ref[t, h] = _dot(values_ext(ki, h), p)

    def finish_past(t, ki, scores):
        for h in range(GROUP_HEADS):
            s = scores[h]
            bias = bias_ref[t, h, pl.ds(ki, 1), :]
            m_old = m_ref[t, h, 0:1, :]
            m_new = jnp.maximum(m_old, jnp.max(s, axis=0, keepdims=True) + bias)
            alpha = jnp.exp2(m_old - m_new)
            p = jnp.exp2(s - (m_new - bias)).astype(BF16)
            m_ref[t, h] = jnp.broadcast_to(m_new, (STAT_ROWS, blk))
            acc_ref[t, h] = alpha * acc_ref[t, h] + _dot(values_ext(ki, h), p)

    def bounded_own(t, ki, scores):
        for h in range(GROUP_HEADS):
            s = jnp.where(causal, scores[h], MASKED)
            p = jnp.exp2(s - m_ref[t, h, 0:1, :]).astype(BF16)
            acc_ref[t, h] = _dot(values_ext(ki, h), p)

    def bounded_past(t, ki, scores):
        for h in range(GROUP_HEADS):
            bias = bias_ref[t, h, pl.ds(ki, 1), :]
            p = jnp.exp2(scores[h] - (m_ref[t, h, 0:1, :] - bias)).astype(BF16)
            acc_ref[t, h] = acc_ref[t, h] + _dot(values_ext(ki, h), p)

    def attend(own, past):
        work = [(own, 0, j), (own, 1, nb - 1 - j)]
        for n in range(nb - 1):
            second = (n >= j).astype(jnp.int32)
            work.append((past, second, n - second * j))
        pending = None
        for fin, t, ki in work:
            scores = issue_scores(t, ki)
            if pending is not None:
                pending[0](*pending[1:])
            pending = (fin, t, ki, scores)
        pending[0](*pending[1:])

    bound_ok = jnp.max(jnp.concatenate(bounds, axis=0)) < SCORE_BOUND_SAFE

    @pl.when(bound_ok)
    def _():
        for i, bound in enumerate(bounds):
            m_ref[i // GROUP_HEADS, i % GROUP_HEADS] = jnp.broadcast_to(bound, (STAT_ROWS, blk))
        attend(bounded_own, bounded_past)

    @pl.when(jnp.logical_not(bound_ok))
    def _():
        attend(finish_own, finish_past)

    for t, (_, gat_ref, out_ref, _) in enumerate(tiles):
        for h in range(GROUP_HEADS):
            rows = slice(h * HEAD_DIM, (h + 1) * HEAD_DIM)
            acc = acc_ref[t, h]
            y = acc[0:HEAD_DIM] / acc[HEAD_DIM:HEAD_DIM + 1]
            out_ref[0, 0, rows, :] = (y * gat_ref[0, 0, rows, :].astype(F32)).astype(BF16)


def _moba(qt, k, kmean, kn2, vt, gat, batch, seq):
    nb = seq // MOBA_BLOCK
    assert nb % 2 == 0
    blk = MOBA_BLOCK
    groups = ATTN_WIDTH // GROUP_WIDTH
    k4 = k.reshape(groups, batch, seq, GROUP_WIDTH)
    km3 = kmean.reshape(batch, nb, ATTN_WIDTH)
    kn3 = kn2.reshape(batch, nb, 128)
    tile_a = pl.BlockSpec((1, 1, GROUP_WIDTH, blk), lambda b, g, j: (b, j, g, 0))
    tile_b = pl.BlockSpec((1, 1, GROUP_WIDTH, blk), lambda b, g, j: (b, nb - 1 - j, g, 0))
    out_shape = jax.ShapeDtypeStruct((batch, nb // 2, ATTN_WIDTH, blk), BF16)
    return pl.pallas_call(
        functools.partial(_moba_kernel, nb=nb),
        grid=(batch, groups, nb // 2),
        in_specs=[
            tile_a,
            tile_b,
            pl.BlockSpec((1, 1, seq, GROUP_WIDTH), lambda b, g, j: (g, b, 0, 0)),
            pl.BlockSpec((1, nb, GROUP_WIDTH), lambda b, g, j: (b, 0, g)),
            pl.BlockSpec((1, nb, 128), lambda b, g, j: (b, 0, 0)),
            pl.BlockSpec((1, nb, GROUP_WIDTH, blk), lambda b, g, j: (b, 0, g, 0)),
            tile_a,
            tile_b,
        ],
        out_specs=[tile_a, tile_a],
        out_shape=[out_shape, out_shape],
        scratch_shapes=[
            pltpu.VMEM((2, GROUP_HEADS, GROUP_WIDTH, blk), BF16),
            pltpu.VMEM((2, GROUP_HEADS, nb, blk), F32),
            pltpu.VMEM((2, GROUP_HEADS, STAT_ROWS, blk), F32),
            pltpu.VMEM((2, GROUP_HEADS, ACC_ROWS, blk), F32),
        ],
        compiler_params=pltpu.CompilerParams(
            dimension_semantics=("arbitrary", "arbitrary", "arbitrary"), vmem_limit_bytes=VMEM_LIMIT_BYTES),
        name="moba",
    )(qt, qt, k4, km3, kn3, vt, gat, gat)


def _block_diag(x_bf16, mask):
    tiled = jnp.concatenate([x_bf16] * GROUP_HEADS, axis=0)
    return jnp.where(mask, tiled, jnp.zeros_like(tiled))


def _rwkv_kernel(rw_ref, edec_ref, gb_ref, gng_ref, gnb_ref, yat1_ref, yat2_ref, x_ref, woa_ref, wob_ref,
                 gain_ref, out_ref, state_ref, y_ref, *, nb):
    tile = rw_ref.shape[0]
    c_w = RWKV_WIDTH

    @pl.when(pl.program_id(1) == 0)
    def _():
        state_ref[...] = jnp.zeros_like(state_ref)

    gr = lax.broadcasted_iota(jnp.int32, (GROUP_WIDTH, GROUP_WIDTH), 0) // HEAD_DIM
    gc = lax.broadcasted_iota(jnp.int32, (GROUP_WIDTH, GROUP_WIDTH), 1) // HEAD_DIM
    diag_mask = gr == gc
    t_idx = lax.broadcasted_iota(jnp.int32, (CHUNK, GROUP_WIDTH), 0)
    j_idx = lax.broadcasted_iota(jnp.int32, (CHUNK, GROUP_WIDTH), 1) % HEAD_DIM
    strict_lower = j_idx < t_idx
    lower = j_idx <= t_idx
    eye = (j_idx == t_idx).astype(F32)

    def below_diag(size):
        return ((t_idx // (2 * size) == j_idx // (2 * size))
                & ((t_idx // size) % 2 == 1) & ((j_idx // size) % 2 == 0))
    lane_head = lax.broadcasted_iota(jnp.int32, (1, GROUP_WIDTH), 1) // HEAD_DIM

    def bd(x):
        return _block_diag(x.astype(BF16), diag_mask)

    def diag_blocks(full):
        out = full[(GROUP_HEADS - 1) * HEAD_DIM:, :]
        for h in range(GROUP_HEADS - 2, -1, -1):
            out = jnp.where(lane_head == h, full[h * HEAD_DIM:(h + 1) * HEAD_DIM, :], out)
        return out

    n_groups = c_w // GROUP_WIDTH
    items = []
    for c in range(tile // CHUNK):
        rs = slice(c * CHUNK, (c + 1) * CHUNK)
        end_decay = edec_ref[c]
        for gi in range(n_groups):
            ls = slice(gi * GROUP_WIDTH, (gi + 1) * GROUP_WIDTH)
            operand = lambda n: rw_ref[rs, n * c_w + gi * GROUP_WIDTH:n * c_w + (gi + 1) * GROUP_WIDTH]
            it = dict(rs=rs, ls=ls, gi=gi)
            it["a_t"] = operand(0)
            it["r_t"] = operand(1)
            it["bk_e"] = jnp.concatenate([operand(4), operand(5)], axis=0)
            it["v_c"] = operand(6)
            it["m_diag"] = eye * end_decay[:, ls]
            ar = jnp.concatenate([it["a_t"], it["r_t"]], axis=0)
            pb = _dot_nt(ar, _block_diag(operand(2), diag_mask))
            pk = _dot_nt(ar, _block_diag(operand(3), diag_mask))
            it["n_ab"] = jnp.where(strict_lower, pb[0:CHUNK], 0.0)
            it["t_inv"] = eye + jnp.where(below_diag(1), it["n_ab"], 0.0)
            it["a_rb"] = jnp.where(lower, pb[CHUNK:], 0.0).astype(BF16)
            it["a_k"] = jnp.concatenate([jnp.where(strict_lower, pk[0:CHUNK], 0.0),
                                         jnp.where(lower, pk[CHUNK:], 0.0)], axis=0).astype(BF16)
            items.append(it)

    size = 2
    while size < CHUNK:
        for it in items:
            n21 = jnp.where(below_diag(size), it["n_ab"], 0.0)
            it["t_n21"] = _dot(it["t_inv"].astype(BF16), bd(n21))
        for it in items:
            it["t_inv"] = it["t_inv"] + _dot(it["t_n21"].astype(BF16), bd(it["t_inv"]))
        size *= 2
    for it in items:
        it["t_inv"] = it["t_inv"].astype(BF16)

    def affine_maps(its):
        for it in its:
            kv = _dot(it["a_k"], bd(it["v_c"]))
            it["akv"] = kv[0:CHUNK]
            it["arkv"] = kv[CHUNK:]
        yield
        for it in its:
            it["w_t"] = _dot(it["t_inv"], bd(it["a_t"])).astype(BF16)
            it["u_t"] = _dot(it["t_inv"], bd(it["akv"])).astype(BF16)
        yield
        for it in its:
            it["r_hat"] = (it["r_t"].astype(F32) + _dot(it["a_rb"], bd(it["w_t"]))).astype(BF16)
            it["y_hat"] = _dot(it["a_rb"], bd(it["u_t"])) + it["arkv"]
        yield
        for it in its:
            m_full = _dot_tn(it["bk_e"][0:CHUNK], it["w_t"])
            it["m"] = (it["m_diag"] + diag_blocks(m_full)).astype(BF16)
            n_full = _dot_tn(it["bk_e"], jnp.concatenate([it["u_t"], it["v_c"]], axis=0))
            it["n"] = diag_blocks(n_full)
        yield

    in_first = pl.program_id(1) < nb // 2
    per = yat1_ref.shape[1]
    n_chunks = tile // CHUNK
    states = [state_ref[gi] for gi in range(n_groups)]

    def advance_chunk(chunk_items):
        for it in chunk_items:
            gi = it["gi"]
            res = _dot(jnp.concatenate([it["m"], it["r_hat"]], axis=0), bd(states[gi]))
            y_ref[it["rs"], it["ls"]] = res[CHUNK:] + it["y_hat"]
            states[gi] = res[0:CHUNK] + it["n"]

    def output_rows(blk):
        rows = slice(blk * MOBA_BLOCK, (blk + 1) * MOBA_BLOCK)
        y = y_ref[rows, :]
        mean = _head_sum(y) * (1.0 / HEAD_DIM)
        d = y - mean
        var = _head_sum(d * d) * (1.0 / HEAD_DIM)
        yn = d * lax.rsqrt(var + GN_EPS) * gng_ref[...] + gnb_ref[...]
        bonus = rw_ref[rows, (RW_OPERANDS - 1) * c_w:].astype(F32)
        yb = ((yn + bonus) * gb_ref[rows, :].astype(F32)).astype(BF16)
        yat_t = jnp.transpose(jnp.where(in_first, yat1_ref[0, blk], yat2_ref[0, per - 1 - blk]))
        pieces = []
        for c0 in range(0, woa_ref.shape[1], GROUP_WIDTH):
            cols = slice(c0, c0 + GROUP_WIDTH)
            pieces.append(x_ref[rows, cols] + _dot(yat_t, woa_ref[:, cols]) + _dot(yb, wob_ref[:, cols]))
            yield
        h = jnp.concatenate(pieces, axis=-1)
        ms = jnp.mean(h * h, axis=-1, keepdims=True)
        out_ref[rows, :] = (h * lax.rsqrt(ms + RMS_EPS) * gain_ref[...]).astype(out_ref.dtype)

    assert per == 2 and n_chunks % 2 == 0
    half = (n_chunks // 2) * n_groups
    first, second = items[:half], items[half:]
    for _ in affine_maps(first):
        pass
    first_chunks = [first[c:c + n_groups] for c in range(0, len(first), n_groups)]
    for _ in affine_maps(second):
        if first_chunks:
            advance_chunk(first_chunks.pop(0))
    for chunk_items in first_chunks:
        advance_chunk(chunk_items)
    first_out = output_rows(0)
    for c in range(0, len(second), n_groups):
        advance_chunk(second[c:c + n_groups])
        next(first_out, None)
    for _ in first_out:
        pass
    for gi in range(n_groups):
        state_ref[gi] = states[gi]
    for _ in output_rows(1):
        pass


def _rwkv(rw, edec, gb, gn_gain, gn_bias, yat_first, yat_second, x2d, woa, wob, out_gain, batch, seq):
    tile, per = ROW_TILE, BLOCKS_PER_TILE
    nt = seq // tile
    assert nt % 2 == 0
    half = nt // 2
    d_model = x2d.shape[1]
    row_spec = lambda width: pl.BlockSpec((tile, width), lambda b, i: (b * nt + i, 0))
    const = lambda shape: pl.BlockSpec(shape, lambda b, i: (0,) * len(shape))
    vec = lambda width: const((1, width))
    blk_t = lambda pick: pl.BlockSpec((1, per, ATTN_WIDTH, MOBA_BLOCK),
                                      lambda b, i: (b, jnp.minimum(pick(i), half - 1), 0, 0))
    return pl.pallas_call(
        functools.partial(_rwkv_kernel, nb=nt),
        grid=(batch, nt),
        in_specs=[
            row_spec(RW_OPERANDS * RWKV_WIDTH),
            pl.BlockSpec((tile // CHUNK, 1, RWKV_WIDTH), lambda b, i: (b * nt + i, 0, 0)),
            row_spec(RWKV_WIDTH),
            vec(RWKV_WIDTH),
            vec(RWKV_WIDTH),
            blk_t(lambda i: i),
            blk_t(lambda i: nt - 1 - i),
            row_spec(d_model),
            const((ATTN_WIDTH, d_model)),
            const((RWKV_WIDTH, d_model)),
            vec(d_model),
        ],
        out_specs=row_spec(d_model),
        out_shape=jax.ShapeDtypeStruct(x2d.shape, x2d.dtype),
        scratch_shapes=[
            pltpu.VMEM((RWKV_WIDTH // GROUP_WIDTH, HEAD_DIM, GROUP_WIDTH), F32),
            pltpu.VMEM((tile, RWKV_WIDTH), F32),
        ],
        compiler_params=pltpu.CompilerParams(
            dimension_semantics=("arbitrary", "arbitrary"), vmem_limit_bytes=VMEM_LIMIT_BYTES),
        name="rwkv_out",
    )(rw, edec, gb, gn_gain, gn_bias, yat_first, yat_second, x2d, woa, wob, out_gain)


def _rotary_column_order():
    order = []
    for g in range(ATTN_WIDTH // GROUP_WIDTH):
        for part in range(2):
            for hh in range(GROUP_HEADS):
                head = g * GROUP_HEADS + hh
                order.extend(head * HEAD_DIM + part * HALF + j for j in range(HALF))
    return np.asarray(order, np.int32)


def _layer(h2d, batch, seq, norm_gain, w_in, shift_mu, decay_bias, decay_up, iclr_bias, iclr_up,
           k_k, k_a, r_k, gn_gain, gn_bias, w_out, out_gain, cos, sin):
    nb = seq // MOBA_BLOCK
    aw, rw = ATTN_WIDTH, RWKV_WIDTH
    order = _rotary_column_order()
    b0 = 4 * aw
    w_in = w_in.astype(BF16)
    wnn = jnp.concatenate([w_in[:, aw:2 * aw][:, order], w_in[:, b0:]], axis=1)
    wnt = jnp.concatenate([w_in[:, 0:aw][:, order], w_in[:, 2 * aw:4 * aw]], axis=1).T
    zeros = jnp.zeros((LORA_RANK, rw), F32)
    wl = jnp.concatenate([jnp.concatenate([decay_up.astype(F32), zeros], axis=1),
                          jnp.concatenate([zeros, iclr_up.astype(F32)], axis=1)], axis=0)
    wlora = wl.astype(BF16)
    row = lambda t: t.astype(F32).reshape(1, -1)

    cos_t = cos.reshape(nb, MOBA_BLOCK, 128).transpose(0, 2, 1)
    sin_t = sin.reshape(nb, MOBA_BLOCK, 128).transpose(0, 2, 1)
    lane = np.arange(aw)
    head_of_lane = (lane // GROUP_WIDTH) * GROUP_HEADS + (lane % 128) // HALF
    hsel = jnp.asarray(head_of_lane[:, None] == np.arange(128)[None, :], BF16)
    rw_vecs = [row(decay_bias), row(iclr_bias), row(k_k), row(k_a), row(r_k)]
    qt, k, kmean, kn2, vt, gat, rw, edec, gb = _in_proj(h2d, row(norm_gain), wnn, wnt, cos, sin, cos_t, sin_t,
                                                        row(shift_mu), hsel, wlora, rw_vecs, batch, nb)
    yat_first, yat_second = _moba(qt, k, kmean, kn2, vt, gat, batch, seq)
    w_out = w_out.astype(BF16)
    return _rwkv(rw, edec, gb, row(gn_gain), row(gn_bias), yat_first, yat_second, h2d, w_out[0:aw], w_out[aw:],
                 out_gain, batch, seq)


def kernel(x, norm_gain, w_in, shift_mu, decay_bias, decay_up, iclr_bias, iclr_up,
           k_k, k_a, r_k, gn_gain, gn_bias, w_out, final_gain):
    batch, seq, d_model = x.shape
    depth = norm_gain.shape[0]
    assert depth == 1, "the final RMSNorm is fused into the single layer's output projection"
    assert seq % MOBA_BLOCK == 0 and w_in.shape[-1] == 4 * ATTN_WIDTH + SHIFT_WIDTH + RWKV_WIDTH
    inv_freq = 1.0 / (ROPE_THETA ** (jnp.arange(0, HEAD_DIM, 2, dtype=F32) / HEAD_DIM))
    ang = jnp.arange(seq, dtype=F32)[:, None] * inv_freq[None, :]
    cos = jnp.tile(jnp.cos(ang), (1, GROUP_HEADS))
    sin = jnp.tile(jnp.sin(ang), (1, GROUP_HEADS))
    out = _layer(x.reshape(batch * seq, d_model).astype(F32), batch, seq, norm_gain[0], w_in[0],
                 shift_mu[0], decay_bias[0], decay_up[0], iclr_bias[0], iclr_up[0], k_k[0], k_a[0], r_k[0],
                 gn_gain[0], gn_bias[0], w_out[0], final_gain.astype(F32).reshape(1, -1), cos, sin)
    return out.reshape(batch, seq, d_model).astype(x.dtype)
```

```python
import functools

import jax
import jax.numpy as jnp
import numpy as np
from jax import lax
from jax.experimental import pallas as pl
from jax.experimental.pallas import tpu as pltpu

HEAD_DIM = 64
HALF = HEAD_DIM // 2
ATTN_HEADS = 8
ATTN_WIDTH = ATTN_HEADS * HEAD_DIM
RWKV_HEADS = 8
RWKV_WIDTH = RWKV_HEADS * HEAD_DIM
LORA_RANK = 64
SHIFT_WIDTH = 3 * RWKV_WIDTH + 2 * LORA_RANK
MOBA_BLOCK = 256
MOBA_TOPK = 3
ROPE_THETA = 10000.0
RMS_EPS = 1e-6
GN_EPS = 64e-5
NORMALIZE_EPS_SQ = 1e-24

GROUP_HEADS = 4
GROUP_WIDTH = GROUP_HEADS * HEAD_DIM
CHUNK = 64
ROW_TILE = 2 * MOBA_BLOCK
BLOCKS_PER_TILE = ROW_TILE // MOBA_BLOCK
RW_OPERANDS = 8
PAIRS_PER_STEP = 2
MASKED = -1e30
DECAY_SCALE = -float(np.exp(-0.5))
SCORE_BOUND_MARGIN = 1.0 + 2.0 ** -6
SCORE_BOUND_SAFE = 56.0
STAT_ROWS = 8
ACC_ROWS = HEAD_DIM + 16
LOG2E = 1.4426950408889634
Q_SCALE = HEAD_DIM ** -0.5 * LOG2E
VMEM_LIMIT_BYTES = 48 * 1024 * 1024

F32 = jnp.float32
BF16 = jnp.bfloat16


def _dot(a, b):
    return jnp.dot(a, b, preferred_element_type=F32)


def _dot_nt(a, b):
    return lax.dot_general(a, b, (((1,), (1,)), ((), ())), preferred_element_type=F32)


def _dot_tn(a, b):
    return lax.dot_general(a, b, (((0,), (0,)), ((), ())), preferred_element_type=F32)


def _split_bf16(x):
    hi = x.astype(BF16)
    lo = (x - hi.astype(F32)).astype(BF16)
    return hi, lo


def _silu(z):
    return z / (1.0 + jnp.exp(-z))


def _rotary(acc, cos, sin):
    outs = []
    for g in range(ATTN_WIDTH // GROUP_WIDTH):
        x1 = acc[:, g * GROUP_WIDTH:g * GROUP_WIDTH + 128]
        x2 = acc[:, g * GROUP_WIDTH + 128:(g + 1) * GROUP_WIDTH]
        outs.append(x1 * cos - x2 * sin)
        outs.append(x2 * cos + x1 * sin)
    return jnp.concatenate(outs, axis=-1)


def _rotary_t(acc, cos_t, sin_t):
    outs = []
    for g in range(ATTN_WIDTH // GROUP_WIDTH):
        x1 = acc[g * GROUP_WIDTH:g * GROUP_WIDTH + 128, :]
        x2 = acc[g * GROUP_WIDTH + 128:(g + 1) * GROUP_WIDTH, :]
        outs.append(x1 * cos_t - x2 * sin_t)
        outs.append(x2 * cos_t + x1 * sin_t)
    return jnp.concatenate(outs, axis=0)


def _head_sum(t):
    gr = lax.broadcasted_iota(jnp.int32, (GROUP_WIDTH, GROUP_WIDTH), 0) // HEAD_DIM
    gc = lax.broadcasted_iota(jnp.int32, (GROUP_WIDTH, GROUP_WIDTH), 1) // HEAD_DIM
    head_ones = jnp.where(gr == gc, 1.0, 0.0).astype(BF16)
    t = t.astype(BF16)
    return jnp.concatenate([_dot(t[:, i:i + GROUP_WIDTH], head_ones)
                            for i in range(0, t.shape[1], GROUP_WIDTH)], axis=-1)


def _in_proj_block(s, x_ref, gain_ref, wnn_ref, wnt_ref, cos_ref, sin_ref, cost_ref, sint_ref, mu_ref, hsel_ref,
                   wlora_ref, dbias_ref, ibias_ref, kk_ref, ka_ref, rk_ref,
                   qt_ref, k_ref, kmean_ref, kn2_ref, vt_ref, gat_ref, rw_ref, edec_ref, gb_ref, prev_ref):
    tile = MOBA_BLOCK
    rows = slice(s * tile, (s + 1) * tile)
    x = x_ref[rows, :]
    ms = jnp.mean(x * x, axis=-1, keepdims=True)
    u = (x * lax.rsqrt(ms + RMS_EPS) * gain_ref[...]).astype(BF16)

    k_att = _rotary(_dot(u, wnn_ref[:, 0:ATTN_WIDTH]), cos_ref[rows, :], sin_ref[rows, :])
    k_bf = k_att.astype(BF16)
    for g in range(ATTN_WIDTH // GROUP_WIDTH):
        k_ref[g, rows, :] = k_bf[:, g * GROUP_WIDTH:(g + 1) * GROUP_WIDTH]
    kmean_ref[s] = jnp.mean(k_att, axis=0, keepdims=True)
    k_sq = k_bf.astype(F32)
    k_sq = (k_sq * k_sq).astype(BF16)

    base = ATTN_WIDTH
    c_w = RWKV_WIDTH
    row = lax.broadcasted_iota(jnp.int32, (tile, 1), 0)

    def shifted_cols(c, w):
        p = _dot(u, wnn_ref[:, base + c:base + c + w])
        shifted = jnp.where(row == 0, prev_ref[0:1, c:c + w], pltpu.roll(p, 1, axis=0))
        prev_ref[0:1, c:c + w] = p[tile - 1:tile, :]
        return p + (shifted - p) * mu_ref[:, c:c + w]

    lora_in = shifted_cols(3 * c_w, 2 * LORA_RANK)
    lane128 = lax.broadcasted_iota(jnp.int32, (1, 2 * LORA_RANK), 1)
    lora_in = jnp.where(lane128 < LORA_RANK, jnp.tanh(lora_in), lora_in)
    li_hi, li_lo = _split_bf16(lora_in)
    k = shifted_cols(c_w, c_w)
    kk = k * kk_ref[...]
    r = shifted_cols(0, c_w)
    v = shifted_cols(2 * c_w, c_w)

    kn2_ref[s] = jnp.max(_dot(k_sq, hsel_ref[...]), axis=0, keepdims=True)
    w_hi = wlora_ref[...]
    lora = _dot(li_hi, w_hi) + _dot(li_lo, w_hi)
    kk = kk * lax.rsqrt(jnp.maximum(_head_sum(kk * kk), NORMALIZE_EPS_SQ))

    gb = _dot(u, wnn_ref[:, base + SHIFT_WIDTH:base + SHIFT_WIDTH + RWKV_WIDTH])
    gb_ref[rows, :] = _silu(gb).astype(BF16)

    g = DECAY_SCALE / (1.0 + jnp.exp(-(dbias_ref[...] + lora[:, 0:c_w])))
    a = 1.0 / (1.0 + jnp.exp(-(ibias_ref[...] + lora[:, c_w:2 * c_w])))
    g_hi, g_lo = _split_bf16(g)
    k2 = k * (1.0 + (a - 1.0) * ka_ref[...])
    b = kk * a

    qt = _rotary_t(_dot_nt(wnt_ref[0:ATTN_WIDTH, :], u), cost_ref[s], sint_ref[s])
    qt_ref[0, s] = (qt * Q_SCALE).astype(BF16)

    ti = lax.broadcasted_iota(jnp.int32, (CHUNK, CHUNK), 0)
    tj = lax.broadcasted_iota(jnp.int32, (CHUNK, CHUNK), 1)
    tri_incl = jnp.where(tj <= ti, 1.0, 0.0).astype(BF16)
    cums = [_dot(tri_incl, g_hi[c:c + CHUNK]) + _dot(tri_incl, g_lo[c:c + CHUNK])
            for c in range(0, tile, CHUNK)]
    bonus = _head_sum(r * k2 * rk_ref[...]) * v

    for i, cum in enumerate(cums):
        rs = slice(i * CHUNK, (i + 1) * CHUNK)
        dec_in = jnp.exp(cum)
        dec_ex = jnp.exp(cum - g[rs])
        inv = 1.0 / dec_in
        end_decay = dec_in[CHUNK - 1:CHUNK, :]
        to_end = inv * end_decay
        edec_ref[s * (tile // CHUNK) + i] = end_decay
        cols = (-kk[rs] * dec_ex, r[rs] * dec_in, b[rs] * inv, k2[rs] * inv,
                b[rs] * to_end, k2[rs] * to_end, v[rs], bonus[rs])
        for n, col in enumerate(cols):
            rw_ref[s * tile + i * CHUNK:s * tile + (i + 1) * CHUNK, n * c_w:(n + 1) * c_w] = col.astype(BF16)

    vt_ref[0, s] = _dot_nt(wnt_ref[ATTN_WIDTH:2 * ATTN_WIDTH, :], u).astype(BF16)
    gat = _dot_nt(wnt_ref[2 * ATTN_WIDTH:3 * ATTN_WIDTH, :], u)
    gat_ref[0, s] = _silu(gat).astype(BF16)


def _in_proj_kernel(x_ref, gain_ref, wnn_ref, wnt_ref, cos_ref, sin_ref, cost_ref, sint_ref, mu_ref, hsel_ref,
                    wlora_ref, dbias_ref, ibias_ref, kk_ref, ka_ref, rk_ref,
                    qt_ref, k_ref, kmean_ref, kn2_ref, vt_ref, gat_ref, rw_ref, edec_ref, gb_ref, prev_ref,
                    *, nb):
    @pl.when(pl.program_id(0) % (nb // BLOCKS_PER_TILE) == 0)
    def _():
        prev_ref[...] = jnp.zeros_like(prev_ref)

    for s in range(BLOCKS_PER_TILE):
        _in_proj_block(s, x_ref, gain_ref, wnn_ref, wnt_ref, cos_ref, sin_ref, cost_ref, sint_ref, mu_ref,
                       hsel_ref, wlora_ref, dbias_ref, ibias_ref, kk_ref, ka_ref, rk_ref,
                       qt_ref, k_ref, kmean_ref, kn2_ref, vt_ref, gat_ref, rw_ref, edec_ref, gb_ref, prev_ref)


def _in_proj(x2d, gain, wnn, wnt, cos, sin, cos_t, sin_t, mu, hsel, wlora, rw_vecs, batch, nb):
    rows, d_model = x2d.shape
    tile, per = ROW_TILE, BLOCKS_PER_TILE
    assert nb % per == 0
    nt = nb // per
    row_spec = lambda width: pl.BlockSpec((tile, width), lambda i: (i, 0))
    const = lambda shape: pl.BlockSpec(shape, lambda i: (0,) * len(shape))
    blk_t = pl.BlockSpec((1, per, ATTN_WIDTH, MOBA_BLOCK), lambda i: (i // nt, i % nt, 0, 0))
    blk_t_shape = jax.ShapeDtypeStruct((batch, nb, ATTN_WIDTH, MOBA_BLOCK), BF16)
    return pl.pallas_call(
        functools.partial(_in_proj_kernel, nb=nb),
        grid=(rows // tile,),
        in_specs=[
            row_spec(d_model),
            const((1, d_model)),
            const(wnn.shape),
            const(wnt.shape),
            pl.BlockSpec((tile, 128), lambda i: (i % nt, 0)),
            pl.BlockSpec((tile, 128), lambda i: (i % nt, 0)),
            pl.BlockSpec((per, 128, MOBA_BLOCK), lambda i: (i % nt, 0, 0)),
            pl.BlockSpec((per, 128, MOBA_BLOCK), lambda i: (i % nt, 0, 0)),
            const((1, SHIFT_WIDTH)),
            const(hsel.shape),
            const(wlora.shape),
        ] + [const((1, RWKV_WIDTH))] * len(rw_vecs),
        out_specs=[
            blk_t,
            pl.BlockSpec((ATTN_WIDTH // GROUP_WIDTH, tile, GROUP_WIDTH), lambda i: (0, i, 0)),
            pl.BlockSpec((per, 1, ATTN_WIDTH), lambda i: (i, 0, 0)),
            pl.BlockSpec((per, 1, 128), lambda i: (i, 0, 0)),
            blk_t,
            blk_t,
            row_spec(RW_OPERANDS * RWKV_WIDTH),
            pl.BlockSpec((tile // CHUNK, 1, RWKV_WIDTH), lambda i: (i, 0, 0)),
            row_spec(RWKV_WIDTH),
        ],
        out_shape=[
            blk_t_shape,
            jax.ShapeDtypeStruct((ATTN_WIDTH // GROUP_WIDTH, rows, GROUP_WIDTH), BF16),
            jax.ShapeDtypeStruct((rows // MOBA_BLOCK, 1, ATTN_WIDTH), F32),
            jax.ShapeDtypeStruct((rows // MOBA_BLOCK, 1, 128), F32),
            blk_t_shape,
            blk_t_shape,
            jax.ShapeDtypeStruct((rows, RW_OPERANDS * RWKV_WIDTH), BF16),
            jax.ShapeDtypeStruct((rows // CHUNK, 1, RWKV_WIDTH), F32),
            jax.ShapeDtypeStruct((rows, RWKV_WIDTH), BF16),
        ],
        scratch_shapes=[pltpu.VMEM((8, SHIFT_WIDTH), F32)],
        compiler_params=pltpu.CompilerParams(
            dimension_semantics=("arbitrary",), vmem_limit_bytes=VMEM_LIMIT_BYTES),
        name="in_proj",
    )(x2d, gain, wnn, wnt, cos, sin, cos_t, sin_t, mu, hsel, wlora, *rw_vecs)


def _moba_kernel(qta_ref, qtb_ref, k_ref, kmean_ref, kn2_ref, vt_ref, gata_ref, gatb_ref, outa_ref, outb_ref,
                 qh_ref, bias_ref, m_ref, acc_ref, *, nb):
    for p in range(PAIRS_PER_STEP):
        q = PAIRS_PER_STEP - 1 - p
        _moba_pair(pl.program_id(2) * PAIRS_PER_STEP + p,
                   (qta_ref.at[0, p], gata_ref.at[0, p], outa_ref.at[0, p]),
                   (qtb_ref.at[0, q], gatb_ref.at[0, q], outb_ref.at[0, p]),
                   k_ref, kmean_ref, kn2_ref, vt_ref, qh_ref, bias_ref, m_ref, acc_ref, nb=nb)


def _moba_pair(j, tile_a, tile_b, k_ref, kmean_ref, kn2_ref, vt_ref, qh_ref, bias_ref, m_ref, acc_ref, *, nb):
    blk = MOBA_BLOCK
    tiles = (tile_a + (j,), tile_b + (nb - 1 - j,))
    feat = lax.broadcasted_iota(jnp.int32, (GROUP_WIDTH, 1), 0)
    head_of_feat = (feat % 128) // HALF
    ones_rows = jnp.ones((ACC_ROWS - HEAD_DIM, blk), BF16)
    km = kmean_ref[0]
    km_lane_head = (lax.broadcasted_iota(jnp.int32, (1, GROUP_WIDTH), 1) % 128) // HALF
    km_heads = jnp.concatenate([jnp.where(km_lane_head == h, km, 0.0) for h in range(GROUP_HEADS)], axis=0)
    km_hi, km_lo = _split_bf16(km_heads)
    blk_id = lax.broadcasted_iota(jnp.int32, (GROUP_HEADS, nb, blk), 1)
    k_pos = lax.broadcasted_iota(jnp.int32, (blk, blk), 0)
    q_pos = lax.broadcasted_iota(jnp.int32, (blk, blk), 1)
    causal = k_pos <= q_pos
    kn2 = kn2_ref[0]
    kn2_row = lax.broadcasted_iota(jnp.int32, kn2.shape, 0)
    kn2_lane = lax.broadcasted_iota(jnp.int32, (1, kn2.shape[1]), 1)
    first_head = pl.program_id(1) * GROUP_HEADS

    def values_ext(ki, h):
        return jnp.concatenate([vt_ref[0, ki, h * HEAD_DIM:(h + 1) * HEAD_DIM, :], ones_rows], axis=0)

    bounds = []
    for t, (qt_ref, _, _, qi) in enumerate(tiles):
        qt = qt_ref[...]
        q_sq = qt.astype(F32)
        q_sq = q_sq * q_sq
        kn2_seen = jnp.max(jnp.where(kn2_row <= qi, kn2, 0.0), axis=0, keepdims=True)
        for h in range(GROUP_HEADS):
            qn2 = (jnp.sum(q_sq[h * HALF:(h + 1) * HALF], axis=0, keepdims=True)
                   + jnp.sum(q_sq[128 + h * HALF:128 + (h + 1) * HALF], axis=0, keepdims=True))
            kn2_h = jnp.sum(jnp.where(kn2_lane == first_head + h, kn2_seen, 0.0), axis=1, keepdims=True)
            bounds.append(jnp.sqrt(kn2_h * qn2) * SCORE_BOUND_MARGIN)
            qh_ref[t, h] = jnp.where(head_of_feat == h, qt, jnp.zeros_like(qt))
        gate = (_dot(km_hi, qt) + _dot(km_lo, qt)).reshape(GROUP_HEADS, nb, blk)
        gate = jnp.where(blk_id < qi, gate, -jnp.inf)
        bias = jnp.full(gate.shape, MASKED, F32)
        for r in range(min(MOBA_TOPK, nb)):
            top = jnp.max(gate, axis=1, keepdims=True)
            idx = jnp.min(jnp.where(gate == top, blk_id, nb), axis=1, keepdims=True)
            pick = blk_id == idx + jnp.where(r < qi, 0, nb)
            bias = jnp.where(pick, 0.0, bias)
            gate = jnp.where(pick, -jnp.inf, gate)
        bias_ref[t] = bias

    def issue_scores(t, ki):
        kb = k_ref[0, 0, pl.ds(pl.multiple_of(ki * blk, blk), blk), :]
        return [_dot(kb, qh_ref[t, h]) for h in range(GROUP_HEADS)]

    def finish_own(t, ki, scores):
        for h in range(GROUP_HEADS):
            s = jnp.where(causal, scores[h], MASKED)
            m = jnp.max(s, axis=0, keepdims=True)
            p = jnp.exp2(s - m).astype(BF16)
            m_ref[t, h] = jnp.broadcast_to(m, (STAT_ROWS, blk))
            acc_ref[t, h] = _dot(values_ext(ki, h), p)

    def finish_past(t, ki, scores):
        for h in range(GROUP_HEADS):
            s = scores[h]
            bias = bias_ref[t, h, pl.ds(ki, 1), :]
            m_old = m_ref[t, h, 0:1, :]
            m_new = jnp.maximum(m_old, jnp.max(s, axis=0, keepdims=True) + bias)
            alpha = jnp.exp2(m_old - m_new)
            p = jnp.exp2(s - (m_new - bias)).astype(BF16)
            m_ref[t, h] = jnp.broadcast_to(m_new, (STAT_ROWS, blk))
            acc_ref[t, h] = alpha * acc_ref[t, h] + _dot(values_ext(ki, h), p)

    def bounded_own(t, ki, scores):
        for h in range(GROUP_HEADS):
            s = jnp.where(causal, scores[h], MASKED)
            p = jnp.exp2(s - m_ref[t, h, 0:1, :]).astype(BF16)
            acc_ref[t, h] = _dot(values_ext(ki, h), p)

    def bounded_past(t, ki, scores):
        for h in range(GROUP_HEADS):
            bias = bias_ref[t, h, pl.ds(ki, 1), :]
            p = jnp.exp2(scores[h] - (m_ref[t, h, 0:1, :] - bias)).astype(BF16)
            acc_ref[t, h] = acc_ref[t, h] + _dot(values_ext(ki, h), p)

    def attend(own, past):
        work = [(own, 0, j), (own, 1, nb - 1 - j)]
        for n in range(nb - 1):
            second = (n >= j).astype(jnp.int32)
            work.append((past, second, n - second * j))
        pending = None
        for fin, t, ki in work:
            scores = issue_scores(t, ki)
            if pending is not None:
                pending[0](*pending[1:])
            pending = (fin, t, ki, scores)
        pending[0](*pending[1:])

    bound_ok = jnp.max(jnp.concatenate(bounds, axis=0)) < SCORE_BOUND_SAFE

    @pl.when(bound_ok)
    def _():
        for i, bound in enumerate(bounds):
            m_ref[i // GROUP_HEADS, i % GROUP_HEADS] = jnp.broadcast_to(bound, (STAT_ROWS, blk))
        attend(bounded_own, bounded_past)

    @pl.when(jnp.logical_not(bound_ok))
    def _():
        attend(finish_own, finish_past)

    for t, (_, gat_ref, out_ref, _) in enumerate(tiles):
        for h in range(GROUP_HEADS):
            rows = slice(h * HEAD_DIM, (h + 1) * HEAD_DIM)
            acc = acc_ref[t, h]
            y = acc[0:HEAD_DIM] / acc[HEAD_DIM:HEAD_DIM + 1]
            out_ref[rows, :] = (y * gat_ref[rows, :].astype(F32)).astype(BF16)


def _moba(qt, k, kmean, kn2, vt, gat, batch, seq):
    nb = seq // MOBA_BLOCK
    pp = PAIRS_PER_STEP
    assert nb % (2 * pp) == 0
    steps = nb // (2 * pp)
    blk = MOBA_BLOCK
    groups = ATTN_WIDTH // GROUP_WIDTH
    k4 = k.reshape(groups, batch, seq, GROUP_WIDTH)
    km3 = kmean.reshape(batch, nb, ATTN_WIDTH)
    kn3 = kn2.reshape(batch, nb, 128)
    tile_a = pl.BlockSpec((1, pp, GROUP_WIDTH, blk), lambda b, g, j: (b, j, g, 0))
    tile_b = pl.BlockSpec((1, pp, GROUP_WIDTH, blk), lambda b, g, j: (b, nb // pp - 1 - j, g, 0))
    out_shape = jax.ShapeDtypeStruct((batch, nb // 2, ATTN_WIDTH, blk), BF16)
    return pl.pallas_call(
        functools.partial(_moba_kernel, nb=nb),
        grid=(batch, groups, steps),
        in_specs=[
            tile_a,
            tile_b,
            pl.BlockSpec((1, 1, seq, GROUP_WIDTH), lambda b, g, j: (g, b, 0, 0)),
            pl.BlockSpec((1, nb, GROUP_WIDTH), lambda b, g, j: (b, 0, g)),
            pl.BlockSpec((1, nb, 128), lambda b, g, j: (b, 0, 0)),
            pl.BlockSpec((1, nb, GROUP_WIDTH, blk), lambda b, g, j: (b, 0, g, 0)),
            tile_a,
            tile_b,
        ],
        out_specs=[tile_a, tile_a],
        out_shape=[out_shape, out_shape],
        scratch_shapes=[
            pltpu.VMEM((2, GROUP_HEADS, GROUP_WIDTH, blk), BF16),
            pltpu.VMEM((2, GROUP_HEADS, nb, blk), F32),
            pltpu.VMEM((2, GROUP_HEADS, STAT_ROWS, blk), F32),
            pltpu.VMEM((2, GROUP_HEADS, ACC_ROWS, blk), F32),
        ],
        compiler_params=pltpu.CompilerParams(
            dimension_semantics=("arbitrary", "arbitrary", "arbitrary"), vmem_limit_bytes=VMEM_LIMIT_BYTES),
        name="moba",
    )(qt, qt, k4, km3, kn3, vt, gat, gat)


def _block_diag(x_bf16, mask):
    tiled = jnp.concatenate([x_bf16] * GROUP_HEADS, axis=0)
    return jnp.where(mask, tiled, jnp.zeros_like(tiled))


def _rwkv_kernel(rw_ref, edec_ref, gb_ref, gng_ref, gnb_ref, yat1_ref, yat2_ref, x_ref, woa_ref, wob_ref,
                 gain_ref, out_ref, state_ref, y_ref, *, nb):
    tile = rw_ref.shape[0]
    c_w = RWKV_WIDTH

    @pl.when(pl.program_id(1) == 0)
    def _():
        state_ref[...] = jnp.zeros_like(state_ref)

    gr = lax.broadcasted_iota(jnp.int32, (GROUP_WIDTH, GROUP_WIDTH), 0) // HEAD_DIM
    gc = lax.broadcasted_iota(jnp.int32, (GROUP_WIDTH, GROUP_WIDTH), 1) // HEAD_DIM
    diag_mask = gr == gc
    t_idx = lax.broadcasted_iota(jnp.int32, (CHUNK, GROUP_WIDTH), 0)
    j_idx = lax.broadcasted_iota(jnp.int32, (CHUNK, GROUP_WIDTH), 1) % HEAD_DIM
    strict_lower = j_idx < t_idx
    lower = j_idx <= t_idx
    eye = (j_idx == t_idx).astype(F32)

    def below_diag(size):
        return ((t_idx // (2 * size) == j_idx // (2 * size))
                & ((t_idx // size) % 2 == 1) & ((j_idx // size) % 2 == 0))
    lane_head = lax.broadcasted_iota(jnp.int32, (1, GROUP_WIDTH), 1) // HEAD_DIM

    def bd(x):
        return _block_diag(x.astype(BF16), diag_mask)

    def diag_blocks(full):
        out = full[(GROUP_HEADS - 1) * HEAD_DIM:, :]
        for h in range(GROUP_HEADS - 2, -1, -1):
            out = jnp.where(lane_head == h, full[h * HEAD_DIM:(h + 1) * HEAD_DIM, :], out)
        return out

    n_groups = c_w // GROUP_WIDTH
    items = []
    for c in range(tile // CHUNK):
        rs = slice(c * CHUNK, (c + 1) * CHUNK)
        end_decay = edec_ref[c]
        for gi in range(n_groups):
            ls = slice(gi * GROUP_WIDTH, (gi + 1) * GROUP_WIDTH)
            operand = lambda n: rw_ref[rs, n * c_w + gi * GROUP_WIDTH:n * c_w + (gi + 1) * GROUP_WIDTH]
            it = dict(rs=rs, ls=ls, gi=gi)
            it["a_t"] = operand(0)
            it["r_t"] = operand(1)
            it["bk_e"] = jnp.concatenate([operand(4), operand(5)], axis=0)
            it["v_c"] = operand(6)
            it["m_diag"] = eye * end_decay[:, ls]
            ar = jnp.concatenate([it["a_t"], it["r_t"]], axis=0)
            pb = _dot_nt(ar, _block_diag(operand(2), diag_mask))
            pk = _dot_nt(ar, _block_diag(operand(3), diag_mask))
            it["n_ab"] = jnp.where(strict_lower, pb[0:CHUNK], 0.0)
            it["t_inv"] = eye + jnp.where(below_diag(1), it["n_ab"], 0.0)
            it["a_rb"] = jnp.where(lower, pb[CHUNK:], 0.0).astype(BF16)
            it["a_k"] = jnp.concatenate([jnp.where(strict_lower, pk[0:CHUNK], 0.0),
                                         jnp.where(lower, pk[CHUNK:], 0.0)], axis=0).astype(BF16)
            items.append(it)

    def affine_maps(its):
        size = 2
        while size < CHUNK:
            for it in its:
                n21 = jnp.where(below_diag(size), it["n_ab"], 0.0)
                it["t_n21"] = _dot(it["t_inv"].astype(BF16), bd(n21))
            yield
            for it in its:
                it["t_inv"] = it["t_inv"] + _dot(it["t_n21"].astype(BF16), bd(it["t_inv"]))
            yield
            size *= 2
        for it in its:
            it["t_inv"] = it["t_inv"].astype(BF16)
        for it in its:
            kv = _dot(it["a_k"], bd(it["v_c"]))
            it["akv"] = kv[0:CHUNK]
            it["arkv"] = kv[CHUNK:]
        yield
        for it in its:
            it["w_t"] = _dot(it["t_inv"], bd(it["a_t"])).astype(BF16)
            it["u_t"] = _dot(it["t_inv"], bd(it["akv"])).astype(BF16)
        yield
        for it in its:
            it["r_hat"] = (it["r_t"].astype(F32) + _dot(it["a_rb"], bd(it["w_t"]))).astype(BF16)
            it["y_hat"] = _dot(it["a_rb"], bd(it["u_t"])) + it["arkv"]
        yield
        for it in its:
            m_full = _dot_tn(it["bk_e"][0:CHUNK], it["w_t"])
            it["m"] = (it["m_diag"] + diag_blocks(m_full)).astype(BF16)
            n_full = _dot_tn(it["bk_e"], jnp.concatenate([it["u_t"], it["v_c"]], axis=0))
            it["n"] = diag_blocks(n_full)
        yield

    in_first = pl.program_id(1) < nb // 2
    per = yat1_ref.shape[1]
    n_chunks = tile // CHUNK
    states = [state_ref[gi] for gi in range(n_groups)]

    def advance_chunk(chunk_items):
        for it in chunk_items:
            gi = it["gi"]
            res = _dot(jnp.concatenate([it["m"], it["r_hat"]], axis=0), bd(states[gi]))
            y_ref[it["rs"], it["ls"]] = res[CHUNK:] + it["y_hat"]
            states[gi] = res[0:CHUNK] + it["n"]

    def output_rows(blk):
        rows = slice(blk * MOBA_BLOCK, (blk + 1) * MOBA_BLOCK)
        y = y_ref[rows, :]
        mean = _head_sum(y) * (1.0 / HEAD_DIM)
        d = y - mean
        var = _head_sum(d * d) * (1.0 / HEAD_DIM)
        yn = d * lax.rsqrt(var + GN_EPS) * gng_ref[...] + gnb_ref[...]
        bonus = rw_ref[rows, (RW_OPERANDS - 1) * c_w:].astype(F32)
        yb = ((yn + bonus) * gb_ref[rows, :].astype(F32)).astype(BF16)
        yat_t = jnp.transpose(jnp.where(in_first, yat1_ref[0, blk], yat2_ref[0, per - 1 - blk]))
        pieces = []
        for c0 in range(0, woa_ref.shape[1], GROUP_WIDTH):
            cols = slice(c0, c0 + GROUP_WIDTH)
            pieces.append(x_ref[rows, cols] + _dot(yat_t, woa_ref[:, cols]) + _dot(yb, wob_ref[:, cols]))
            yield
        h = jnp.concatenate(pieces, axis=-1)
        ms = jnp.mean(h * h, axis=-1, keepdims=True)
        out_ref[rows, :] = (h * lax.rsqrt(ms + RMS_EPS) * gain_ref[...]).astype(out_ref.dtype)

    assert per == 2 and n_chunks % 2 == 0
    half = (n_chunks // 2) * n_groups
    first, second = items[:half], items[half:]
    for _ in affine_maps(first):
        pass
    first_chunks = [first[c:c + n_groups] for c in range(0, len(first), n_groups)]
    for _ in affine_maps(second):
        if first_chunks:
            advance_chunk(first_chunks.pop(0))
    for chunk_items in first_chunks:
        advance_chunk(chunk_items)
    first_out = output_rows(0)
    for c in range(0, len(second), n_groups):
        advance_chunk(second[c:c + n_groups])
        next(first_out, None)
    for _ in first_out:
        pass
    for gi in range(n_groups):
        state_ref[gi] = states[gi]
    for _ in output_rows(1):
        pass


def _rwkv(rw, edec, gb, gn_gain, gn_bias, yat_first, yat_second, x2d, woa, wob, out_gain, batch, seq):
    tile, per = ROW_TILE, BLOCKS_PER_TILE
    nt = seq // tile
    assert nt % 2 == 0
    half = nt // 2
    d_model = x2d.shape[1]
    row_spec = lambda width: pl.BlockSpec((tile, width), lambda b, i: (b * nt + i, 0))
    const = lambda shape: pl.BlockSpec(shape, lambda b, i: (0,) * len(shape))
    vec = lambda width: const((1, width))
    blk_t = lambda pick: pl.BlockSpec((1, per, ATTN_WIDTH, MOBA_BLOCK),
                                      lambda b, i: (b, jnp.minimum(pick(i), half - 1), 0, 0))
    return pl.pallas_call(
        functools.partial(_rwkv_kernel, nb=nt),
        grid=(batch, nt),
        in_specs=[
            row_spec(RW_OPERANDS * RWKV_WIDTH),
            pl.BlockSpec((tile // CHUNK, 1, RWKV_WIDTH), lambda b, i: (b * nt + i, 0, 0)),
            row_spec(RWKV_WIDTH),
            vec(RWKV_WIDTH),
            vec(RWKV_WIDTH),
            blk_t(lambda i: i),
            blk_t(lambda i: nt - 1 - i),
            row_spec(d_model),
            const((ATTN_WIDTH, d_model)),
            const((RWKV_WIDTH, d_model)),
            vec(d_model),
        ],
        out_specs=row_spec(d_model),
        out_shape=jax.ShapeDtypeStruct(x2d.shape, x2d.dtype),
        scratch_shapes=[
            pltpu.VMEM((RWKV_WIDTH // GROUP_WIDTH, HEAD_DIM, GROUP_WIDTH), F32),
            pltpu.VMEM((tile, RWKV_WIDTH), F32),
        ],
        compiler_params=pltpu.CompilerParams(
            dimension_semantics=("arbitrary", "arbitrary"), vmem_limit_bytes=VMEM_LIMIT_BYTES),
        name="rwkv_out",
    )(rw, edec, gb, gn_gain, gn_bias, yat_first, yat_second, x2d, woa, wob, out_gain)


def _rotary_column_order():
    order = []
    for g in range(ATTN_WIDTH // GROUP_WIDTH):
        for part in range(2):
            for hh in range(GROUP_HEADS):
                head = g * GROUP_HEADS + hh
                order.extend(head * HEAD_DIM + part * HALF + j for j in range(HALF))
    return np.asarray(order, np.int32)


def _layer(h2d, batch, seq, norm_gain, w_in, shift_mu, decay_bias, decay_up, iclr_bias, iclr_up,
           k_k, k_a, r_k, gn_gain, gn_bias, w_out, out_gain, cos, sin):
    nb = seq // MOBA_BLOCK
    aw, rw = ATTN_WIDTH, RWKV_WIDTH
    order = _rotary_column_order()
    b0 = 4 * aw
    w_in = w_in.astype(BF16)
    wnn = jnp.concatenate([w_in[:, aw:2 * aw][:, order], w_in[:, b0:]], axis=1)
    wnt = jnp.concatenate([w_in[:, 0:aw][:, order], w_in[:, 2 * aw:4 * aw]], axis=1).T
    zeros = jnp.zeros((LORA_RANK, rw), F32)
    wl = jnp.concatenate([jnp.concatenate([decay_up.astype(F32), zeros], axis=1),
                          jnp.concatenate([zeros, iclr_up.astype(F32)], axis=1)], axis=0)
    wlora = wl.astype(BF16)
    row = lambda t: t.astype(F32).reshape(1, -1)

    cos_t = cos.reshape(nb, MOBA_BLOCK, 128).transpose(0, 2, 1)
    sin_t = sin.reshape(nb, MOBA_BLOCK, 128).transpose(0, 2, 1)
    lane = np.arange(aw)
    head_of_lane = (lane // GROUP_WIDTH) * GROUP_HEADS + (lane % 128) // HALF
    hsel = jnp.asarray(head_of_lane[:, None] == np.arange(128)[None, :], BF16)
    rw_vecs = [row(decay_bias), row(iclr_bias), row(k_k), row(k_a), row(r_k)]
    qt, k, kmean, kn2, vt, gat, rw, edec, gb = _in_proj(h2d, row(norm_gain), wnn, wnt, cos, sin, cos_t, sin_t,
                                                        row(shift_mu), hsel, wlora, rw_vecs, batch, nb)
    yat_first, yat_second = _moba(qt, k, kmean, kn2, vt, gat, batch, seq)
    w_out = w_out.astype(BF16)
    return _rwkv(rw, edec, gb, row(gn_gain), row(gn_bias), yat_first, yat_second, h2d, w_out[0:aw], w_out[aw:],
                 out_gain, batch, seq)


def kernel(x, norm_gain, w_in, shift_mu, decay_bias, decay_up, iclr_bias, iclr_up,
           k_k, k_a, r_k, gn_gain, gn_bias, w_out, final_gain):
    batch, seq, d_model = x.shape
    depth = norm_gain.shape[0]
    assert depth == 1, "the final RMSNorm is fused into the single layer's output projection"
    assert seq % MOBA_BLOCK == 0 and w_in.shape[-1] == 4 * ATTN_WIDTH + SHIFT_WIDTH + RWKV_WIDTH
    inv_freq = 1.0 / (ROPE_THETA ** (jnp.arange(0, HEAD_DIM, 2, dtype=F32) / HEAD_DIM))
    ang = jnp.arange(seq, dtype=F32)[:, None] * inv_freq[None, :]
    cos = jnp.tile(jnp.cos(ang), (1, GROUP_HEADS))
    sin = jnp.tile(jnp.sin(ang), (1, GROUP_HEADS))
    out = _layer(x.reshape(batch * seq, d_model).astype(F32), batch, seq, norm_gain[0], w_in[0],
                 shift_mu[0], decay_bias[0], decay_up[0], iclr_bias[0], iclr_up[0], k_k[0], k_a[0], r_k[0],
                 gn_gain[0], gn_bias[0], w_out[0], final_gain.astype(F32).reshape(1, -1), cos, sin)
    return out.reshape(batch, seq, d_model).astype(x.dtype)
```

```python
import functools

import jax
import jax.numpy as jnp
import numpy as np
from jax import lax
from jax.experimental import pallas as pl
from jax.experimental.pallas import tpu as pltpu

HEAD_DIM = 64
HALF = HEAD_DIM // 2
ATTN_HEADS = 8
ATTN_WIDTH = ATTN_HEADS * HEAD_DIM
RWKV_HEADS = 8
RWKV_WIDTH = RWKV_HEADS * HEAD_DIM
LORA_RANK = 64
SHIFT_WIDTH = 3 * RWKV_WIDTH + 2 * LORA_RANK
MOBA_BLOCK = 256
MOBA_TOPK = 3
ROPE_THETA = 10000.0
RMS_EPS = 1e-6
GN_EPS = 64e-5
NORMALIZE_EPS_SQ = 1e-24

GROUP_HEADS = 4
GROUP_WIDTH = GROUP_HEADS * HEAD_DIM
CHUNK = 64
ROW_TILE = 2 * MOBA_BLOCK
BLOCKS_PER_TILE = ROW_TILE // MOBA_BLOCK
SERIES_SAFE_PEAK = 2.0
RW_OPERANDS = 8
MASKED = -1e30
DECAY_SCALE = -float(np.exp(-0.5))
SCORE_BOUND_MARGIN = 1.0 + 2.0 ** -6
SCORE_BOUND_SAFE = 56.0
STAT_ROWS = 8
ACC_ROWS = HEAD_DIM + 16
LOG2E = 1.4426950408889634
Q_SCALE = HEAD_DIM ** -0.5 * LOG2E
VMEM_LIMIT_BYTES = 48 * 1024 * 1024

F32 = jnp.float32
BF16 = jnp.bfloat16


def _dot(a, b):
    return jnp.dot(a, b, preferred_element_type=F32)


def _dot_nt(a, b):
    return lax.dot_general(a, b, (((1,), (1,)), ((), ())), preferred_element_type=F32)


def _dot_tn(a, b):
    return lax.dot_general(a, b, (((0,), (0,)), ((), ())), preferred_element_type=F32)


def _split_bf16(x):
    hi = x.astype(BF16)
    lo = (x - hi.astype(F32)).astype(BF16)
    return hi, lo


def _silu(z):
    return z / (1.0 + jnp.exp(-z))


def _rotary(acc, cos, sin):
    outs = []
    for g in range(ATTN_WIDTH // GROUP_WIDTH):
        x1 = acc[:, g * GROUP_WIDTH:g * GROUP_WIDTH + 128]
        x2 = acc[:, g * GROUP_WIDTH + 128:(g + 1) * GROUP_WIDTH]
        outs.append(x1 * cos - x2 * sin)
        outs.append(x2 * cos + x1 * sin)
    return jnp.concatenate(outs, axis=-1)


def _rotary_t(acc, cos_t, sin_t):
    outs = []
    for g in range(ATTN_WIDTH // GROUP_WIDTH):
        x1 = acc[g * GROUP_WIDTH:g * GROUP_WIDTH + 128, :]
        x2 = acc[g * GROUP_WIDTH + 128:(g + 1) * GROUP_WIDTH, :]
        outs.append(x1 * cos_t - x2 * sin_t)
        outs.append(x2 * cos_t + x1 * sin_t)
    return jnp.concatenate(outs, axis=0)


def _head_sum(t):
    gr = lax.broadcasted_iota(jnp.int32, (GROUP_WIDTH, GROUP_WIDTH), 0) // HEAD_DIM
    gc = lax.broadcasted_iota(jnp.int32, (GROUP_WIDTH, GROUP_WIDTH), 1) // HEAD_DIM
    head_ones = jnp.where(gr == gc, 1.0, 0.0).astype(BF16)
    t = t.astype(BF16)
    return jnp.concatenate([_dot(t[:, i:i + GROUP_WIDTH], head_ones)
                            for i in range(0, t.shape[1], GROUP_WIDTH)], axis=-1)


def _in_proj_block(s, x_ref, gain_ref, wnn_ref, wnt_ref, cos_ref, sin_ref, cost_ref, sint_ref, mu_ref, hsel_ref,
                   wlora_ref, dbias_ref, ibias_ref, kk_ref, ka_ref, rk_ref,
                   qt_ref, k_ref, kmean_ref, kn2_ref, vt_ref, gat_ref, rw_ref, edec_ref, gb_ref, prev_ref):
    tile = MOBA_BLOCK
    rows = slice(s * tile, (s + 1) * tile)
    x = x_ref[rows, :]
    ms = jnp.mean(x * x, axis=-1, keepdims=True)
    u = (x * lax.rsqrt(ms + RMS_EPS) * gain_ref[...]).astype(BF16)

    k_att = _rotary(_dot(u, wnn_ref[:, 0:ATTN_WIDTH]), cos_ref[rows, :], sin_ref[rows, :])
    k_bf = k_att.astype(BF16)
    for g in range(ATTN_WIDTH // GROUP_WIDTH):
        k_ref[g, rows, :] = k_bf[:, g * GROUP_WIDTH:(g + 1) * GROUP_WIDTH]
    kmean_ref[s] = jnp.mean(k_att, axis=0, keepdims=True)
    k_sq = k_bf.astype(F32)
    k_sq = (k_sq * k_sq).astype(BF16)

    base = ATTN_WIDTH
    c_w = RWKV_WIDTH
    row = lax.broadcasted_iota(jnp.int32, (tile, 1), 0)

    def shifted_cols(c, w):
        p = _dot(u, wnn_ref[:, base + c:base + c + w])
        shifted = jnp.where(row == 0, prev_ref[0:1, c:c + w], pltpu.roll(p, 1, axis=0))
        prev_ref[0:1, c:c + w] = p[tile - 1:tile, :]
        return p + (shifted - p) * mu_ref[:, c:c + w]

    lora_in = shifted_cols(3 * c_w, 2 * LORA_RANK)
    lane128 = lax.broadcasted_iota(jnp.int32, (1, 2 * LORA_RANK), 1)
    lora_in = jnp.where(lane128 < LORA_RANK, jnp.tanh(lora_in), lora_in)
    li_hi, li_lo = _split_bf16(lora_in)
    k = shifted_cols(c_w, c_w)
    kk = k * kk_ref[...]
    r = shifted_cols(0, c_w)
    v = shifted_cols(2 * c_w, c_w)

    kn2_ref[s] = jnp.max(_dot(k_sq, hsel_ref[...]), axis=0, keepdims=True)
    w_hi = wlora_ref[...]
    lora = _dot(li_hi, w_hi) + _dot(li_lo, w_hi)
    kk = kk * lax.rsqrt(jnp.maximum(_head_sum(kk * kk), NORMALIZE_EPS_SQ))

    gb = _dot(u, wnn_ref[:, base + SHIFT_WIDTH:base + SHIFT_WIDTH + RWKV_WIDTH])
    gb_ref[rows, :] = _silu(gb).astype(BF16)

    g = DECAY_SCALE / (1.0 + jnp.exp(-(dbias_ref[...] + lora[:, 0:c_w])))
    a = 1.0 / (1.0 + jnp.exp(-(ibias_ref[...] + lora[:, c_w:2 * c_w])))
    g_hi, g_lo = _split_bf16(g)
    k2 = k * (1.0 + (a - 1.0) * ka_ref[...])
    b = kk * a

    qt = _rotary_t(_dot_nt(wnt_ref[0:ATTN_WIDTH, :], u), cost_ref[s], sint_ref[s])
    qt_ref[0, s] = (qt * Q_SCALE).astype(BF16)

    ti = lax.broadcasted_iota(jnp.int32, (CHUNK, CHUNK), 0)
    tj = lax.broadcasted_iota(jnp.int32, (CHUNK, CHUNK), 1)
    tri_incl = jnp.where(tj <= ti, 1.0, 0.0).astype(BF16)
    cums = [_dot(tri_incl, g_hi[c:c + CHUNK]) + _dot(tri_incl, g_lo[c:c + CHUNK])
            for c in range(0, tile, CHUNK)]
    bonus = _head_sum(r * k2 * rk_ref[...]) * v

    for i, cum in enumerate(cums):
        rs = slice(i * CHUNK, (i + 1) * CHUNK)
        dec_in = jnp.exp(cum)
        dec_ex = jnp.exp(cum - g[rs])
        inv = 1.0 / dec_in
        end_decay = dec_in[CHUNK - 1:CHUNK, :]
        to_end = inv * end_decay
        edec_ref[s * (tile // CHUNK) + i] = end_decay
        cols = (-kk[rs] * dec_ex, r[rs] * dec_in, b[rs] * inv, k2[rs] * inv,
                b[rs] * to_end, k2[rs] * to_end, v[rs], bonus[rs])
        for n, col in enumerate(cols):
            rw_ref[s * tile + i * CHUNK:s * tile + (i + 1) * CHUNK, n * c_w:(n + 1) * c_w] = col.astype(BF16)

    vt_ref[0, s] = _dot_nt(wnt_ref[ATTN_WIDTH:2 * ATTN_WIDTH, :], u).astype(BF16)
    gat = _dot_nt(wnt_ref[2 * ATTN_WIDTH:3 * ATTN_WIDTH, :], u)
    gat_ref[0, s] = _silu(gat).astype(BF16)


def _in_proj_kernel(x_ref, gain_ref, wnn_ref, wnt_ref, cos_ref, sin_ref, cost_ref, sint_ref, mu_ref, hsel_ref,
                    wlora_ref, dbias_ref, ibias_ref, kk_ref, ka_ref, rk_ref,
                    qt_ref, k_ref, kmean_ref, kn2_ref, vt_ref, gat_ref, rw_ref, edec_ref, gb_ref, prev_ref,
                    *, nb):
    @pl.when(pl.program_id(0) % (nb // BLOCKS_PER_TILE) == 0)
    def _():
        prev_ref[...] = jnp.zeros_like(prev_ref)

    for s in range(BLOCKS_PER_TILE):
        _in_proj_block(s, x_ref, gain_ref, wnn_ref, wnt_ref, cos_ref, sin_ref, cost_ref, sint_ref, mu_ref,
                       hsel_ref, wlora_ref, dbias_ref, ibias_ref, kk_ref, ka_ref, rk_ref,
                       qt_ref, k_ref, kmean_ref, kn2_ref, vt_ref, gat_ref, rw_ref, edec_ref, gb_ref, prev_ref)


def _in_proj(x2d, gain, wnn, wnt, cos, sin, cos_t, sin_t, mu, hsel, wlora, rw_vecs, batch, nb):
    rows, d_model = x2d.shape
    tile, per = ROW_TILE, BLOCKS_PER_TILE
    assert nb % per == 0
    nt = nb // per
    row_spec = lambda width: pl.BlockSpec((tile, width), lambda i: (i, 0))
    const = lambda shape: pl.BlockSpec(shape, lambda i: (0,) * len(shape))
    blk_t = pl.BlockSpec((1, per, ATTN_WIDTH, MOBA_BLOCK), lambda i: (i // nt, i % nt, 0, 0))
    blk_t_shape = jax.ShapeDtypeStruct((batch, nb, ATTN_WIDTH, MOBA_BLOCK), BF16)
    return pl.pallas_call(
        functools.partial(_in_proj_kernel, nb=nb),
        grid=(rows // tile,),
        in_specs=[
            row_spec(d_model),
            const((1, d_model)),
            const(wnn.shape),
            const(wnt.shape),
            pl.BlockSpec((tile, 128), lambda i: (i % nt, 0)),
            pl.BlockSpec((tile, 128), lambda i: (i % nt, 0)),
            pl.BlockSpec((per, 128, MOBA_BLOCK), lambda i: (i % nt, 0, 0)),
            pl.BlockSpec((per, 128, MOBA_BLOCK), lambda i: (i % nt, 0, 0)),
            const((1, SHIFT_WIDTH)),
            const(hsel.shape),
            const(wlora.shape),
        ] + [const((1, RWKV_WIDTH))] * len(rw_vecs),
        out_specs=[
            blk_t,
            pl.BlockSpec((ATTN_WIDTH // GROUP_WIDTH, tile, GROUP_WIDTH), lambda i: (0, i, 0)),
            pl.BlockSpec((per, 1, ATTN_WIDTH), lambda i: (i, 0, 0)),
            pl.BlockSpec((per, 1, 128), lambda i: (i, 0, 0)),
            blk_t,
            blk_t,
            row_spec(RW_OPERANDS * RWKV_WIDTH),
            pl.BlockSpec((tile // CHUNK, 1, RWKV_WIDTH), lambda i: (i, 0, 0)),
            row_spec(RWKV_WIDTH),
        ],
        out_shape=[
            blk_t_shape,
            jax.ShapeDtypeStruct((ATTN_WIDTH // GROUP_WIDTH, rows, GROUP_WIDTH), BF16),
            jax.ShapeDtypeStruct((rows // MOBA_BLOCK, 1, ATTN_WIDTH), F32),
            jax.ShapeDtypeStruct((rows // MOBA_BLOCK, 1, 128), F32),
            blk_t_shape,
            blk_t_shape,
            jax.ShapeDtypeStruct((rows, RW_OPERANDS * RWKV_WIDTH), BF16),
            jax.ShapeDtypeStruct((rows // CHUNK, 1, RWKV_WIDTH), F32),
            jax.ShapeDtypeStruct((rows, RWKV_WIDTH), BF16),
        ],
        scratch_shapes=[pltpu.VMEM((8, SHIFT_WIDTH), F32)],
        compiler_params=pltpu.CompilerParams(
            dimension_semantics=("arbitrary",), vmem_limit_bytes=VMEM_LIMIT_BYTES),
        name="in_proj",
    )(x2d, gain, wnn, wnt, cos, sin, cos_t, sin_t, mu, hsel, wlora, *rw_vecs)


def _moba_kernel(qta_ref, qtb_ref, k_ref, kmean_ref, kn2_ref, vt_ref, gata_ref, gatb_ref, outa_ref, outb_ref,
                 qh_ref, bias_ref, m_ref, acc_ref, *, nb):
    j = pl.program_id(2)
    blk = MOBA_BLOCK
    tiles = ((qta_ref, gata_ref, outa_ref, j), (qtb_ref, gatb_ref, outb_ref, nb - 1 - j))
    feat = lax.broadcasted_iota(jnp.int32, (GROUP_WIDTH, 1), 0)
    head_of_feat = (feat % 128) // HALF
    ones_rows = jnp.ones((ACC_ROWS - HEAD_DIM, blk), BF16)
    km = kmean_ref[0]
    km_lane_head = (lax.broadcasted_iota(jnp.int32, (1, GROUP_WIDTH), 1) % 128) // HALF
    km_heads = jnp.concatenate([jnp.where(km_lane_head == h, km, 0.0) for h in range(GROUP_HEADS)], axis=0)
    km_hi, km_lo = _split_bf16(km_heads)
    blk_id = lax.broadcasted_iota(jnp.int32, (GROUP_HEADS, nb, blk), 1)
    k_pos = lax.broadcasted_iota(jnp.int32, (blk, blk), 0)
    q_pos = lax.broadcasted_iota(jnp.int32, (blk, blk), 1)
    causal = k_pos <= q_pos
    kn2 = kn2_ref[0]
    kn2_row = lax.broadcasted_iota(jnp.int32, kn2.shape, 0)
    kn2_lane = lax.broadcasted_iota(jnp.int32, (1, kn2.shape[1]), 1)
    first_head = pl.program_id(1) * GROUP_HEADS

    def values_ext(ki, h):
        return jnp.concatenate([vt_ref[0, ki, h * HEAD_DIM:(h + 1) * HEAD_DIM, :], ones_rows], axis=0)

    bounds = []
    for t, (qt_ref, _, _, qi) in enumerate(tiles):
        qt = qt_ref[0, 0]
        q_sq = qt.astype(F32)
        q_sq = q_sq * q_sq
        kn2_seen = jnp.max(jnp.where(kn2_row <= qi, kn2, 0.0), axis=0, keepdims=True)
        for h in range(GROUP_HEADS):
            qn2 = (jnp.sum(q_sq[h * HALF:(h + 1) * HALF], axis=0, keepdims=True)
                   + jnp.sum(q_sq[128 + h * HALF:128 + (h + 1) * HALF], axis=0, keepdims=True))
            kn2_h = jnp.sum(jnp.where(kn2_lane == first_head + h, kn2_seen, 0.0), axis=1, keepdims=True)
            bounds.append(jnp.sqrt(kn2_h * qn2) * SCORE_BOUND_MARGIN)
            qh_ref[t, h] = jnp.where(head_of_feat == h, qt, jnp.zeros_like(qt))
        gate = (_dot(km_hi, qt) + _dot(km_lo, qt)).reshape(GROUP_HEADS, nb, blk)
        gate = jnp.where(blk_id < qi, gate, -jnp.inf)
        bias = jnp.full(gate.shape, MASKED, F32)
        for r in range(min(MOBA_TOPK, nb)):
            top = jnp.max(gate, axis=1, keepdims=True)
            idx = jnp.min(jnp.where(gate == top, blk_id, nb), axis=1, keepdims=True)
            pick = blk_id == idx + jnp.where(r < qi, 0, nb)
            bias = jnp.where(pick, 0.0, bias)
            gate = jnp.where(pick, -jnp.inf, gate)
        bias_ref[t] = bias

    def issue_scores(t, ki):
        kb = k_ref[0, 0, pl.ds(pl.multiple_of(ki * blk, blk), blk), :]
        return [_dot(kb, qh_ref[t, h]) for h in range(GROUP_HEADS)]

    def finish_own(t, ki, scores):
        for h in range(GROUP_HEADS):
            s = jnp.where(causal, scores[h], MASKED)
            m = jnp.max(s, axis=0, keepdims=True)
            p = jnp.exp2(s - m).astype(BF16)
            m_ref[t, h] = jnp.broadcast_to(m, (STAT_ROWS, blk))
            acc_ref[t, h] = _dot(values_ext(ki, h), p)

    def finish_past(t, ki, scores):
        for h in range(GROUP_HEADS):
            s = scores[h]
            bias = bias_ref[t, h, pl.ds(ki, 1), :]
            m_old = m_ref[t, h, 0:1, :]
            m_new = jnp.maximum(m_old, jnp.max(s, axis=0, keepdims=True) + bias)
            alpha = jnp.exp2(m_old - m_new)
            p = jnp.exp2(s - (m_new - bias)).astype(BF16)
            m_ref[t, h] = jnp.broadcast_to(m_new, (STAT_ROWS, blk))
            acc_ref[t, h] = alpha * acc_ref[t, h] + _dot(values_ext(ki, h), p)

    def bounded_own(t, ki, scores):
        for h in range(GROUP_HEADS):
            s = jnp.where(causal, scores[h], MASKED)
            p = jnp.exp2(s - m_ref[t, h, 0:1, :]).astype(BF16)
            acc_ref[t, h] = _dot(values_ext(ki, h), p)

    def bounded_past(t, ki, scores):
        for h in range(GROUP_HEADS):
            bias = bias_ref[t, h, pl.ds(ki, 1), :]
            p = jnp.exp2(scores[h] - (m_ref[t, h, 0:1, :] - bias)).astype(BF16)
            acc_ref[t, h] = acc_ref[t, h] + _dot(values_ext(ki, h), p)

    def attend(own, past):
        work = [(own, 0, j), (own, 1, nb - 1 - j)]
        for n in range(nb - 1):
            second = (n >= j).astype(jnp.int32)
            work.append((past, second, n - second * j))
        pending = None
        for fin, t, ki in work:
            scores = issue_scores(t, ki)
            if pending is not None:
                pending[0](*pending[1:])
            pending = (fin, t, ki, scores)
        pending[0](*pending[1:])

    bound_ok = jnp.max(jnp.concatenate(bounds, axis=0)) < SCORE_BOUND_SAFE

    @pl.when(bound_ok)
    def _():
        for i, bound in enumerate(bounds):
            m_ref[i // GROUP_HEADS, i % GROUP_HEADS] = jnp.broadcast_to(bound, (STAT_ROWS, blk))
        attend(bounded_own, bounded_past)

    @pl.when(jnp.logical_not(bound_ok))
    def _():
        attend(finish_own, finish_past)

    for t, (_, gat_ref, out_ref, _) in enumerate(tiles):
        for h in range(GROUP_HEADS):
            rows = slice(h * HEAD_DIM, (h + 1) * HEAD_DIM)
            acc = acc_ref[t, h]
            y = acc[0:HEAD_DIM] / acc[HEAD_DIM:HEAD_DIM + 1]
            out_ref[0, 0, rows, :] = (y * gat_ref[0, 0, rows, :].astype(F32)).astype(BF16)


def _moba(qt, k, kmean, kn2, vt, gat, batch, seq):
    nb = seq // MOBA_BLOCK
    assert nb % 2 == 0
    blk = MOBA_BLOCK
    groups = ATTN_WIDTH // GROUP_WIDTH
    k4 = k.reshape(groups, batch, seq, GROUP_WIDTH)
    km3 = kmean.reshape(batch, nb, ATTN_WIDTH)
    kn3 = kn2.reshape(batch, nb, 128)
    tile_a = pl.BlockSpec((1, 1, GROUP_WIDTH, blk), lambda b, g, j: (b, j, g, 0))
    tile_b = pl.BlockSpec((1, 1, GROUP_WIDTH, blk), lambda b, g, j: (b, nb - 1 - j, g, 0))
    out_shape = jax.ShapeDtypeStruct((batch, nb // 2, ATTN_WIDTH, blk), BF16)
    return pl.pallas_call(
        functools.partial(_moba_kernel, nb=nb),
        grid=(batch, groups, nb // 2),
        in_specs=[
            tile_a,
            tile_b,
            pl.BlockSpec((1, 1, seq, GROUP_WIDTH), lambda b, g, j: (g, b, 0, 0)),
            pl.BlockSpec((1, nb, GROUP_WIDTH), lambda b, g, j: (b, 0, g)),
            pl.BlockSpec((1, nb, 128), lambda b, g, j: (b, 0, 0)),
            pl.BlockSpec((1, nb, GROUP_WIDTH, blk), lambda b, g, j: (b, 0, g, 0)),
            tile_a,
            tile_b,
        ],
        out_specs=[tile_a, tile_a],
        out_shape=[out_shape, out_shape],
        scratch_shapes=[
            pltpu.VMEM((2, GROUP_HEADS, GROUP_WIDTH, blk), BF16),
            pltpu.VMEM((2, GROUP_HEADS, nb, blk), F32),
            pltpu.VMEM((2, GROUP_HEADS, STAT_ROWS, blk), F32),
            pltpu.VMEM((2, GROUP_HEADS, ACC_ROWS, blk), F32),
        ],
        compiler_params=pltpu.CompilerParams(
            dimension_semantics=("arbitrary", "arbitrary", "arbitrary"), vmem_limit_bytes=VMEM_LIMIT_BYTES),
        name="moba",
    )(qt, qt, k4, km3, kn3, vt, gat, gat)


def _block_diag(x_bf16, mask):
    tiled = jnp.concatenate([x_bf16] * GROUP_HEADS, axis=0)
    return jnp.where(mask, tiled, jnp.zeros_like(tiled))


def _rwkv_kernel(rw_ref, edec_ref, gb_ref, gng_ref, gnb_ref, yat1_ref, yat2_ref, x_ref, woa_ref, wob_ref,
                 gain_ref, out_ref, state_ref, y_ref, tinv_ref, *, nb):
    tile = rw_ref.shape[0]
    c_w = RWKV_WIDTH

    @pl.when(pl.program_id(1) == 0)
    def _():
        state_ref[...] = jnp.zeros_like(state_ref)

    gr = lax.broadcasted_iota(jnp.int32, (GROUP_WIDTH, GROUP_WIDTH), 0) // HEAD_DIM
    gc = lax.broadcasted_iota(jnp.int32, (GROUP_WIDTH, GROUP_WIDTH), 1) // HEAD_DIM
    diag_mask = gr == gc
    t_idx = lax.broadcasted_iota(jnp.int32, (CHUNK, GROUP_WIDTH), 0)
    j_idx = lax.broadcasted_iota(jnp.int32, (CHUNK, GROUP_WIDTH), 1) % HEAD_DIM
    strict_lower = j_idx < t_idx
    lower = j_idx <= t_idx
    eye = (j_idx == t_idx).astype(F32)

    def below_diag(size):
        return ((t_idx // (2 * size) == j_idx // (2 * size))
                & ((t_idx // size) % 2 == 1) & ((j_idx // size) % 2 == 0))
    lane_head = lax.broadcasted_iota(jnp.int32, (1, GROUP_WIDTH), 1) // HEAD_DIM

    def bd(x):
        return _block_diag(x.astype(BF16), diag_mask)

    def diag_blocks(full):
        out = full[(GROUP_HEADS - 1) * HEAD_DIM:, :]
        for h in range(GROUP_HEADS - 2, -1, -1):
            out = jnp.where(lane_head == h, full[h * HEAD_DIM:(h + 1) * HEAD_DIM, :], out)
        return out

    n_groups = c_w // GROUP_WIDTH
    items = []
    for c in range(tile // CHUNK):
        rs = slice(c * CHUNK, (c + 1) * CHUNK)
        end_decay = edec_ref[c]
        for gi in range(n_groups):
            ls = slice(gi * GROUP_WIDTH, (gi + 1) * GROUP_WIDTH)
            operand = lambda n: rw_ref[rs, n * c_w + gi * GROUP_WIDTH:n * c_w + (gi + 1) * GROUP_WIDTH]
            it = dict(rs=rs, ls=ls, gi=gi)
            it["a_t"] = operand(0)
            it["r_t"] = operand(1)
            it["bk_e"] = jnp.concatenate([operand(4), operand(5)], axis=0)
            it["v_c"] = operand(6)
            it["m_diag"] = eye * end_decay[:, ls]
            ar = jnp.concatenate([it["a_t"], it["r_t"]], axis=0)
            pb = _dot_nt(ar, _block_diag(operand(2), diag_mask))
            pk = _dot_nt(ar, _block_diag(operand(3), diag_mask))
            it["n_ab"] = jnp.where(strict_lower, pb[0:CHUNK], 0.0)
            it["idx"] = len(items)
            it["a_rb"] = jnp.where(lower, pb[CHUNK:], 0.0).astype(BF16)
            it["a_k"] = jnp.concatenate([jnp.where(strict_lower, pk[0:CHUNK], 0.0),
                                         jnp.where(lower, pk[CHUNK:], 0.0)], axis=0).astype(BF16)
            items.append(it)

    def inverse_by_substitution(its):
        for it in its:
            t_inv = eye + jnp.where(below_diag(1), it["n_ab"], 0.0)
            size = 2
            while size < CHUNK:
                n21 = jnp.where(below_diag(size), it["n_ab"], 0.0)
                t_n21 = _dot(t_inv.astype(BF16), bd(n21))
                t_inv = t_inv + _dot(t_n21.astype(BF16), bd(t_inv))
                size *= 2
            tinv_ref[it["idx"]] = t_inv.astype(BF16)

    def affine_maps(its):
        for it in its:
            it["pw"] = it["n_ab"]
            it["t_inv"] = eye + it["n_ab"]
        peak = jnp.zeros((CHUNK, GROUP_WIDTH), F32)
        steps = CHUNK.bit_length() - 1
        for s in range(1, steps + 1):
            for it in its:
                w_p = bd(it["pw"])
                if s == 1:
                    it["pw"] = _dot(it["pw"].astype(BF16), w_p)
                elif s < steps:
                    both = _dot(jnp.concatenate([it["pw"], it["t_inv"]], axis=0).astype(BF16), w_p)
                    it["t_inv"] = it["t_inv"] + both[CHUNK:]
                    it["pw"] = both[0:CHUNK]
                else:
                    it["t_inv"] = it["t_inv"] + _dot(it["t_inv"].astype(BF16), w_p)
                peak = jnp.maximum(peak, jnp.abs(it["t_inv"] if s == steps else it["pw"]))
            yield
        for it in its:
            tinv_ref[it["idx"]] = it["t_inv"].astype(BF16)

        @pl.when(jnp.logical_not(jnp.max(peak) < SERIES_SAFE_PEAK))
        def _():
            inverse_by_substitution(its)

        for it in its:
            it["t_inv"] = tinv_ref[it["idx"]]
        for it in its:
            kv = _dot(it["a_k"], bd(it["v_c"]))
            it["akv"] = kv[0:CHUNK]
            it["arkv"] = kv[CHUNK:]
        yield
        for it in its:
            it["w_t"] = _dot(it["t_inv"], bd(it["a_t"])).astype(BF16)
            it["u_t"] = _dot(it["t_inv"], bd(it["akv"])).astype(BF16)
        yield
        for it in its:
            it["r_hat"] = (it["r_t"].astype(F32) + _dot(it["a_rb"], bd(it["w_t"]))).astype(BF16)
            it["y_hat"] = _dot(it["a_rb"], bd(it["u_t"])) + it["arkv"]
        yield
        for it in its:
            m_full = _dot_tn(it["bk_e"][0:CHUNK], it["w_t"])
            it["m"] = (it["m_diag"] + diag_blocks(m_full)).astype(BF16)
            n_full = _dot_tn(it["bk_e"], jnp.concatenate([it["u_t"], it["v_c"]], axis=0))
            it["n"] = diag_blocks(n_full)
        yield

    in_first = pl.program_id(1) < nb // 2
    per = yat1_ref.shape[1]
    n_chunks = tile // CHUNK
    states = [state_ref[gi] for gi in range(n_groups)]

    def advance_chunk(chunk_items):
        for it in chunk_items:
            gi = it["gi"]
            res = _dot(jnp.concatenate([it["m"], it["r_hat"]], axis=0), bd(states[gi]))
            y_ref[it["rs"], it["ls"]] = res[CHUNK:] + it["y_hat"]
            states[gi] = res[0:CHUNK] + it["n"]

    def output_rows(blk):
        rows = slice(blk * MOBA_BLOCK, (blk + 1) * MOBA_BLOCK)
        y = y_ref[rows, :]
        mean = _head_sum(y) * (1.0 / HEAD_DIM)
        d = y - mean
        var = _head_sum(d * d) * (1.0 / HEAD_DIM)
        yn = d * lax.rsqrt(var + GN_EPS) * gng_ref[...] + gnb_ref[...]
        bonus = rw_ref[rows, (RW_OPERANDS - 1) * c_w:].astype(F32)
        yb = ((yn + bonus) * gb_ref[rows, :].astype(F32)).astype(BF16)
        yat_t = jnp.transpose(jnp.where(in_first, yat1_ref[0, blk], yat2_ref[0, per - 1 - blk]))
        pieces = []
        for c0 in range(0, woa_ref.shape[1], GROUP_WIDTH):
            cols = slice(c0, c0 + GROUP_WIDTH)
            pieces.append(x_ref[rows, cols] + _dot(yat_t, woa_ref[:, cols]) + _dot(yb, wob_ref[:, cols]))
            yield
        h = jnp.concatenate(pieces, axis=-1)
        ms = jnp.mean(h * h, axis=-1, keepdims=True)
        out_ref[rows, :] = (h * lax.rsqrt(ms + RMS_EPS) * gain_ref[...]).astype(out_ref.dtype)

    assert per == 2 and n_chunks % 2 == 0
    half = (n_chunks // 2) * n_groups
    first, second = items[:half], items[half:]
    for _ in affine_maps(first):
        pass
    first_chunks = [first[c:c + n_groups] for c in range(0, len(first), n_groups)]
    for _ in affine_maps(second):
        if first_chunks:
            advance_chunk(first_chunks.pop(0))
    for chunk_items in first_chunks:
        advance_chunk(chunk_items)
    first_out = output_rows(0)
    for c in range(0, len(second), n_groups):
        advance_chunk(second[c:c + n_groups])
        next(first_out, None)
    for _ in first_out:
        pass
    for gi in range(n_groups):
        state_ref[gi] = states[gi]
    for _ in output_rows(1):
        pass


def _rwkv(rw, edec, gb, gn_gain, gn_bias, yat_first, yat_second, x2d, woa, wob, out_gain, batch, seq):
    tile, per = ROW_TILE, BLOCKS_PER_TILE
    nt = seq // tile
    assert nt % 2 == 0
    half = nt // 2
    d_model = x2d.shape[1]
    row_spec = lambda width: pl.BlockSpec((tile, width), lambda b, i: (b * nt + i, 0))
    const = lambda shape: pl.BlockSpec(shape, lambda b, i: (0,) * len(shape))
    vec = lambda width: const((1, width))
    blk_t = lambda pick: pl.BlockSpec((1, per, ATTN_WIDTH, MOBA_BLOCK),
                                      lambda b, i: (b, jnp.minimum(pick(i), half - 1), 0, 0))
    return pl.pallas_call(
        functools.partial(_rwkv_kernel, nb=nt),
        grid=(batch, nt),
        in_specs=[
            row_spec(RW_OPERANDS * RWKV_WIDTH),
            pl.BlockSpec((tile // CHUNK, 1, RWKV_WIDTH), lambda b, i: (b * nt + i, 0, 0)),
            row_spec(RWKV_WIDTH),
            vec(RWKV_WIDTH),
            vec(RWKV_WIDTH),
            blk_t(lambda i: i),
            blk_t(lambda i: nt - 1 - i),
            row_spec(d_model),
            const((ATTN_WIDTH, d_model)),
            const((RWKV_WIDTH, d_model)),
            vec(d_model),
        ],
        out_specs=row_spec(d_model),
        out_shape=jax.ShapeDtypeStruct(x2d.shape, x2d.dtype),
        scratch_shapes=[
            pltpu.VMEM((RWKV_WIDTH // GROUP_WIDTH, HEAD_DIM, GROUP_WIDTH), F32),
            pltpu.VMEM((tile, RWKV_WIDTH), F32),
            pltpu.VMEM((tile // CHUNK * (RWKV_WIDTH // GROUP_WIDTH), CHUNK, GROUP_WIDTH), BF16),
        ],
        compiler_params=pltpu.CompilerParams(
            dimension_semantics=("arbitrary", "arbitrary"), vmem_limit_bytes=VMEM_LIMIT_BYTES),
        name="rwkv_out",
    )(rw, edec, gb, gn_gain, gn_bias, yat_first, yat_second, x2d, woa, wob, out_gain)


def _rotary_column_order():
    order = []
    for g in range(ATTN_WIDTH // GROUP_WIDTH):
        for part in range(2):
            for hh in range(GROUP_HEADS):
                head = g * GROUP_HEADS + hh
                order.extend(head * HEAD_DIM + part * HALF + j for j in range(HALF))
    return np.asarray(order, np.int32)


def _layer(h2d, batch, seq, norm_gain, w_in, shift_mu, decay_bias, decay_up, iclr_bias, iclr_up,
           k_k, k_a, r_k, gn_gain, gn_bias, w_out, out_gain, cos, sin):
    nb = seq // MOBA_BLOCK
    aw, rw = ATTN_WIDTH, RWKV_WIDTH
    order = _rotary_column_order()
    b0 = 4 * aw
    w_in = w_in.astype(BF16)
    wnn = jnp.concatenate([w_in[:, aw:2 * aw][:, order], w_in[:, b0:]], axis=1)
    wnt = jnp.concatenate([w_in[:, 0:aw][:, order], w_in[:, 2 * aw:4 * aw]], axis=1).T
    zeros = jnp.zeros((LORA_RANK, rw), F32)
    wl = jnp.concatenate([jnp.concatenate([decay_up.astype(F32), zeros], axis=1),
                          jnp.concatenate([zeros, iclr_up.astype(F32)], axis=1)], axis=0)
    wlora = wl.astype(BF16)
    row = lambda t: t.astype(F32).reshape(1, -1)

    cos_t = cos.reshape(nb, MOBA_BLOCK, 128).transpose(0, 2, 1)
    sin_t = sin.reshape(nb, MOBA_BLOCK, 128).transpose(0, 2, 1)
    lane = np.arange(aw)
    head_of_lane = (lane // GROUP_WIDTH) * GROUP_HEADS + (lane % 128) // HALF
    hsel = jnp.asarray(head_of_lane[:, None] == np.arange(128)[None, :], BF16)
    rw_vecs = [row(decay_bias), row(iclr_bias), row(k_k), row(k_a), row(r_k)]
    qt, k, kmean, kn2, vt, gat, rw, edec, gb = _in_proj(h2d, row(norm_gain), wnn, wnt, cos, sin, cos_t, sin_t,
                                                        row(shift_mu), hsel, wlora, rw_vecs, batch, nb)
    yat_first, yat_second = _moba(qt, k, kmean, kn2, vt, gat, batch, seq)
    w_out = w_out.astype(BF16)
    return _rwkv(rw, edec, gb, row(gn_gain), row(gn_bias), yat_first, yat_second, h2d, w_out[0:aw], w_out[aw:],
                 out_gain, batch, seq)


def kernel(x, norm_gain, w_in, shift_mu, decay_bias, decay_up, iclr_bias, iclr_up,
           k_k, k_a, r_k, gn_gain, gn_bias, w_out, final_gain):
    batch, seq, d_model = x.shape
    depth = norm_gain.shape[0]
    assert depth == 1, "the final RMSNorm is fused into the single layer's output projection"
    assert seq % MOBA_BLOCK == 0 and w_in.shape[-1] == 4 * ATTN_WIDTH + SHIFT_WIDTH + RWKV_WIDTH
    inv_freq = 1.0 / (ROPE_THETA ** (jnp.arange(0, HEAD_DIM, 2, dtype=F32) / HEAD_DIM))
    ang = jnp.arange(seq, dtype=F32)[:, None] * inv_freq[None, :]
    cos = jnp.tile(jnp.cos(ang), (1, GROUP_HEADS))
    sin = jnp.tile(jnp.sin(ang), (1, GROUP_HEADS))
    out = _layer(x.reshape(batch * seq, d_model).astype(F32), batch, seq, norm_gain[0], w_in[0],
                 shift_mu[0], decay_bias[0], decay_up[0], iclr_bias[0], iclr_up[0], k_k[0], k_a[0], r_k[0],
                 gn_gain[0], gn_bias[0], w_out[0], final_gain.astype(F32).reshape(1, -1), cos, sin)
    return out.reshape(batch, seq, d_model).astype(x.dtype)
```
